```python
import jax, jax.numpy as jnp
from jax import lax
import numpy as np

D_MODEL = 1024
BATCH = 32
SEQ = 256
DEPTH = 2
DEC_BATCH = 2
DEC_SEQ = 4096
PAST_LEN = 512

GRID_W = 64
N_DIR = 2
EPS = 1e-6
N_EVEN = (DEPTH + 1) // 2
N_ODD = DEPTH // 2

D_A = D_MODEL // 2
H_A = 8
BLK_A = D_A // H_A
CONV_W = 4
CONV_LO = 1
C_RGLRU = 8.0

D_B = D_MODEL // 2
HEAD_B = 64
H_B = D_B // HEAD_B
LORA_W = 64
LORA_A = 64
LORA_G = 128
GN_EPS_B = 64e-5
SPLIT_B = [D_B, 2 * D_B, 3 * D_B, 3 * D_B + N_DIR * LORA_W, 3 * D_B + N_DIR * (LORA_W + LORA_A)]
N_B_COLS = 3 * D_B + N_DIR * (LORA_W + LORA_A) + LORA_G
N_EVEN_COLS = 2 * D_A + N_B_COLS

H_C = 4
DK_C = D_MODEL // 2 // H_C
DV_C = D_MODEL // H_C
GATE_RANK = 16
GATE_NORM = 16.0
CHUNK = 64
SPLIT_C = [H_C * DK_C, 2 * H_C * DK_C, 2 * H_C * DK_C + H_C * DV_C, 2 * H_C * DK_C + 2 * H_C * DV_C]
N_ODD_COLS = 2 * H_C * DK_C + 2 * H_C * DV_C + N_DIR * GATE_RANK

N_EXPERTS = 16
D_FF = 2048
CAP_FACTOR = 2

kernel_name = 'bidir_hybrid_rglru_rwkv7_gla_ecmoe_step'

F32 = jnp.float32


def rmsnorm(x, g):
    xf = x.astype(F32)
    y = xf * lax.rsqrt(jnp.mean(xf * xf, -1, keepdims=True) + EPS)
    return (y * g.astype(F32)).astype(x.dtype)


def dirs_shared(t):
    return jnp.stack([t, jnp.flip(t, 1)], axis=1)


def dirs_split(t):
    return jnp.stack([t[:, :, 0], jnp.flip(t[:, :, 1], 1)], axis=1)


def merge_dirs(t):
    return t[:, 0] + jnp.flip(t[:, 1], 1)


def grid_to_colmajor(t):
    b, n, ch = t.shape
    rows = n // GRID_W
    return t.reshape(b, rows, GRID_W, ch).transpose(0, 2, 1, 3).reshape(b, n, ch)


def colmajor_to_grid(t):
    b, n, ch = t.shape
    rows = n // GRID_W
    return t.reshape(b, GRID_W, rows, ch).transpose(0, 2, 1, 3).reshape(b, n, ch)


def depthwise_conv_centred(x, w, b):
    y = lax.conv_general_dilated(x, w[:, None, :].astype(x.dtype), (1,), [(CONV_LO, CONV_W - 1 - CONV_LO)],
                                 dimension_numbers=('NWC', 'WIO', 'NWC'), feature_group_count=x.shape[-1])
    return y + b


def centred_shift(t):
    z = jnp.zeros_like(t[:, :1])
    return 0.5 * (jnp.concatenate([z, t[:, :-1]], 1) + jnp.concatenate([t[:, 1:], z], 1))


def rglru_mix(xa, ga, h0, conv_w, conv_b, w_rg, b_rg, w_ig, b_ig, lam):
    bsz, n, _ = xa.shape
    u = depthwise_conv_centred(xa, conv_w, conv_b).astype(F32)
    ud = dirs_shared(u)
    ub = ud.reshape(bsz, N_DIR, n, H_A, BLK_A)
    gate_r = jax.nn.sigmoid(jnp.einsum('bdthi,dhij->bdthj', ub, w_rg).reshape(bsz, N_DIR, n, D_A)
                            + b_rg[None, :, None, :])
    gate_i = jax.nn.sigmoid(jnp.einsum('bdthi,dhij->bdthj', ub, w_ig).reshape(bsz, N_DIR, n, D_A)
                            + b_ig[None, :, None, :])
    log_a = -C_RGLRU * gate_r * jax.nn.softplus(-lam.astype(F32))[None, :, None, :]
    a = jnp.exp(log_a)
    xin = jnp.sqrt(-jnp.expm1(2.0 * log_a)) * gate_i * ud
    xin = xin.at[:, :, 0].add(a[:, :, 0] * h0.astype(F32))
    _, h = lax.associative_scan(lambda l, r: (l[0] * r[0], r[0] * l[1] + r[1]), (a, xin), axis=2)
    y = merge_dirs(h) * jax.nn.gelu(ga.astype(F32))
    return y.astype(xa.dtype), h[:, :, -1].astype(xa.dtype)


def rwkv7_mix(pb, s0, mu, w0, w2, a0, a2, g2, k_k, k_a, r_k, ln_w, ln_b):
    bsz, n, _ = pb.shape
    dtype = pb.dtype
    pb = pb.astype(F32)
    z = pb + mu * (centred_shift(pb) - pb)
    r, k, v, zw, za, zg = jnp.split(z, SPLIT_B, axis=-1)
    zw = zw.reshape(bsz, n, N_DIR, LORA_W)
    za = za.reshape(bsz, n, N_DIR, LORA_A)
    w = -jax.nn.softplus(-(w0 + jnp.einsum('btdl,dlc->btdc', jnp.tanh(zw), w2))) - 0.5
    decay = jnp.exp(-jnp.exp(w))
    a = jax.nn.sigmoid(a0 + jnp.einsum('btdl,dlc->btdc', za, a2))
    g = jax.nn.sigmoid(zg) @ g2
    heads = lambda t: t.reshape(t.shape[:-1] + (H_B, HEAD_B))
    kk = heads(k * k_k)
    kk = kk / jnp.maximum(jnp.sqrt(jnp.sum(kk * kk, -1, keepdims=True)), 1e-12)
    kd = k[:, :, None] * (1 + (a - 1) * k_a)
    seqs = (dirs_shared(heads(r)), dirs_split(heads(decay)), dirs_shared(kk),
            dirs_split(heads(a)), dirs_split(heads(kd)), dirs_shared(heads(v)))

    def step(S, inp):
        r_t, w_t, kk_t, a_t, k_t, v_t = inp
        sa = jnp.einsum('bdhij,bdhj->bdhi', S, -kk_t)
        S = (S * w_t[..., None, :] + sa[..., :, None] * (kk_t * a_t)[..., None, :]
             + v_t[..., :, None] * k_t[..., None, :])
        return S, jnp.einsum('bdhij,bdhj->bdhi', S, r_t)

    s_fin, y = lax.scan(step, s0.astype(F32), tuple(jnp.moveaxis(t, 2, 0) for t in seqs))
    y = merge_dirs(jnp.moveaxis(y, 0, 2))
    mean = jnp.mean(y, -1, keepdims=True)
    var = jnp.mean(jnp.square(y - mean), -1, keepdims=True)
    y = ((y - mean) * lax.rsqrt(var + GN_EPS_B)).reshape(bsz, n, D_B) * ln_w + ln_b
    bonus = jnp.einsum('bthn,btdhn->bth', heads(r * r_k), heads(kd))[..., None] * heads(v)
    y = (y + bonus.reshape(bsz, n, D_B)) * g
    return y.astype(dtype), s_fin.astype(dtype)


def gla_mix(p, s0, w_gk_up, b_gk, gnorm_w):
    bsz, n, _ = p.shape
    dtype = p.dtype
    p = p.astype(F32)
    q, k, v, g, gk = jnp.split(p, SPLIT_C, axis=-1)
    q = q.reshape(bsz, n, H_C, DK_C) * DK_C ** -0.5
    k = k.reshape(bsz, n, H_C, DK_C)
    v = v.reshape(bsz, n, H_C, DV_C)
    gk = jnp.einsum('btdr,drk->btdk', gk.reshape(bsz, n, N_DIR, GATE_RANK), w_gk_up) + b_gk
    log_alpha = (jax.nn.log_sigmoid(gk) / GATE_NORM).reshape(bsz, n, N_DIR, H_C, DK_C)
    nc = n // CHUNK
    chunks = lambda t: t.reshape((bsz, N_DIR, nc, CHUNK) + t.shape[3:])
    qd, kd, vd = chunks(dirs_shared(q)), chunks(dirs_shared(k)), chunks(dirs_shared(v))
    bcum = jnp.cumsum(chunks(dirs_split(log_alpha)), axis=3)
    blast = bcum[:, :, :, -1]
    q_t = qd * jnp.exp(bcum)
    k_t = kd * jnp.exp(-bcum)
    k_end = kd * jnp.exp(blast[:, :, :, None] - bcum)
    causal = jnp.tril(jnp.ones((CHUNK, CHUNK), bool))
    att = jnp.where(causal, jnp.einsum('bdcthk,bdcshk->bdchts', q_t, k_t), 0.0)
    o_intra = jnp.einsum('bdchts,bdcshv->bdcthv', att, vd)

    def step(S, inp):
        qc, kc, vc, dc = inp
        o = jnp.einsum('bdthk,bdhkv->bdthv', qc, S)
        S = S * jnp.exp(dc)[..., None] + jnp.einsum('bdthk,bdthv->bdhkv', kc, vc)
        return S, o

    cm = lambda t: jnp.moveaxis(t, 2, 0)
    s_fin, o_inter = lax.scan(step, s0.astype(F32), (cm(q_t), cm(k_end), cm(vd), cm(blast)))
    o = o_intra + jnp.moveaxis(o_inter, 0, 2)
    o = merge_dirs(o.reshape(bsz, N_DIR, n, H_C, DV_C))
    o = o * lax.rsqrt(jnp.mean(o * o, -1, keepdims=True) + EPS) * gnorm_w
    y = o.reshape(bsz, n, H_C * DV_C) * jax.nn.silu(g)
    return y.astype(dtype), s_fin.astype(dtype)


def ec_moe(h, router_w, w_gate, w_up, w_down):
    bsz, n, d = h.shape
    ntok = bsz * n
    xf = h.reshape(ntok, d)
    aff = jax.nn.softmax(xf.astype(F32) @ router_w.astype(F32), axis=-1)
    cap = max(1, CAP_FACTOR * ntok // N_EXPERTS)
    gsel, idx = lax.top_k(aff.T, cap)
    xe = xf[idx]
    hid = jax.nn.silu(jnp.einsum('ecd,edf->ecf', xe, w_gate)) * jnp.einsum('ecd,edf->ecf', xe, w_up)
    ye = jnp.einsum('ecf,efd->ecd', hid, w_down) * gsel[..., None].astype(h.dtype)
    out = jnp.zeros_like(xf).at[idx.reshape(-1)].add(ye.reshape(-1, d).astype(xf.dtype))
    return out.reshape(bsz, n, d)


def trunk(x, cvec, st_rglru, st_rwkv, st_gla, grid, P):
    new_rglru, new_rwkv, new_gla = [], [], []
    for l in range(DEPTH):
        m = jax.nn.silu(cvec) @ P['ada_w'][l] + P['ada_b'][l]
        sh1, sc1, g1, sh2, sc2, g2 = [m[:, None, i * D_MODEL:(i + 1) * D_MODEL] for i in range(6)]
        h = rmsnorm(x, P['norm1_g'][l]) * (1 + sc1) + sh1
        if l % 2 == 0:
            e = l // 2
            p = h @ P['ev_w_in'][e]
            ya, s_a = rglru_mix(p[..., :D_A], p[..., D_A:2 * D_A], st_rglru[:, e],
                                P['a_conv_w'][e], P['a_conv_b'][e], P['a_w_rg'][e], P['a_b_rg'][e],
                                P['a_w_ig'][e], P['a_b_ig'][e], P['a_lam'][e])
            yb, s_b = rwkv7_mix(p[..., 2 * D_A:], st_rwkv[:, e], P['b_mu'][e], P['b_w0'][e], P['b_w2'][e],
                                P['b_a0'][e], P['b_a2'][e], P['b_g2'][e], P['b_k_k'][e], P['b_k_a'][e],
                                P['b_r_k'][e], P['b_ln_w'][e], P['b_ln_b'][e])
            y = jnp.concatenate([ya, yb], -1) @ P['ev_w_out'][e]
            new_rglru.append(s_a)
            new_rwkv.append(s_b)
        else:
            o = l // 2
            p = h @ P['od_w_in'][o]
            if grid:
                p = grid_to_colmajor(p)
            yc, s_c = gla_mix(p, st_gla[:, o], P['c_w_gk_up'][o], P['c_b_gk'][o], P['c_gnorm_w'][o])
            if grid:
                yc = colmajor_to_grid(yc)
            y = yc @ P['od_w_out'][o]
            new_gla.append(s_c)
        x = x + g1 * y
        h = rmsnorm(x, P['norm2_g'][l]) * (1 + sc2) + sh2
        x = x + g2 * ec_moe(h, P['router_w'][l], P['moe_w_gate'][l], P['moe_w_up'][l], P['moe_w_down'][l])
    return rmsnorm(x, P['final_norm_g']), jnp.stack(new_rglru, 1), jnp.stack(new_rwkv, 1), jnp.stack(new_gla, 1)


def setup_inputs(seed: int = 0) -> dict:
    key = jax.random.key(seed)
    ks = iter(jax.random.split(key, 64))

    def nrm(shape, scale=1.0):
        return jax.random.normal(next(ks), shape, jnp.float32) * scale

    def gain(shape):
        return 1.0 + 0.05 * nrm(shape)

    u = jax.random.uniform(next(ks), (N_EVEN, N_DIR, D_A), jnp.float32, 0.9, 0.999)
    a_base = u ** (1.0 / C_RGLRU)
    ramp = jnp.linspace(-6.5, -1.5, D_B, dtype=jnp.float32)
    return {
        'x_prompt': nrm((BATCH, SEQ, D_MODEL)),
        'x_sample': nrm((DEC_BATCH, DEC_SEQ, D_MODEL)),
        'state_rglru': nrm((DEC_BATCH, N_EVEN, N_DIR, D_A)),
        'state_rwkv': nrm((DEC_BATCH, N_EVEN, N_DIR, H_B, HEAD_B, HEAD_B)),
        'state_gla': nrm((DEC_BATCH, N_ODD, N_DIR, H_C, DK_C, DV_C)),
        'c': nrm((DEC_BATCH, D_MODEL)),
        'c_ctx': nrm((D_MODEL,)),
        'norm1_g': gain((DEPTH, D_MODEL)),
        'norm2_g': gain((DEPTH, D_MODEL)),
        'ada_w': nrm((DEPTH, D_MODEL, 6 * D_MODEL), D_MODEL ** -0.5),
        'ada_b': nrm((DEPTH, 6 * D_MODEL), 0.01),
        'router_w': nrm((DEPTH, D_MODEL, N_EXPERTS), D_MODEL ** -0.5),
        'moe_w_gate': nrm((DEPTH, N_EXPERTS, D_MODEL, D_FF), D_MODEL ** -0.5),
        'moe_w_up': nrm((DEPTH, N_EXPERTS, D_MODEL, D_FF), D_MODEL ** -0.5),
        'moe_w_down': nrm((DEPTH, N_EXPERTS, D_FF, D_MODEL), D_FF ** -0.5),
        'ev_w_in': nrm((N_EVEN, D_MODEL, N_EVEN_COLS), D_MODEL ** -0.5),
        'ev_w_out': nrm((N_EVEN, D_A + D_B, D_MODEL), (D_A + D_B) ** -0.5),
        'a_conv_w': nrm((N_EVEN, CONV_W, D_A), CONV_W ** -0.5),
        'a_conv_b': nrm((N_EVEN, D_A), 0.01),
        'a_w_rg': nrm((N_EVEN, N_DIR, H_A, BLK_A, BLK_A), BLK_A ** -0.5),
        'a_b_rg': nrm((N_EVEN, N_DIR, D_A), 0.01),
        'a_w_ig': nrm((N_EVEN, N_DIR, H_A, BLK_A, BLK_A), BLK_A ** -0.5),
        'a_b_ig': nrm((N_EVEN, N_DIR, D_A), 0.01),
        'a_lam': jnp.log(a_base) - jnp.log1p(-a_base),
        'b_mu': jax.random.uniform(next(ks), (N_EVEN, N_B_COLS), jnp.float32),
        'b_w0': ramp + 0.1 * nrm((N_EVEN, N_DIR, D_B)),
        'b_w2': nrm((N_EVEN, N_DIR, LORA_W, D_B), LORA_W ** -0.5),
        'b_a0': nrm((N_EVEN, N_DIR, D_B), 0.1),
        'b_a2': nrm((N_EVEN, N_DIR, LORA_A, D_B), LORA_A ** -0.5),
        'b_g2': nrm((N_EVEN, LORA_G, D_B), LORA_G ** -0.5),
        'b_k_k': gain((N_EVEN, D_B)),
        'b_k_a': gain((N_EVEN, D_B)),
        'b_r_k': nrm((N_EVEN, D_B), 0.1),
        'b_ln_w': gain((N_EVEN, D_B)),
        'b_ln_b': nrm((N_EVEN, D_B), 0.01),
        'od_w_in': nrm((N_ODD, D_MODEL, N_ODD_COLS), D_MODEL ** -0.5),
        'od_w_out': nrm((N_ODD, H_C * DV_C, D_MODEL), (H_C * DV_C) ** -0.5),
        'c_w_gk_up': nrm((N_ODD, N_DIR, GATE_RANK, H_C * DK_C), GATE_RANK ** -0.5),
        'c_b_gk': nrm((N_ODD, N_DIR, H_C * DK_C), 0.1),
        'c_gnorm_w': gain((N_ODD, DV_C)),
        'final_norm_g': gain((D_MODEL,)),
    }


def reference(x_prompt, x_sample, state_rglru, state_rwkv, state_gla, c, c_ctx,
              norm1_g, norm2_g, ada_w, ada_b, router_w, moe_w_gate, moe_w_up, moe_w_down,
              ev_w_in, ev_w_out, a_conv_w, a_conv_b, a_w_rg, a_b_rg, a_w_ig, a_b_ig, a_lam,
              b_mu, b_w0, b_w2, b_a0, b_a2, b_g2, b_k_k, b_k_a, b_r_k, b_ln_w, b_ln_b,
              od_w_in, od_w_out, c_w_gk_up, c_b_gk, c_gnorm_w, final_norm_g):
    P = dict(norm1_g=norm1_g, norm2_g=norm2_g, ada_w=ada_w, ada_b=ada_b, router_w=router_w,
             moe_w_gate=moe_w_gate, moe_w_up=moe_w_up, moe_w_down=moe_w_down,
             ev_w_in=ev_w_in, ev_w_out=ev_w_out, a_conv_w=a_conv_w, a_conv_b=a_conv_b,
             a_w_rg=a_w_rg, a_b_rg=a_b_rg, a_w_ig=a_w_ig, a_b_ig=a_b_ig, a_lam=a_lam,
             b_mu=b_mu, b_w0=b_w0, b_w2=b_w2, b_a0=b_a0, b_a2=b_a2, b_g2=b_g2, b_k_k=b_k_k,
             b_k_a=b_k_a, b_r_k=b_r_k, b_ln_w=b_ln_w, b_ln_b=b_ln_b,
             od_w_in=od_w_in, od_w_out=od_w_out, c_w_gk_up=c_w_gk_up, c_b_gk=c_b_gk,
             c_gnorm_w=c_gnorm_w, final_norm_g=final_norm_g)
    n_req = x_prompt.shape[0]
    dt = x_prompt.dtype
    z_rglru = jnp.zeros((n_req, N_EVEN, N_DIR, D_A), dt)
    z_rwkv = jnp.zeros((n_req, N_EVEN, N_DIR, H_B, HEAD_B, HEAD_B), dt)
    z_gla = jnp.zeros((n_req, N_ODD, N_DIR, H_C, DK_C, DV_C), dt)
    y_prompt, new_rglru, new_rwkv, new_gla = trunk(x_prompt, c_ctx[None, :], z_rglru, z_rwkv, z_gla, False, P)
    y_sample, _, _, _ = trunk(x_sample, c, state_rglru, state_rwkv, state_gla, True, P)
    return (y_prompt, y_sample, new_rglru, new_rwkv, new_gla)
```

```python
import functools
import math

import numpy as np
import jax
import jax.numpy as jnp
from jax import lax
from jax.experimental import pallas as pl
from jax.experimental.pallas import tpu as pltpu

F32 = jnp.float32
BF16 = jnp.bfloat16
HI = lax.Precision.HIGHEST

GRID_W = 64
EPS = 1e-6
C_RGLRU = 8.0
GN_EPS_B = 64e-5
GATE_NORM = 16.0
CAP_FACTOR = 2
HEAD_B = 64
H_A = 8
H_C = 4
GATE_RANK = 16

TB = 256
CH = 64
SUB = 16
SLOT_B = 256
TOK_B = 256
FF_B = 512
VMEM_BIG = 56 * 1024 * 1024


def _dot(a, b, ca=1, cb=0, prec=None):
    return lax.dot_general(a, b, (((ca,), (cb,)), ((), ())), precision=prec,
                           preferred_element_type=F32)


def _dot_bf(a, b, ca=1, cb=0):
    return _dot(a.astype(BF16), b.astype(BF16), ca, cb)


def _dot_hi(a, b, ca=1, cb=0):
    return _dot(a.astype(F32), b.astype(F32), ca, cb, HI)


def _seg_sum(x, e_bf):
    hi = x.astype(BF16)
    lo = (x - hi.astype(F32)).astype(BF16)
    return _dot(hi, e_bf) + _dot(lo, e_bf)


def _sigmoid(x):
    return 1.0 / (1.0 + jnp.exp(-x))


def _softplus(x):
    return jnp.maximum(x, 0.0) + jnp.log1p(jnp.exp(-jnp.abs(x)))


def _silu(x):
    return x * _sigmoid(x)


def _gelu_tanh(x):
    return 0.5 * x * (1.0 + jnp.tanh(math.sqrt(2.0 / math.pi) * (x + 0.044715 * (x * x * x))))


def _rms(x, g):
    return x * lax.rsqrt(jnp.mean(x * x, axis=-1, keepdims=True) + EPS) * g


def _cparams(sem, vmem=None):
    return pltpu.CompilerParams(dimension_semantics=sem, vmem_limit_bytes=vmem)


def _mod_kernel(c_ref, w_ref, b_ref, o_ref):
    o_ref[...] = _dot_hi(_silu(c_ref[...]), w_ref[...]) + b_ref[...]


def _modulation(cv8, ada_w, ada_b):
    depth, d, n6 = ada_w.shape
    nb = 1536
    return pl.pallas_call(
        _mod_kernel,
        grid=(depth, n6 // nb),
        in_specs=[pl.BlockSpec((8, d), lambda l, j: (0, 0)),
                  pl.BlockSpec((None, d, nb), lambda l, j: (l, 0, j)),
                  pl.BlockSpec((None, 1, nb), lambda l, j: (l, 0, j))],
        out_specs=pl.BlockSpec((None, 8, nb), lambda l, j: (l, 0, j)),
        out_shape=jax.ShapeDtypeStruct((depth, 8, n6), F32),
        compiler_params=_cparams(("arbitrary", "arbitrary")),
        name="modulation",
    )(cv8, ada_w, ada_b.reshape(depth, 1, n6))


def _project(x, mod, g_ref, w_ref, outs, splits):
    h = _rms(x, g_ref[...]) * (1.0 + mod[1:2]) + mod[0:1]
    p = _dot_bf(h, w_ref[...])
    lo = 0
    for o_ref, n in zip(outs, splits):
        o_ref[...] = p[:, lo:lo + n]
        lo += n


def _inproj_kernel(x_ref, mod_ref, g_ref, w_ref, *outs, splits):
    _project(x_ref[...], mod_ref[...], g_ref, w_ref, outs, splits)


def _inproj_res_kernel(x_ref, moe_ref, modp_ref, mod_ref, g_ref, w_ref, xo_ref, *outs, splits):
    x = x_ref[...] + modp_ref[...][5:6] * moe_ref[...]
    xo_ref[...] = x
    _project(x, mod_ref[...], g_ref, w_ref, outs, splits)


def _inproj(x, moe_prev, mod_prev_idx, modr, mod_idx, gain, w_bf, splits, bsz, seq):
    n, d = x.shape
    nt = seq // TB
    ncols = w_bf.shape[1]
    with_res = moe_prev is not None
    row = lambda b, t: (b * nt + t, 0)
    in_specs = [pl.BlockSpec((TB, d), row)]
    args = [x]
    if with_res:
        in_specs += [pl.BlockSpec((TB, d), row),
                     pl.BlockSpec((None, 6, d), lambda b, t: (mod_prev_idx(b), 0, 0))]
        args += [moe_prev, modr]
    in_specs += [pl.BlockSpec((None, 6, d), lambda b, t: (mod_idx(b), 0, 0))]
    args += [modr]
    in_specs +=[pl.BlockSpec((1, d), lambda b, t: (0, 0)),
                 pl.BlockSpec((d, ncols), lambda b, t: (0, 0))]
    args += [gain.reshape(1, d), w_bf]
    out_shape = [jax.ShapeDtypeStruct((n, c), F32) for c in splits]
    out_specs = [pl.BlockSpec((TB, c), row) for c in splits]
    if with_res:
        out_shape = [jax.ShapeDtypeStruct((n, d), F32)] + out_shape
        out_specs = [pl.BlockSpec((TB, d), row)] + out_specs
        kern = functools.partial(_inproj_res_kernel, splits=splits)
    else:
        kern = functools.partial(_inproj_kernel, splits=splits)
    return pl.pallas_call(
        kern, grid=(bsz, nt), in_specs=in_specs, out_specs=out_specs, out_shape=out_shape,
        compiler_params=_cparams(("arbitrary", "arbitrary"), VMEM_BIG),
        name="inproj",
    )(*args)


def _fill_ext(ext_ref, x, prev_ref, next_ref, first, last, tb):
    ext_ref[0:8, :] = jnp.where(first, 0.0, prev_ref[...])
    ext_ref[8:8 + tb, :] = x
    ext_ref[8 + tb:16 + tb, :] = jnp.where(last, 0.0, next_ref[...])


def _rglru_kernel(xa_ref, prev_ref, next_ref, cw_ref, cb_ref, wg_ref, bg_ref, lam_ref, h0_ref,
                  h_ref, hfin_ref, ext_ref, carry_ref, *, d, nt, tb):
    i = pl.program_id(1)
    t = i if d == 0 else nt - 1 - i
    first = t == 0
    last = t == nt - 1
    x = xa_ref[...]
    da = x.shape[1]
    _fill_ext(ext_ref, x, prev_ref, next_ref, first, last, tb)
    cw = cw_ref[...]
    u = (cw[0:1] * ext_ref[7:7 + tb, :] + cw[1:2] * x + cw[2:3] * ext_ref[9:9 + tb, :]
         + cw[3:4] * ext_ref[10:10 + tb, :] + cb_ref[...])
    gts = _dot_bf(u, wg_ref[...]) + bg_ref[...]
    gate_r = _sigmoid(gts[:, :da])
    gate_i = _sigmoid(gts[:, da:])
    log_a = -C_RGLRU * gate_r * _softplus(-lam_ref[...])
    a = jnp.exp(log_a)
    xin = jnp.sqrt(1.0 - jnp.exp(2.0 * log_a)) * gate_i * u

    @pl.when(i == 0)
    def _():
        carry_ref[...] = h0_ref[...]

    rows = lax.broadcasted_iota(jnp.int32, (tb, da), 0)
    s = 1
    while s < tb:
        if d == 0:
            a_sh = pltpu.roll(a, s, 0)
            x_sh = pltpu.roll(xin, s, 0)
            valid = rows >= s
        else:
            a_sh = pltpu.roll(a, tb - s, 0)
            x_sh = pltpu.roll(xin, tb - s, 0)
            valid = rows < tb - s
        xin = jnp.where(valid, a * x_sh + xin, xin)
        a = jnp.where(valid, a * a_sh, a)
        s *= 2
    h = xin + a * carry_ref[...]
    h_ref[...] = h
    newc = h[tb - 1:tb, :] if d == 0 else h[0:1, :]
    carry_ref[...] = newc
    hfin_ref[...] = newc


def _rglru(pa, conv_w, conv_b, wg_d, bg_d, lam_d, h0_d, d, bsz, seq):
    n = pa.shape[0]
    da = conv_w.shape[1]
    nt = seq // TB
    r8 = TB // 8
    tt = (lambda t: t) if d == 0 else (lambda t: nt - 1 - t)
    blk = lambda b, t: b * nt + tt(t)
    return pl.pallas_call(
        functools.partial(_rglru_kernel, d=d, nt=nt, tb=TB),
        grid=(bsz, nt),
        in_specs=[pl.BlockSpec((TB, da), lambda b, t: (blk(b, t), 0)),
                  pl.BlockSpec((8, da), lambda b, t: (jnp.maximum(blk(b, t) * r8 - 1, 0), 0)),
                  pl.BlockSpec((8, da), lambda b, t: (jnp.minimum((blk(b, t) + 1) * r8, n // 8 - 1), 0)),
                  pl.BlockSpec((4, da), lambda b, t: (0, 0)),
                  pl.BlockSpec((1, da), lambda b, t: (0, 0)),
                  pl.BlockSpec((da, 2 * da), lambda b, t: (0, 0)),
                  pl.BlockSpec((1, 2 * da), lambda b, t: (0, 0)),
                  pl.BlockSpec((1, da), lambda b, t: (0, 0)),
                  pl.BlockSpec((None, 1, da), lambda b, t: (b, 0, 0))],
        out_specs=[pl.BlockSpec((TB, da), lambda b, t: (blk(b, t), 0)),
                   pl.BlockSpec((None, 1, da), lambda b, t: (b, 0, 0))],
        out_shape=[jax.ShapeDtypeStruct((n, da), F32), jax.ShapeDtypeStruct((bsz, 1, da), F32)],
        scratch_shapes=[pltpu.VMEM((TB + 16, da), F32), pltpu.VMEM((1, da), F32)],
        compiler_params=_cparams(("arbitrary", "arbitrary")),
        name=f"rglru_d{d}",
    )(pa, pa, pa, conv_w, conv_b.reshape(1, da), wg_d, bg_d.reshape(1, 2 * da), lam_d.reshape(1, da),
      h0_d.reshape(bsz, 1, da))


def _rwkv_prep_kernel(pb_ref, prev_ref, next_ref, mu_ref, w0_ref, w2_ref, a0_ref, a2_ref, g2_ref,
                      kk_ref, ka_ref, rk_ref, e_ref,
                      r_out, v_out, kk_out, g_out, bonus_out, lw_out, bt_out, kd_out, ext_ref, *, nt, tb, db, lw_, la_):
    t = pl.program_id(1)
    x = pb_ref[...]
    _fill_ext(ext_ref, x, prev_ref, next_ref, t == 0, t == nt - 1, tb)
    sh = 0.5 * (ext_ref[7:7 + tb, :] + ext_ref[9:9 + tb, :])
    z = x + mu_ref[...] * (sh - x)
    r = z[:, 0:db]
    k = z[:, db:2 * db]
    v = z[:, 2 * db:3 * db]
    o = 3 * db
    zw = z[:, o:o + 2 * lw_]
    za = z[:, o + 2 * lw_:o + 2 * lw_ + 2 * la_]
    zg = z[:, o + 2 * lw_ + 2 * la_:]
    w = -_softplus(-(w0_ref[...] + _dot_hi(jnp.tanh(zw), w2_ref[...]))) - 0.5
    lw = -jnp.exp(w)
    a = _sigmoid(a0_ref[...] + _dot_hi(za, a2_ref[...]))
    g = _dot_bf(_sigmoid(zg), g2_ref[...])
    e = e_ref[...]
    kkv = k * kk_ref[...]
    ss = _seg_sum(kkv * kkv, e)
    kkn = kkv / jnp.maximum(jnp.sqrt(ss), 1e-12)
    ka = ka_ref[...]
    kd0 = k * (1.0 + (a[:, :db] - 1.0) * ka)
    kd1 = k * (1.0 + (a[:, db:] - 1.0) * ka)
    bonus = _seg_sum(r * rk_ref[...] * (kd0 + kd1), e) * v
    r_out[...] = r
    v_out[...] = v
    kk_out[...] = kkn
    g_out[...] = g
    bonus_out[...] = bonus
    lw_out[0] = lw[:, :db]
    lw_out[1] = lw[:, db:]
    bt_out[0] = kkn * a[:, :db]
    bt_out[1] = kkn * a[:, db:]
    kd_out[0] = kd0
    kd_out[1] = kd1


def _rwkv_prep(pb, mu, w0c, w2c, a0c, a2c, g2, k_k, k_a, r_k, e_bf, bsz, seq):
    n, ncol = pb.shape
    db = k_k.shape[-1]
    nt = seq // TB
    r8 = TB // 8
    blk = lambda b, t: b * nt + t
    full = lambda shp: pl.BlockSpec(shp, lambda b, t: (0,) * len(shp))
    row = pl.BlockSpec((TB, db), lambda b, t: (blk(b, t), 0))
    row2 = pl.BlockSpec((2, TB, db), lambda b, t: (0, blk(b, t), 0))
    s1 = jax.ShapeDtypeStruct((n, db), F32)
    s2 = jax.ShapeDtypeStruct((2, n, db), F32)
    lw_ = w2c.shape[0] // 2
    la_ = a2c.shape[0] // 2
    return pl.pallas_call(
        functools.partial(_rwkv_prep_kernel, nt=nt, tb=TB, db=db, lw_=lw_, la_=la_),
        grid=(bsz, nt),
        in_specs=[pl.BlockSpec((TB, ncol), lambda b, t: (blk(b, t), 0)),
                  pl.BlockSpec((8, ncol), lambda b, t: (jnp.maximum(blk(b, t) * r8 - 1, 0), 0)),
                  pl.BlockSpec((8, ncol), lambda b, t: (jnp.minimum((blk(b, t) + 1) * r8, n // 8 - 1), 0)),
                  full((1, ncol)), full((1, 2 * db)), full(w2c.shape), full((1, 2 * db)), full(a2c.shape),
                  full(g2.shape), full((1, db)), full((1, db)), full((1, db)), full((db, db))],
        out_specs=[row, row, row, row, row, row2, row2, row2],
        out_shape=[s1, s1, s1, s1, s1, s2, s2, s2],
        scratch_shapes=[pltpu.VMEM((TB + 16, ncol), F32)],
        compiler_params=_cparams(("arbitrary", "arbitrary"), VMEM_BIG),
        name="rwkv_prep",
    )(pb, pb, pb, mu.reshape(1, ncol), w0c, w2c, a0c, a2c, g2, k_k.reshape(1, db), k_a.reshape(1, db),
      r_k.reshape(1, db), e_bf)


def _tri_masks(c, d):
    ti = lax.broadcasted_iota(jnp.int32, (c, c), 0)
    si = lax.broadcasted_iota(jnp.int32, (c, c), 1)
    if d == 0:
        return si < ti, si <= ti, ti == si
    return si > ti, si >= ti, ti == si


def _split(x):
    hi = x.astype(BF16)
    return hi, (x - hi.astype(F32)).astype(BF16)


def _cut(p, sl):
    return p[0][sl], p[1][sl]


def _dot3(a, b, ca=1, cb=0):
    return _dot(a[0], b[0], ca, cb) + _dot(a[1], b[0], ca, cb) + _dot(a[0], b[1], ca, cb)


def _staged(gens):
    results = [None] * len(gens)
    live = list(range(len(gens)))
    while live:
        still = []
        for i in live:
            try:
                out = next(gens[i])
            except StopIteration:
                continue
            if out is not None:
                results[i] = out
            still.append(i)
        live = still
    return results


def _rwkv_chunk(d, c, nh, hd, r, kk, v, lw, bt, kd, s_heads):
    strict, incl, diag = _tri_masks(c, d)
    cs = _dot_hi(incl.astype(F32), lw)
    tot = cs[c - 1:c] if d == 0 else cs[0:1]
    einv = jnp.exp(-cs)
    eend = jnp.exp(tot - cs)
    gtot = jnp.exp(tot)
    a_t = _split(-kk * jnp.exp(cs - lw))
    r_t = _split(r * jnp.exp(cs))
    b_t = _split(bt * einv)
    k_t = _split(kd * einv)
    b_g = _split(bt * eend)
    k_g = _split(kd * eend)
    vs = _split(v)
    eye = diag.astype(F32)
    nsq = int(round(math.log2(SUB))) - 1
    ti = lax.broadcasted_iota(jnp.int32, (c, c), 0)
    si = lax.broadcasted_iota(jnp.int32, (c, c), 1)
    blk_masks = [ti // SUB == si // SUB]
    w = SUB
    while w < c:
        blk_masks.append((ti // (2 * w) == si // (2 * w)) & (ti // w != si // w))
        w *= 2
    def head(h):
        sl = (slice(None), slice(h * hd, (h + 1) * hd))
        s0 = s_heads[h]
        s0s = _split(s0)
        ar = tuple(jnp.concatenate([x[sl], y[sl]], axis=0) for x, y in zip(a_t, r_t))
        bk = tuple(jnp.concatenate([x[sl], y[sl]], axis=0) for x, y in zip(b_t, k_t))
        gm = _dot3(ar, bk, 1, 1)
        yield
        a_ab = jnp.where(strict, gm[:c, :c], 0.0)
        a_ak = jnp.where(strict, gm[:c, c:], 0.0)
        a_rb = jnp.where(incl, gm[c:, :c], 0.0)
        a_rk = jnp.where(incl, gm[c:, c:], 0.0)
        ars = _dot3(ar, s0s, 1, 1)
        yield
        vh = _cut(vs, sl)
        x = ars[:c] + _dot3(_split(a_ak), vh)
        yield
        nd = jnp.where(blk_masks[0], a_ab, 0.0)
        tinv = eye + nd
        p = _split(nd)
        for _ in range(nsq):
            p = _split(_dot3(p, p))
            yield
            tinv = tinv + _dot3(_split(tinv), p)
            yield
        for m in blk_masks[1:]:
            ts = _split(tinv)
            half = _split(_dot3(ts, _split(jnp.where(m, a_ab, 0.0))))
            yield
            tinv = tinv + _dot3(half, ts)
            yield
        u = _dot3(_split(tinv), _split(x))
        yield
        us = _split(u)
        y_out = ars[c:] + _dot(a_rb.astype(BF16), us[0]) + _dot(a_rk.astype(BF16), vh[0])
        yield
        s_out = s0 * gtot[sl] + _dot3(us, _cut(b_g, sl), 0, 0) + _dot3(vh, _cut(k_g, sl), 0, 0)
        yield (y_out, s_out)

    return [head(h) for h in range(nh)]


def _rwkv_scan_kernel(r0_ref, kk0_ref, v0_ref, lw0_ref, bt0_ref, kd0_ref,
                      r1_ref, kk1_ref, v1_ref, lw1_ref, bt1_ref, kd1_ref, s0_ref,
                      y0_ref, y1_ref, sfin_ref, s_ref, *, c, nh, hd):
    @pl.when(pl.program_id(1) == 0)
    def _():
        s_ref[...] = s0_ref[...]

    s_in = [[s_ref[d, h] for h in range(nh)] for d in range(2)]
    ga = _rwkv_chunk(0, c, nh, hd, r0_ref[...], kk0_ref[...], v0_ref[...], lw0_ref[...], bt0_ref[...],
                     kd0_ref[...], s_in[0])
    gb = _rwkv_chunk(1, c, nh, hd, r1_ref[...], kk1_ref[...], v1_ref[...], lw1_ref[...], bt1_ref[...],
                     kd1_ref[...], s_in[1])
    res = _staged([g for pair in zip(ga, gb) for g in pair])
    ra, rb = res[0::2], res[1::2]
    y0_ref[...] = jnp.concatenate([y for y, _ in ra], axis=1)
    y1_ref[...] = jnp.concatenate([y for y, _ in rb], axis=1)
    for h in range(nh):
        s_ref[0, h] = ra[h][1]
        s_ref[1, h] = rb[h][1]
        sfin_ref[0, h] = ra[h][1]
        sfin_ref[1, h] = rb[h][1]


def _rwkv_scan(r, kkn, v, lw, bt, kd, st, e, bsz, seq):
    n, db = r.shape
    nh = db // HEAD_B
    nc = seq // CH
    fwd = lambda b, c: (b * nc + c, 0)
    bwd = lambda b, c: (b * nc + nc - 1 - c, 0)
    row0 = pl.BlockSpec((CH, db), fwd)
    row1 = pl.BlockSpec((CH, db), bwd)
    dir0 = pl.BlockSpec((None, CH, db), lambda b, c: (0,) + fwd(b, c))
    dir1 = pl.BlockSpec((None, CH, db), lambda b, c: (1,) + bwd(b, c))
    st_in = pl.BlockSpec((None, None, 2, nh, HEAD_B, HEAD_B), lambda b, c: (b, e, 0, 0, 0, 0))
    st_out = pl.BlockSpec((None, 2, nh, HEAD_B, HEAD_B), lambda b, c: (b, 0, 0, 0, 0))
    return pl.pallas_call(
        functools.partial(_rwkv_scan_kernel, c=CH, nh=nh, hd=HEAD_B),
        grid=(bsz, nc),
        in_specs=[row0, row0, row0, dir0, dir0, dir0, row1, row1, row1, dir1, dir1, dir1, st_in],
        out_specs=[row0, row1, st_out],
        out_shape=[jax.ShapeDtypeStruct((n, db), F32), jax.ShapeDtypeStruct((n, db), F32),
                   jax.ShapeDtypeStruct((bsz, 2, nh, HEAD_B, HEAD_B), F32)],
        scratch_shapes=[pltpu.VMEM((2, nh, HEAD_B, HEAD_B), F32)],
        compiler_params=_cparams(("arbitrary", "arbitrary")),
        name="rwkv_scan",
    )(r, kkn, v, lw, bt, kd, r, kkn, v, lw, bt, kd, st)


def _log_sigmoid(x):
    return -_softplus(-x)


def _gla_chunk(d, c, nh, dk, dv, p_ref, wup, bgk, s_heads):
    _, incl, _ = _tri_masks(c, d)
    hk = nh * dk
    q = p_ref[:, 0:hk] * dk ** -0.5
    k = p_ref[:, hk:2 * hk]
    gk_in = p_ref[:, 2 * hk + 2 * nh * dv:]
    gk = _dot_hi(gk_in, wup) + bgk
    la = _log_sigmoid(gk) / GATE_NORM
    cs = _dot_hi(incl.astype(F32), la)
    tot = cs[c - 1:c] if d == 0 else cs[0:1]
    q_t = _split(q * jnp.exp(cs))
    k_t = _split(k * jnp.exp(-cs))
    k_e = _split(k * jnp.exp(tot - cs))
    ones = jnp.ones((c, dv), F32)

    def head(h):
        sk = (slice(None), slice(h * dk, (h + 1) * dk))
        vh = _split(p_ref[:, 2 * hk + h * dv:2 * hk + (h + 1) * dv])
        s0 = s_heads[h]
        qh = _cut(q_t, sk)
        att = jnp.where(incl, _dot3(qh, _cut(k_t, sk), 1, 1), 0.0)
        yield
        inter = _dot3(qh, _split(s0))
        yield
        o_out = _dot3(_split(att), vh) + inter
        yield
        dec = jnp.exp(_dot_hi(la[sk], ones, 0, 0))
        yield
        yield (o_out, s0 * dec + _dot3(_cut(k_e, sk), vh, 0, 0))

    return [head(h) for h in range(nh)]


def _gla_kernel(p0_ref, p1_ref, s0_ref, wup_ref, bgk_ref, o0_ref, o1_ref, sfin_ref, s_ref, *, c, nh, dk, dv):
    @pl.when(pl.program_id(1) == 0)
    def _():
        s_ref[...] = s0_ref[...]

    s_in = [[s_ref[d, h] for h in range(nh)] for d in range(2)]
    ga = _gla_chunk(0, c, nh, dk, dv, p0_ref, wup_ref[0], bgk_ref[0], s_in[0])
    gb = _gla_chunk(1, c, nh, dk, dv, p1_ref, wup_ref[1], bgk_ref[1], s_in[1])
    res = _staged([g for pair in zip(ga, gb) for g in pair])
    ra, rb = res[0::2], res[1::2]
    o0_ref[...] = jnp.concatenate([o for o, _ in ra], axis=1)
    o1_ref[...] = jnp.concatenate([o for o, _ in rb], axis=1)
    for h in range(nh):
        s_ref[0, h] = ra[h][1]
        s_ref[1, h] = rb[h][1]
        sfin_ref[0, h] = ra[h][1]
        sfin_ref[1, h] = rb[h][1]


def _gla(p, st, o_, wup, bgk, bsz, seq, grid_mode):
    n, ncol = p.shape
    nh, dk, dv = st.shape[3:]
    hv = nh * dv
    nc = seq // CH
    if grid_mode:
        assert seq // GRID_W == CH and nc == GRID_W
        p_v = p.reshape(bsz, CH, GRID_W * ncol)
        blk = lambda w: (None, CH, w)
        fwd = lambda b, c: (b, 0, c)
        bwd = lambda b, c: (b, 0, nc - 1 - c)
        o_shape = jax.ShapeDtypeStruct((bsz, CH, GRID_W * hv), F32)
    else:
        p_v = p.reshape(bsz * nc, CH, ncol)
        blk = lambda w: (None, CH, w)
        fwd = lambda b, c: (b * nc + c, 0, 0)
        bwd = lambda b, c: (b * nc + nc - 1 - c, 0, 0)
        o_shape = jax.ShapeDtypeStruct((bsz * nc, CH, hv), F32)
    st_in = pl.BlockSpec((None, None, 2, nh, dk, dv), lambda b, c: (b, o_, 0, 0, 0, 0))
    st_out = pl.BlockSpec((None, 2, nh, dk, dv), lambda b, c: (b, 0, 0, 0, 0))
    o0, o1, sfin = pl.pallas_call(
        functools.partial(_gla_kernel, c=CH, nh=nh, dk=dk, dv=dv),
        grid=(bsz, nc),
        in_specs=[pl.BlockSpec(blk(ncol), fwd), pl.BlockSpec(blk(ncol), bwd), st_in,
                  pl.BlockSpec(wup.shape, lambda b, c: (0, 0, 0)),
                  pl.BlockSpec(bgk.shape, lambda b, c: (0, 0, 0))],
        out_specs=[pl.BlockSpec(blk(hv), fwd), pl.BlockSpec(blk(hv), bwd), st_out],
        out_shape=[o_shape, o_shape, jax.ShapeDtypeStruct((bsz, 2, nh, dk, dv), F32)],
        scratch_shapes=[pltpu.VMEM((2, nh, dk, dv), F32)],
        compiler_params=_cparams(("arbitrary", "arbitrary"), VMEM_BIG),
        name="gla",
    )(p_v, p_v, st, wup, bgk)
    return o0.reshape(n, hv), o1.reshape(n, hv), sfin


def _router_tail(x1, mod, g2n_ref, rw_ref, x_out, h2_out, aff_out):
    x_out[...] = x1
    h2 = _rms(x1, g2n_ref[...]) * (1.0 + mod[4:5]) + mod[3:4]
    h2_out[...] = h2.astype(BF16)
    logits = _dot_hi(rw_ref[...], h2, 1, 1)
    m = jnp.max(logits, axis=0, keepdims=True)
    ex = jnp.exp(logits - m)
    aff_out[...] = ex / jnp.sum(ex, axis=0, keepdims=True)


def _out_even_kernel(x_ref, mod_ref, ha0_ref, ha1_ref, ga_ref, y0_ref, y1_ref, g_ref, bonus_ref, lnw_ref, lnb_ref,
                     e_ref, wo_ref, g2n_ref, rw_ref, x_out, h2_out, aff_out, *, da, hd):
    mod = mod_ref[...]
    ya = (ha0_ref[...] + ha1_ref[...]) * _gelu_tanh(ga_ref[...])
    y = y0_ref[...] + y1_ref[...]
    e = e_ref[...]
    mean = _seg_sum(y, e) * (1.0 / hd)
    yc = y - mean
    var = _seg_sum(yc * yc, e) * (1.0 / hd)
    yn = yc * lax.rsqrt(var + GN_EPS_B) * lnw_ref[...] + lnb_ref[...]
    yb = (yn + bonus_ref[...]) * g_ref[...]
    out = _dot_bf(ya, wo_ref[0:da, :]) + _dot_bf(yb, wo_ref[da:, :])
    _router_tail(x_ref[...] + mod[2:3] * out, mod, g2n_ref, rw_ref, x_out, h2_out, aff_out)


def _out_odd_kernel(x_ref, mod_ref, o0_ref, o1_ref, g_ref, gw_ref, wo_ref, g2n_ref, rw_ref,
                    x_out, h2_out, aff_out, *, nh, dv):
    mod = mod_ref[...]
    gw = gw_ref[...]
    parts = []
    for h in range(nh):
        sl = slice(h * dv, (h + 1) * dv)
        o = o0_ref[:, sl] + o1_ref[:, sl]
        parts.append(o * lax.rsqrt(jnp.mean(o * o, axis=-1, keepdims=True) + EPS) * gw)
    y = jnp.concatenate(parts, axis=1) * _silu(g_ref[...])
    out = _dot_bf(y, wo_ref[...])
    _router_tail(x_ref[...] + mod[2:3] * out, mod, g2n_ref, rw_ref, x_out, h2_out, aff_out)


def _out_common(kern, x, modr, mod_idx, extra_args, extra_specs, norm2_g, router_wt, bsz, seq, name):
    n, d = x.shape
    ne = router_wt.shape[0]
    nt = seq // TB
    row = lambda b, t: (b * nt + t, 0)
    in_specs = ([pl.BlockSpec((TB, d), row), pl.BlockSpec((None, 6, d), lambda b, t: (mod_idx(b), 0, 0))]
                + extra_specs
                + [pl.BlockSpec((1, d), lambda b, t: (0, 0)), pl.BlockSpec((ne, d), lambda b, t: (0, 0))])
    return pl.pallas_call(
        kern, grid=(bsz, nt), in_specs=in_specs,
        out_specs=[pl.BlockSpec((TB, d), row), pl.BlockSpec((TB, d), row),
                   pl.BlockSpec((ne, TB), lambda b, t: (0, b * nt + t))],
        out_shape=[jax.ShapeDtypeStruct((n, d), F32), jax.ShapeDtypeStruct((n, d), BF16),
                   jax.ShapeDtypeStruct((ne, n), F32)],
        compiler_params=_cparams(("arbitrary", "arbitrary"), VMEM_BIG),
        name=name,
    )(x, modr, *extra_args, norm2_g.reshape(1, d), router_wt)


def _select_kernel(aff_ref, slot_ref, cnt_ref, *, cap, ntb, tokb):
    aff = aff_ref[...]
    ne = aff.shape[0]
    big = jnp.float32(4.0)

    def body(carry):
        lo, hi, _, _ = carry
        inr = (aff > lo) & (aff <= hi)
        mn = jnp.min(jnp.where(inr, aff, big), axis=1, keepdims=True)
        mx = jnp.max(jnp.where(inr, aff, -big), axis=1, keepdims=True)
        done = mn >= mx
        piv = mn + (mx - mn) * 0.5
        piv = jnp.where(piv >= mx, mn, piv)
        ge = jnp.sum((aff > piv).astype(jnp.int32), axis=1, keepdims=True) >= cap
        lo = jnp.where(done | ~ge, lo, piv)
        hi = jnp.where(done | ge, hi, piv)
        return lo, hi, mx, jnp.sum((~done).astype(jnp.int32))

    init = (jnp.full((ne, 1), -1.0, F32), jnp.full((ne, 1), big, F32), jnp.zeros((ne, 1), F32), jnp.int32(1))
    _, _, thr, _ = lax.while_loop(lambda carry: carry[3] > 0, body, init)
    gt = aff > thr
    eq = aff == thr
    need = cap - jnp.sum(gt.astype(jnp.int32), axis=1, keepdims=True)
    si = lax.broadcasted_iota(jnp.int32, (tokb, tokb), 0)
    ti = lax.broadcasted_iota(jnp.int32, (tokb, tokb), 1)
    ustrict = (si < ti).astype(BF16)

    def excl_count(mask_blk, carry):
        m = mask_blk.astype(BF16)
        pre = _dot(m, ustrict).astype(jnp.int32) + carry
        return pre, carry + jnp.sum(mask_blk.astype(jnp.int32), axis=1, keepdims=True)

    ceq = jnp.zeros((ne, 1), jnp.int32)
    csel = jnp.zeros((ne, 1), jnp.int32)
    lane = lax.broadcasted_iota(jnp.int32, (ne, 128), 1)
    cnt_acc = jnp.zeros((ne, 128), jnp.int32)
    for j in range(ntb):
        sl = slice(j * tokb, (j + 1) * tokb)
        eq_rank, ceq = excl_count(eq[:, sl], ceq)
        sel = gt[:, sl] | (eq[:, sl] & (eq_rank < need))
        cnt_acc = jnp.where(lane == j, csel, cnt_acc)
        rank, csel = excl_count(sel, csel)
        slot_ref[:, sl] = jnp.where(sel, rank, -1)
    cnt_acc = jnp.where(lane == ntb, csel, cnt_acc)
    cnt_ref[...] = cnt_acc


def _select(aff_t, cap):
    ne, n = aff_t.shape
    ntb = n // TOK_B
    assert ntb < 128
    return pl.pallas_call(
        functools.partial(_select_kernel, cap=cap, ntb=ntb, tokb=TOK_B),
        grid=(1,),
        in_specs=[pl.BlockSpec((ne, n), lambda i: (0, 0))],
        out_specs=[pl.BlockSpec((ne, n), lambda i: (0, 0)), pl.BlockSpec((ne, 128), lambda i: (0, 0))],
        out_shape=[jax.ShapeDtypeStruct((ne, n), jnp.int32), jax.ShapeDtypeStruct((ne, 128), jnp.int32)],
        compiler_params=_cparams(("arbitrary",)),
        name="select",
    )(aff_t)


def _gather_kernel(lo_ref, hi_ref, slot_ref, h2_ref, xe_ref, acc_ref, *, nsb, slotb, tokb):
    e = pl.program_id(0)
    sb = pl.program_id(1)
    base = sb * slotb
    sidx = lax.broadcasted_iota(jnp.int32, (slotb, tokb), 0) + base
    acc_ref[...] = jnp.zeros_like(acc_ref)

    def body(tb, carry):
        srow = slot_ref[pl.ds(tb, 1), :]
        onehot = jnp.where(srow == sidx, 1.0, 0.0).astype(BF16)
        off = pl.multiple_of(tb * tokb, tokb)
        acc_ref[...] += _dot(onehot, h2_ref[pl.ds(off, tokb), :])
        return carry

    lax.fori_loop(lo_ref[e * nsb + sb], hi_ref[e * nsb + sb] + 1, body, 0)
    xe_ref[...] = acc_ref[...].astype(BF16)


def _gather(lo, hi, slot3, h2, cap):
    ne, ntb, tokb = slot3.shape
    n, d = h2.shape
    nsb = cap // SLOT_B
    return pl.pallas_call(
        functools.partial(_gather_kernel, nsb=nsb, slotb=SLOT_B, tokb=tokb),
        grid_spec=pltpu.PrefetchScalarGridSpec(
            num_scalar_prefetch=2, grid=(ne, nsb),
            in_specs=[pl.BlockSpec((None, ntb, tokb), lambda e, s, lo, hi: (e, 0, 0)),
                      pl.BlockSpec((n, d), lambda e, s, lo, hi: (0, 0))],
            out_specs=pl.BlockSpec((None, SLOT_B, d), lambda e, s, lo, hi: (e, s, 0)),
            scratch_shapes=[pltpu.VMEM((SLOT_B, d), F32)]),
        out_shape=jax.ShapeDtypeStruct((ne, cap, d), BF16),
        compiler_params=_cparams(("arbitrary", "arbitrary"), VMEM_BIG),
        name="moe_gather",
    )(lo, hi, slot3, h2)


def _ffn_kernel(xe_ref, wg_ref, wu_ref, wd_ref, ye_ref, acc_ref):
    f = pl.program_id(1)
    xe = xe_ref[...]
    hg = _dot(xe, wg_ref[...].astype(BF16))
    hu = _dot(xe, wu_ref[...].astype(BF16))
    hid = (_silu(hg) * hu).astype(BF16)
    part = _dot(hid, wd_ref[...].astype(BF16))

    @pl.when(f == 0)
    def _():
        acc_ref[...] = part

    @pl.when(f > 0)
    def _():
        acc_ref[...] += part

    @pl.when(f == pl.num_programs(1) - 1)
    def _():
        ye_ref[...] = acc_ref[...].astype(BF16)


def _ffn(xe, w_gate, w_up, w_down):
    ne, cap, d = xe.shape
    dff = w_gate.shape[-1]
    nf = dff // FF_B
    return pl.pallas_call(
        _ffn_kernel,
        grid=(ne, nf),
        in_specs=[pl.BlockSpec((None, cap, d), lambda e, f: (e, 0, 0)),
                  pl.BlockSpec((None, d, FF_B), lambda e, f: (e, 0, f)),
                  pl.BlockSpec((None, d, FF_B), lambda e, f: (e, 0, f)),
                  pl.BlockSpec((None, FF_B, d), lambda e, f: (e, f, 0))],
        out_specs=pl.BlockSpec((None, cap, d), lambda e, f: (e, 0, 0)),
        out_shape=jax.ShapeDtypeStruct((ne, cap, d), BF16),
        scratch_shapes=[pltpu.VMEM((cap, d), F32)],
        compiler_params=_cparams(("arbitrary", "arbitrary"), VMEM_BIG),
        name="moe_ffn",
    )(xe, w_gate, w_up, w_down)


def _scatter_kernel(lo_ref, hi_ref, slot_ref, gate_ref, ye_ref, out_ref, *, nsb, slotb, tokb):
    e = pl.program_id(1)
    sb = pl.program_id(2)

    @pl.when((e == 0) & (sb == 0))
    def _():
        out_ref[...] = jnp.zeros_like(out_ref)

    base = sb * slotb
    sidx = lax.broadcasted_iota(jnp.int32, (slotb, tokb), 0) + base
    ye = ye_ref[...]

    def body(tb, carry):
        srow = slot_ref[pl.ds(tb, 1), :]
        grow = gate_ref[pl.ds(tb, 1), :]
        gm = jnp.where(srow == sidx, grow, 0.0).astype(BF16)
        off = pl.multiple_of(tb * tokb, tokb)
        out_ref[pl.ds(off, tokb), :] += _dot(gm, ye, 0, 0)
        return carry

    lax.fori_loop(lo_ref[e * nsb + sb], hi_ref[e * nsb + sb] + 1, body, 0)


def _scatter(lo, hi, slot3, gate3, ye, n):
    ne, cap, d = ye.shape
    _, ntb, tokb = slot3.shape
    nsb = cap // SLOT_B
    dh = d // 2
    return pl.pallas_call(
        functools.partial(_scatter_kernel, nsb=nsb, slotb=SLOT_B, tokb=tokb),
        grid_spec=pltpu.PrefetchScalarGridSpec(
            num_scalar_prefetch=2, grid=(2, ne, nsb),
            in_specs=[pl.BlockSpec((None, ntb, tokb), lambda c, e, s, lo, hi: (e, 0, 0)),
                      pl.BlockSpec((None, ntb, tokb), lambda c, e, s, lo, hi: (e, 0, 0)),
                      pl.BlockSpec((None, SLOT_B, dh), lambda c, e, s, lo, hi: (e, s, c))],
            out_specs=pl.BlockSpec((n, dh), lambda c, e, s, lo, hi: (0, c))),
        out_shape=jax.ShapeDtypeStruct((n, d), F32),
        compiler_params=_cparams(("arbitrary", "arbitrary", "arbitrary"), VMEM_BIG),
        name="moe_scatter",
    )(lo, hi, slot3, gate3, ye)


def _moe(h2, aff_t, w_gate, w_up, w_down):
    n, d = h2.shape
    ne = aff_t.shape[0]
    cap = max(1, CAP_FACTOR * n // ne)
    assert cap % SLOT_B == 0 and n % TOK_B == 0
    ntb = n // TOK_B
    nsb = cap // SLOT_B
    slot, cnt = _select(aff_t, cap)
    cnt = cnt[:, :ntb + 1]
    starts = (jnp.arange(nsb, dtype=jnp.int32) * SLOT_B)[None, :, None]
    lo = jnp.clip(jnp.sum(cnt[:, None, 1:] <= starts, -1), 0, ntb - 1).astype(jnp.int32).reshape(-1)
    hi = jnp.clip(jnp.sum(cnt[:, None, 1:] < starts + SLOT_B, -1), 0, ntb - 1).astype(jnp.int32).reshape(-1)
    slot3 = slot.reshape(ne, ntb, TOK_B)
    gate3 = aff_t.reshape(ne, ntb, TOK_B)
    xe = _gather(lo, hi, slot3, h2, cap)
    ye = _ffn(xe, w_gate, w_up, w_down)
    return _scatter(lo, hi, slot3, gate3, ye, n)


def _final_kernel(x_ref, moe_ref, mod_ref, g_ref, o_ref):
    x = x_ref[...] + mod_ref[...][5:6] * moe_ref[...]
    o_ref[...] = _rms(x, g_ref[...])


def _final(x, moe, modr, mod_idx, gain, bsz, seq):
    n, d = x.shape
    nt = seq // TB
    row = lambda b, t: (b * nt + t, 0)
    return pl.pallas_call(
        _final_kernel, grid=(bsz, nt),
        in_specs=[pl.BlockSpec((TB, d), row), pl.BlockSpec((TB, d), row),
                  pl.BlockSpec((None, 6, d), lambda b, t: (mod_idx(b), 0, 0)),
                  pl.BlockSpec((1, d), lambda b, t: (0, 0))],
        out_specs=pl.BlockSpec((TB, d), row),
        out_shape=jax.ShapeDtypeStruct((n, d), F32),
        compiler_params=_cparams(("arbitrary", "arbitrary")),
        name="final_norm",
    )(x, moe, modr, gain.reshape(1, d))


def _block_diag(w):
    h, n, _ = w.shape
    eye = jnp.eye(h, dtype=w.dtype)
    return (eye[:, None, :, None] * w[:, :, None, :]).reshape(h * n, h * n)


def _lora_cat(w2):
    _, l, c = w2.shape
    z = jnp.zeros((l, c), w2.dtype)
    return jnp.concatenate([jnp.concatenate([w2[0], z], 1), jnp.concatenate([z, w2[1]], 1)], 0)


def _trunk(x, bsz, seq, cond_base, per_batch, st_rglru, st_rwkv, st_gla, grid_mode, modr, P):
    d_model = x.shape[1]
    depth = P['norm1_g'].shape[0]
    new_rglru, new_rwkv, new_gla = [], [], []
    moe_prev = None
    for l in range(depth):
        mod_idx = (lambda b, l=l: l * 8 + cond_base + b) if per_batch else (lambda b, l=l: l * 8 + cond_base)
        mod_prev = (lambda b, l=l: (l - 1) * 8 + cond_base + b) if per_batch else (lambda b, l=l: (l - 1) * 8 + cond_base)
        router_wt = P['router_w'][l].T
        if l % 2 == 0:
            e = l // 2
            da = P['a_conv_w'].shape[-1]
            db = P['b_k_k'].shape[-1]
            w_in = P['ev_w_in'][e].astype(BF16)
            nb_cols = w_in.shape[1] - 2 * da
            res = _inproj(x, moe_prev, mod_prev, modr, mod_idx, P['norm1_g'][l], w_in, (2 * da, nb_cols), bsz, seq)
            if moe_prev is not None:
                x, pa, pb = res
            else:
                pa, pb = res
            ha, sa = [], []
            for d in range(2):
                wg = jnp.concatenate([_block_diag(P['a_w_rg'][e, d]), _block_diag(P['a_w_ig'][e, d])], 1).astype(BF16)
                bg = jnp.concatenate([P['a_b_rg'][e, d], P['a_b_ig'][e, d]])
                h_d, hf = _rglru(pa, P['a_conv_w'][e], P['a_conv_b'][e], wg, bg, P['a_lam'][e, d],
                                 st_rglru[:, e, d], d, bsz, seq)
                ha.append(h_d)
                sa.append(hf.reshape(bsz, da))
            new_rglru.append(jnp.stack(sa, 1))
            e_bf = _block_diag(jnp.ones((db // HEAD_B, HEAD_B, HEAD_B), BF16))
            r, v, kkn, g, bonus, lw, bt, kd = _rwkv_prep(
                pb, P['b_mu'][e], P['b_w0'][e].reshape(1, 2 * db), _lora_cat(P['b_w2'][e]),
                P['b_a0'][e].reshape(1, 2 * db), _lora_cat(P['b_a2'][e]), P['b_g2'][e].astype(BF16),
                P['b_k_k'][e], P['b_k_a'][e], P['b_r_k'][e], e_bf, bsz, seq)
            y0, y1, sfin = _rwkv_scan(r, kkn, v, lw, bt, kd, st_rwkv, e, bsz, seq)
            ys = [y0, y1]
            new_rwkv.append(sfin)
            nt = seq // TB
            rowh = pl.BlockSpec((TB, da), lambda b, t: (b * nt + t, 0))
            rowg = pl.BlockSpec((TB, da), lambda b, t: (b * nt + t, 1))
            vec = pl.BlockSpec((1, db), lambda b, t: (0, 0))
            x, h2, aff_t = _out_common(
                functools.partial(_out_even_kernel, da=da, hd=HEAD_B), x, modr, mod_idx,
                [ha[0], ha[1], pa, ys[0], ys[1], g, bonus, P['b_ln_w'][e].reshape(1, db), P['b_ln_b'][e].reshape(1, db),
                 e_bf, P['ev_w_out'][e].astype(BF16)],
                [rowh, rowh, rowg, rowh, rowh, rowh, rowh, vec, vec,
                 pl.BlockSpec((db, db), lambda b, t: (0, 0)),
                 pl.BlockSpec((da + db, d_model), lambda b, t: (0, 0))],
                P['norm2_g'][l], router_wt, bsz, seq, "out_even")
        else:
            o_ = l // 2
            w_in = P['od_w_in'][o_]
            ncol = w_in.shape[1]
            npad = -ncol % 128
            nh, dk, dv = st_gla.shape[3:]
            w_in = jnp.pad(w_in, ((0, 0), (0, npad))).astype(BF16)
            res = _inproj(x, moe_prev, mod_prev, modr, mod_idx, P['norm1_g'][l], w_in, (ncol + npad,), bsz, seq)
            if moe_prev is not None:
                x, p = res
            else:
                (p,) = res
            gin = ncol + npad - 2 * nh * dk - 2 * nh * dv
            wup = jnp.stack([jnp.zeros((gin, nh * dk), F32).at[d * GATE_RANK:(d + 1) * GATE_RANK]
                             .set(P['c_w_gk_up'][o_, d]) for d in range(2)])
            o0, o1, sfin = _gla(p, st_gla, o_, wup, P['c_b_gk'][o_].reshape(2, 1, nh * dk), bsz, seq, grid_mode)
            os_ = [o0, o1]
            new_gla.append(sfin)
            nt = seq // TB
            hv = nh * dv
            rowo = pl.BlockSpec((TB, hv), lambda b, t: (b * nt + t, 0))
            gblk = (2 * nh * dk + hv) // hv
            assert gblk * hv == 2 * nh * dk + hv
            rowg = pl.BlockSpec((TB, hv), lambda b, t: (b * nt + t, gblk))
            x, h2, aff_t = _out_common(
                functools.partial(_out_odd_kernel, nh=nh, dv=dv), x, modr, mod_idx,
                [os_[0], os_[1], p, P['c_gnorm_w'][o_].reshape(1, dv), P['od_w_out'][o_].astype(BF16)],
                [rowo, rowo, rowg, pl.BlockSpec((1, dv), lambda b, t: (0, 0)),
                 pl.BlockSpec((hv, d_model), lambda b, t: (0, 0))],
                P['norm2_g'][l], router_wt, bsz, seq, "out_odd")
        moe_prev = _moe(h2, aff_t, P['moe_w_gate'][l], P['moe_w_up'][l], P['moe_w_down'][l])
    mod_last = (lambda b: (depth - 1) * 8 + cond_base + b) if per_batch else (lambda b: (depth - 1) * 8 + cond_base)
    y = _final(x, moe_prev, modr, mod_last, P['final_norm_g'], bsz, seq)
    return y, jnp.stack(new_rglru, 1), jnp.stack(new_rwkv, 1), jnp.stack(new_gla, 1)


def kernel(x_prompt, x_sample, state_rglru, state_rwkv, state_gla, c, c_ctx, norm1_g, norm2_g, ada_w, ada_b, router_w, moe_w_gate, moe_w_up, moe_w_down, ev_w_in, ev_w_out, a_conv_w, a_conv_b, a_w_rg, a_b_rg, a_w_ig, a_b_ig, a_lam, b_mu, b_w0, b_w2, b_a0, b_a2, b_g2, b_k_k, b_k_a, b_r_k, b_ln_w, b_ln_b, od_w_in, od_w_out, c_w_gk_up, c_b_gk, c_gnorm_w, final_norm_g):
    P = dict(norm1_g=norm1_g, norm2_g=norm2_g, router_w=router_w,
             moe_w_gate=moe_w_gate, moe_w_up=moe_w_up, moe_w_down=moe_w_down,
             ev_w_in=ev_w_in, ev_w_out=ev_w_out, a_conv_w=a_conv_w, a_conv_b=a_conv_b,
             a_w_rg=a_w_rg, a_b_rg=a_b_rg, a_w_ig=a_w_ig, a_b_ig=a_b_ig, a_lam=a_lam,
             b_mu=b_mu, b_w0=b_w0, b_w2=b_w2, b_a0=b_a0, b_a2=b_a2, b_g2=b_g2, b_k_k=b_k_k,
             b_k_a=b_k_a, b_r_k=b_r_k, b_ln_w=b_ln_w, b_ln_b=b_ln_b,
             od_w_in=od_w_in, od_w_out=od_w_out, c_w_gk_up=c_w_gk_up, c_b_gk=c_b_gk,
             c_gnorm_w=c_gnorm_w, final_norm_g=final_norm_g)
    bp, tp, d_model = x_prompt.shape
    bs, ts, _ = x_sample.shape
    assert bs + 1 <= 8 and tp % TB == 0 and ts % TB == 0
    depth = ada_w.shape[0]
    cv8 = jnp.zeros((8, d_model), F32).at[0].set(c_ctx).at[1:1 + bs].set(c)
    modr = _modulation(cv8, ada_w, ada_b).reshape(depth * 8, 6, d_model)
    dt = x_prompt.dtype
    z_rglru = jnp.zeros((bp,) + state_rglru.shape[1:], dt)
    z_rwkv = jnp.zeros((bp,) + state_rwkv.shape[1:], dt)
    z_gla = jnp.zeros((bp,) + state_gla.shape[1:], dt)
    y_prompt, n_rglru, n_rwkv, n_gla = _trunk(x_prompt.reshape(bp * tp, d_model), bp, tp, 0, False,
                                              z_rglru, z_rwkv, z_gla, False, modr, P)
    y_sample, _, _, _ = _trunk(x_sample.reshape(bs * ts, d_model), bs, ts, 1, True,
                               state_rglru, state_rwkv, state_gla, True, modr, P)
    return (y_prompt.reshape(bp, tp, d_model), y_sample.reshape(bs, ts, d_model), n_rglru, n_rwkv, n_gla)
```

```python
import functools
import math

import numpy as np
import jax
import jax.numpy as jnp
from jax import lax
from jax.experimental import pallas as pl
from jax.experimental.pallas import tpu as pltpu

F32 = jnp.float32
BF16 = jnp.bfloat16
HI = lax.Precision.HIGHEST

GRID_W = 64
EPS = 1e-6
C_RGLRU = 8.0
GN_EPS_B = 64e-5
GATE_NORM = 16.0
CAP_FACTOR = 2
HEAD_B = 64
H_A = 8
H_C = 4
GATE_RANK = 16

SUBLANES = 8
LANES = 128
TB = 256
CH = 64
SUB = 16
SLOT_B = 256
TOK_B = 256
FF_B = 512
VMEM_BIG = 56 * 1024 * 1024


def _dot(a, b, ca=1, cb=0, prec=None):
    return lax.dot_general(a, b, (((ca,), (cb,)), ((), ())), precision=prec,
                           preferred_element_type=F32)


def _dot_bf(a, b, ca=1, cb=0):
    return _dot(a.astype(BF16), b.astype(BF16), ca, cb)


def _dot_hi(a, b, ca=1, cb=0):
    return _dot(a.astype(F32), b.astype(F32), ca, cb, HI)


def _seg_sum(x, e_bf):
    hi = x.astype(BF16)
    lo = (x - hi.astype(F32)).astype(BF16)
    return _dot(hi, e_bf) + _dot(lo, e_bf)


def _mask_dot(mask_bf, x, cm=1, cx=0):
    hi = x.astype(BF16)
    r1 = x - hi.astype(F32)
    mid = r1.astype(BF16)
    lo = (r1 - mid.astype(F32)).astype(BF16)
    return _dot(mask_bf, hi, cm, cx) + _dot(mask_bf, mid, cm, cx) + _dot(mask_bf, lo, cm, cx)


def _sigmoid(x):
    return 1.0 / (1.0 + jnp.exp(-x))


def _softplus(x):
    return jnp.maximum(x, 0.0) + jnp.log1p(jnp.exp(-jnp.abs(x)))


def _silu(x):
    return x * _sigmoid(x)


def _gelu_tanh(x):
    return 0.5 * x * (1.0 + jnp.tanh(math.sqrt(2.0 / math.pi) * (x + 0.044715 * (x * x * x))))


def _rms(x, g):
    return x * lax.rsqrt(jnp.mean(x * x, axis=-1, keepdims=True) + EPS) * g


def _cparams(sem, vmem=None):
    return pltpu.CompilerParams(dimension_semantics=sem, vmem_limit_bytes=vmem)


def _mod_kernel(c_ref, w_ref, b_ref, o_ref):
    o_ref[...] = _dot_hi(_silu(c_ref[...]), w_ref[...]) + b_ref[...]


def _modulation(cv8, ada_w, ada_b):
    depth, d, n6 = ada_w.shape
    nb = 1536
    return pl.pallas_call(
        _mod_kernel,
        grid=(depth, n6 // nb),
        in_specs=[pl.BlockSpec((8, d), lambda l, j: (0, 0)),
                  pl.BlockSpec((None, d, nb), lambda l, j: (l, 0, j)),
                  pl.BlockSpec((None, 1, nb), lambda l, j: (l, 0, j))],
        out_specs=pl.BlockSpec((None, 8, nb), lambda l, j: (l, 0, j)),
        out_shape=jax.ShapeDtypeStruct((depth, 8, n6), F32),
        compiler_params=_cparams(("arbitrary", "arbitrary")),
        name="modulation",
    )(cv8, ada_w, ada_b.reshape(depth, 1, n6))


def _project(x, mod, g_ref, w_ref, outs, splits, grid_rows):
    h = _rms(x, g_ref[...]) * (1.0 + mod[1:2]) + mod[0:1]
    p = _dot_bf(h, w_ref[...])
    lo = 0
    if grid_rows:
        n0 = splits[0]
        scr = outs[-1]
        for j in range(n0 // LANES):
            scr[j] = p[:, j * LANES:(j + 1) * LANES]
        for c in range(GRID_W):
            for j in range(n0 // LANES):
                outs[0][:, c * n0 + j * LANES:c * n0 + (j + 1) * LANES] = scr[j, pl.ds(c, grid_rows, stride=GRID_W), :]
        lo, outs, splits = n0, outs[1:-1], splits[1:]
    for o_ref, n in zip(outs, splits):
        o_ref[...] = p[:, lo:lo + n]
        lo += n


def _inproj_kernel(x_ref, mod_ref, g_ref, w_ref, *outs, splits, grid_rows):
    _project(x_ref[...], mod_ref[...], g_ref, w_ref, outs, splits, grid_rows)


def _inproj_res_kernel(x_ref, moe_ref, modp_ref, mod_ref, g_ref, w_ref, xo_ref, *outs, splits, grid_rows):
    x = x_ref[...] + modp_ref[...][5:6] * moe_ref[...]
    xo_ref[...] = x
    _project(x, mod_ref[...], g_ref, w_ref, outs, splits, grid_rows)


def _inproj(x, moe_prev, mod_prev_idx, modr, mod_idx, gain, w_bf, splits, bsz, seq, grid_view=False):
    n, d = x.shape
    tb = SUBLANES * GRID_W if grid_view else TB
    nt = seq // tb
    ncols = w_bf.shape[1]
    with_res = moe_prev is not None
    row = lambda b, t: (b * nt + t, 0)
    in_specs = [pl.BlockSpec((tb, d), row)]
    args = [x]
    if with_res:
        in_specs += [pl.BlockSpec((tb, d), row),
                     pl.BlockSpec((None, 6, d), lambda b, t: (mod_prev_idx(b), 0, 0))]
        args += [moe_prev, modr]
    in_specs += [pl.BlockSpec((None, 6, d), lambda b, t: (mod_idx(b), 0, 0))]
    args += [modr]
    in_specs += [pl.BlockSpec((1, d), lambda b, t: (0, 0)),
                 pl.BlockSpec((d, ncols), lambda b, t: (0, 0), pipeline_mode=pl.Buffered(1))]
    args += [gain.reshape(1, d), w_bf]
    out_shape = [jax.ShapeDtypeStruct((n, c), F32) for c in splits]
    out_specs = [pl.BlockSpec((tb, c), row) for c in splits]
    scratch = []
    grid_rows = 0
    if grid_view:
        grid_rows = tb // GRID_W
        out_shape[0] = jax.ShapeDtypeStruct((bsz, seq // GRID_W, GRID_W * splits[0]), F32)
        out_specs[0] = pl.BlockSpec((None, grid_rows, GRID_W * splits[0]), lambda b, t: (b, t, 0))
        scratch = [pltpu.VMEM((splits[0] // LANES, tb, LANES), F32)]
    if with_res:
        out_shape = [jax.ShapeDtypeStruct((n, d), F32)] + out_shape
        out_specs = [pl.BlockSpec((tb, d), row)] + out_specs
        kern = functools.partial(_inproj_res_kernel, splits=splits, grid_rows=grid_rows)
    else:
        kern = functools.partial(_inproj_kernel, splits=splits, grid_rows=grid_rows)
    return pl.pallas_call(
        kern, grid=(bsz, nt), in_specs=in_specs, out_specs=out_specs, out_shape=out_shape,
        scratch_shapes=scratch,
        compiler_params=_cparams(("arbitrary", "arbitrary"), VMEM_BIG),
        name="inproj",
    )(*args)


def _fill_ext(ext_ref, x, prev_ref, next_ref, first, last, tb):
    ext_ref[0:8, :] = jnp.where(first, 0.0, prev_ref[...])
    ext_ref[8:8 + tb, :] = x
    ext_ref[8 + tb:16 + tb, :] = jnp.where(last, 0.0, next_ref[...])


def _rglru_kernel(xa_ref, prev_ref, next_ref, cw_ref, cb_ref, wg_ref, bg_ref, lam_ref, h0_ref,
                  h_ref, hfin_ref, ext_ref, carry_ref, *, d, nt, tb):
    i = pl.program_id(1)
    t = i if d == 0 else nt - 1 - i
    first = t == 0
    last = t == nt - 1
    x = xa_ref[...]
    da = x.shape[1]
    _fill_ext(ext_ref, x, prev_ref, next_ref, first, last, tb)
    cw = cw_ref[...]
    u = (cw[0:1] * ext_ref[7:7 + tb, :] + cw[1:2] * x + cw[2:3] * ext_ref[9:9 + tb, :]
         + cw[3:4] * ext_ref[10:10 + tb, :] + cb_ref[...])
    gts = _dot_bf(u, wg_ref[...]) + bg_ref[...]
    gate_r = _sigmoid(gts[:, :da])
    gate_i = _sigmoid(gts[:, da:])
    log_a = -C_RGLRU * gate_r * _softplus(-lam_ref[...])
    a = jnp.exp(log_a)
    xin = jnp.sqrt(1.0 - jnp.exp(2.0 * log_a)) * gate_i * u

    @pl.when(i == 0)
    def _():
        carry_ref[...] = h0_ref[...]

    rows = lax.broadcasted_iota(jnp.int32, (tb, da), 0)
    s = 1
    while s < tb:
        if d == 0:
            a_sh = pltpu.roll(a, s, 0)
            x_sh = pltpu.roll(xin, s, 0)
            valid = rows >= s
        else:
            a_sh = pltpu.roll(a, tb - s, 0)
            x_sh = pltpu.roll(xin, tb - s, 0)
            valid = rows < tb - s
        xin = jnp.where(valid, a * x_sh + xin, xin)
        a = jnp.where(valid, a * a_sh, a)
        s *= 2
    h = xin + a * carry_ref[...]
    h_ref[...] = h
    newc = h[tb - 1:tb, :] if d == 0 else h[0:1, :]
    carry_ref[...] = newc
    hfin_ref[...] = newc


def _rglru(pa, conv_w, conv_b, wg_d, bg_d, lam_d, h0_d, d, bsz, seq):
    n = pa.shape[0]
    da = conv_w.shape[1]
    nt = seq // TB
    r8 = TB // 8
    tt = (lambda t: t) if d == 0 else (lambda t: nt - 1 - t)
    blk = lambda b, t: b * nt + tt(t)
    return pl.pallas_call(
        functools.partial(_rglru_kernel, d=d, nt=nt, tb=TB),
        grid=(bsz, nt),
        in_specs=[pl.BlockSpec((TB, da), lambda b, t: (blk(b, t), 0)),
                  pl.BlockSpec((8, da), lambda b, t: (jnp.maximum(blk(b, t) * r8 - 1, 0), 0)),
                  pl.BlockSpec((8, da), lambda b, t: (jnp.minimum((blk(b, t) + 1) * r8, n // 8 - 1), 0)),
                  pl.BlockSpec((4, da), lambda b, t: (0, 0)),
                  pl.BlockSpec((1, da), lambda b, t: (0, 0)),
                  pl.BlockSpec((da, 2 * da), lambda b, t: (0, 0)),
                  pl.BlockSpec((1, 2 * da), lambda b, t: (0, 0)),
                  pl.BlockSpec((1, da), lambda b, t: (0, 0)),
                  pl.BlockSpec((None, 1, da), lambda b, t: (b, 0, 0))],
        out_specs=[pl.BlockSpec((TB, da), lambda b, t: (blk(b, t), 0)),
                   pl.BlockSpec((None, 1, da), lambda b, t: (b, 0, 0))],
        out_shape=[jax.ShapeDtypeStruct((n, da), F32), jax.ShapeDtypeStruct((bsz, 1, da), F32)],
        scratch_shapes=[pltpu.VMEM((TB + 16, da), F32), pltpu.VMEM((1, da), F32)],
        compiler_params=_cparams(("arbitrary", "arbitrary")),
        name=f"rglru_d{d}",
    )(pa, pa, pa, conv_w, conv_b.reshape(1, da), wg_d, bg_d.reshape(1, 2 * da), lam_d.reshape(1, da),
      h0_d.reshape(bsz, 1, da))


def _rwkv_prep_kernel(pb_ref, prev_ref, next_ref, mu_ref, w0_ref, w2_ref, a0_ref, a2_ref, g2_ref,
                      kk_ref, ka_ref, rk_ref, e_ref,
                      r_out, v_out, kk_out, g_out, bonus_out, lw_out, bt_out, kd_out, ext_ref, *, nt, tb, db, lw_, la_):
    t = pl.program_id(1)
    x = pb_ref[...]
    _fill_ext(ext_ref, x, prev_ref, next_ref, t == 0, t == nt - 1, tb)
    sh = 0.5 * (ext_ref[7:7 + tb, :] + ext_ref[9:9 + tb, :])
    z = x + mu_ref[...] * (sh - x)
    r = z[:, 0:db]
    k = z[:, db:2 * db]
    v = z[:, 2 * db:3 * db]
    o = 3 * db
    zw = z[:, o:o + 2 * lw_]
    za = z[:, o + 2 * lw_:o + 2 * lw_ + 2 * la_]
    zg = z[:, o + 2 * lw_ + 2 * la_:]
    w = -_softplus(-(w0_ref[...] + _dot_hi(jnp.tanh(zw), w2_ref[...]))) - 0.5
    lw = -jnp.exp(w)
    a = _sigmoid(a0_ref[...] + _dot_hi(za, a2_ref[...]))
    g = _dot_bf(_sigmoid(zg), g2_ref[...])
    e = e_ref[...]
    kkv = k * kk_ref[...]
    ss = _seg_sum(kkv * kkv, e)
    kkn = kkv / jnp.maximum(jnp.sqrt(ss), 1e-12)
    ka = ka_ref[...]
    kd0 = k * (1.0 + (a[:, :db] - 1.0) * ka)
    kd1 = k * (1.0 + (a[:, db:] - 1.0) * ka)
    bonus = _seg_sum(r * rk_ref[...] * (kd0 + kd1), e) * v
    r_out[...] = r
    v_out[...] = v
    kk_out[...] = kkn
    g_out[...] = g
    bonus_out[...] = bonus
    lw_out[0] = lw[:, :db]
    lw_out[1] = lw[:, db:]
    bt_out[0] = kkn * a[:, :db]
    bt_out[1] = kkn * a[:, db:]
    kd_out[0] = kd0
    kd_out[1] = kd1


def _rwkv_prep(pb, mu, w0c, w2c, a0c, a2c, g2, k_k, k_a, r_k, e_bf, bsz, seq):
    n, ncol = pb.shape
    db = k_k.shape[-1]
    nt = seq // TB
    r8 = TB // 8
    blk = lambda b, t: b * nt + t
    full = lambda shp: pl.BlockSpec(shp, lambda b, t: (0,) * len(shp))
    row = pl.BlockSpec((TB, db), lambda b, t: (blk(b, t), 0))
    row2 = pl.BlockSpec((2, TB, db), lambda b, t: (0, blk(b, t), 0))
    s1 = jax.ShapeDtypeStruct((n, db), F32)
    s2 = jax.ShapeDtypeStruct((2, n, db), F32)
    lw_ = w2c.shape[0] // 2
    la_ = a2c.shape[0] // 2
    return pl.pallas_call(
        functools.partial(_rwkv_prep_kernel, nt=nt, tb=TB, db=db, lw_=lw_, la_=la_),
        grid=(bsz, nt),
        in_specs=[pl.BlockSpec((TB, ncol), lambda b, t: (blk(b, t), 0)),
                  pl.BlockSpec((8, ncol), lambda b, t: (jnp.maximum(blk(b, t) * r8 - 1, 0), 0)),
                  pl.BlockSpec((8, ncol), lambda b, t: (jnp.minimum((blk(b, t) + 1) * r8, n // 8 - 1), 0)),
                  full((1, ncol)), full((1, 2 * db)), full(w2c.shape), full((1, 2 * db)), full(a2c.shape),
                  full(g2.shape), full((1, db)), full((1, db)), full((1, db)), full((db, db))],
        out_specs=[row, row, row, row, row, row2, row2, row2],
        out_shape=[s1, s1, s1, s1, s1, s2, s2, s2],
        scratch_shapes=[pltpu.VMEM((TB + 16, ncol), F32)],
        compiler_params=_cparams(("arbitrary", "arbitrary"), VMEM_BIG),
        name="rwkv_prep",
    )(pb, pb, pb, mu.reshape(1, ncol), w0c, w2c, a0c, a2c, g2, k_k.reshape(1, db), k_a.reshape(1, db),
      r_k.reshape(1, db), e_bf)


def _tri_masks(c, d):
    ti = lax.broadcasted_iota(jnp.int32, (c, c), 0)
    si = lax.broadcasted_iota(jnp.int32, (c, c), 1)
    if d == 0:
        return si < ti, si <= ti, ti == si
    return si > ti, si >= ti, ti == si


def _split(x):
    hi = x.astype(BF16)
    return hi, (x - hi.astype(F32)).astype(BF16)


def _cut(p, sl):
    return p[0][sl], p[1][sl]


def _dot3(a, b, ca=1, cb=0):
    return _dot(a[0], b[0], ca, cb) + _dot(a[1], b[0], ca, cb) + _dot(a[0], b[1], ca, cb)


def _staged(gens):
    results = [None] * len(gens)
    live = list(range(len(gens)))
    while live:
        still = []
        for i in live:
            try:
                out = next(gens[i])
            except StopIteration:
                continue
            if out is not None:
                results[i] = out
            still.append(i)
        live = still
    return results


def _rwkv_chunk(d, c, nh, hd, r, kk, v, lw, bt, kd, s_heads):
    strict, incl, diag = _tri_masks(c, d)
    cs = _mask_dot(incl.astype(BF16), lw)
    tot = cs[c - 1:c] if d == 0 else cs[0:1]
    einv = jnp.exp(-cs)
    eend = jnp.exp(tot - cs)
    gtot = jnp.exp(tot)
    a_t = _split(-kk * jnp.exp(cs - lw))
    r_t = _split(r * jnp.exp(cs))
    b_t = _split(bt * einv)
    k_t = _split(kd * einv)
    b_g = _split(bt * eend)
    k_g = _split(kd * eend)
    vs = _split(v)
    eye = diag.astype(F32)
    t2 = lax.broadcasted_iota(jnp.int32, (c, 2 * c), 0)
    s2 = lax.broadcasted_iota(jnp.int32, (c, 2 * c), 1)
    s2 = jnp.where(s2 >= c, s2 - c, s2)
    incl2 = (s2 <= t2) if d == 0 else (s2 >= t2)
    nsq = int(round(math.log2(SUB))) - 1
    ti = lax.broadcasted_iota(jnp.int32, (c, c), 0)
    si = lax.broadcasted_iota(jnp.int32, (c, c), 1)
    blk_masks = [ti // SUB == si // SUB]
    w = SUB
    while w < c:
        blk_masks.append((ti // (2 * w) == si // (2 * w)) & (ti // w != si // w))
        w *= 2
    def head(h):
        sl = (slice(None), slice(h * hd, (h + 1) * hd))
        s0 = s_heads[h]
        s0s = _split(s0)
        ar = tuple(jnp.concatenate([x[sl], y[sl]], axis=0) for x, y in zip(a_t, r_t))
        bk = tuple(jnp.concatenate([x[sl], y[sl]], axis=0) for x, y in zip(b_t, k_t))
        gm = _dot3(ar, bk, 1, 1)
        yield
        a_ab = jnp.where(strict, gm[:c, :c], 0.0)
        a_ak = jnp.where(strict, gm[:c, c:], 0.0)
        ars =_dot3(ar, s0s, 1, 1)
        yield
        vh = _cut(vs, sl)
        x = ars[:c] + _dot3(_split(a_ak), vh)
        yield
        nd = jnp.where(blk_masks[0], a_ab, 0.0)
        tinv = eye + nd
        p = _split(nd)
        for _ in range(nsq):
            p = _split(_dot3(p, p))
            yield
            tinv = tinv + _dot3(_split(tinv), p)
            yield
        for m in blk_masks[1:]:
            ts = _split(tinv)
            half = _split(_dot3(ts, _split(jnp.where(m, a_ab, 0.0))))
            yield
            tinv = tinv + _dot3(half, ts)
            yield
        u = _dot3(_split(tinv), _split(x))
        yield
        us = _split(u)
        uv = tuple(jnp.concatenate([x, y], axis=0) for x, y in zip(us, vh))
        y_out = ars[c:] + _dot(jnp.where(incl2, gm[c:], 0.0).astype(BF16), uv[0])
        yield
        bkg = tuple(jnp.concatenate([x[sl], y[sl]], axis=0) for x, y in zip(b_g, k_g))
        s_out = s0 * gtot[sl] + _dot3(uv, bkg, 0, 0)
        yield (y_out, s_out)

    return [head(h) for h in range(nh)]


def _rwkv_scan_kernel(r0_ref, kk0_ref, v0_ref, lw0_ref, bt0_ref, kd0_ref,
                      r1_ref, kk1_ref, v1_ref, lw1_ref, bt1_ref, kd1_ref, s0_ref,
                      y0_ref, y1_ref, sfin_ref, s_ref, *, c, nh, hd):
    @pl.when(pl.program_id(1) == 0)
    def _():
        s_ref[...] = s0_ref[...]

    s_in = [[s_ref[d, h] for h in range(nh)] for d in range(2)]
    ga = _rwkv_chunk(0, c, nh, hd, r0_ref[...], kk0_ref[...], v0_ref[...], lw0_ref[...], bt0_ref[...],
                     kd0_ref[...], s_in[0])
    gb = _rwkv_chunk(1, c, nh, hd, r1_ref[...], kk1_ref[...], v1_ref[...], lw1_ref[...], bt1_ref[...],
                     kd1_ref[...], s_in[1])
    res = _staged([g for pair in zip(ga, gb) for g in pair])
    ra, rb = res[0::2], res[1::2]
    y0_ref[...] = jnp.concatenate([y for y, _ in ra], axis=1)
    y1_ref[...] = jnp.concatenate([y for y, _ in rb], axis=1)
    for h in range(nh):
        s_ref[0, h] = ra[h][1]
        s_ref[1, h] = rb[h][1]
        sfin_ref[0, h] = ra[h][1]
        sfin_ref[1, h] = rb[h][1]


def _rwkv_scan(r, kkn, v, lw, bt, kd, st, e, bsz, seq):
    n, db = r.shape
    nh = db // HEAD_B
    nc = seq // CH
    fwd = lambda b, c: (b * nc + c, 0)
    bwd = lambda b, c: (b * nc + nc - 1 - c, 0)
    row0 = pl.BlockSpec((CH, db), fwd)
    row1 = pl.BlockSpec((CH, db), bwd)
    dir0 = pl.BlockSpec((None, CH, db), lambda b, c: (0,) + fwd(b, c))
    dir1 = pl.BlockSpec((None, CH, db), lambda b, c: (1,) + bwd(b, c))
    st_in = pl.BlockSpec((None, None, 2, nh, HEAD_B, HEAD_B), lambda b, c: (b, e, 0, 0, 0, 0))
    st_out = pl.BlockSpec((None, 2, nh, HEAD_B, HEAD_B), lambda b, c: (b, 0, 0, 0, 0))
    return pl.pallas_call(
        functools.partial(_rwkv_scan_kernel, c=CH, nh=nh, hd=HEAD_B),
        grid=(bsz, nc),
        in_specs=[row0, row0, row0, dir0, dir0, dir0, row1, row1, row1, dir1, dir1, dir1, st_in],
        out_specs=[row0, row1, st_out],
        out_shape=[jax.ShapeDtypeStruct((n, db), F32), jax.ShapeDtypeStruct((n, db), F32),
                   jax.ShapeDtypeStruct((bsz, 2, nh, HEAD_B, HEAD_B), F32)],
        scratch_shapes=[pltpu.VMEM((2, nh, HEAD_B, HEAD_B), F32)],
        compiler_params=_cparams(("arbitrary", "arbitrary")),
        name="rwkv_scan",
    )(r, kkn, v, lw, bt, kd, r, kkn, v, lw, bt, kd, st)


def _log_sigmoid(x):
    return -_softplus(-x)


def _split3(x):
    hi = x.astype(BF16)
    r1 = x - hi.astype(F32)
    mid = r1.astype(BF16)
    return hi, mid, (r1 - mid.astype(F32)).astype(BF16)


def _gla_chunk(d, c, nh, dk, dv, p_ref, wup, bgk, s_heads):
    _, incl, _ = _tri_masks(c, d)
    hk = nh * dk
    q = p_ref[:, 0:hk] * dk ** -0.5
    k = p_ref[:, hk:2 * hk]
    gk = _dot3(_split(p_ref[:, 2 * hk + nh * dv:]), _split(wup)) + bgk
    la = _log_sigmoid(gk) / GATE_NORM
    la3 = _split3(la)
    mask_bf = incl.astype(BF16)
    cs = _dot(mask_bf, la3[0]) + _dot(mask_bf, la3[1]) + _dot(mask_bf, la3[2])
    tot = cs[c - 1:c] if d == 0 else cs[0:1]
    q_t = _split(q * jnp.exp(cs))
    k_t = _split(k * jnp.exp(-cs))
    k_e = _split(k * jnp.exp(tot - cs))
    ones = jnp.ones((c, dv), BF16)

    def head(h):
        sk = (slice(None), slice(h * dk, (h + 1) * dk))
        vh = _split(p_ref[:, 2 * hk + h * dv:2 * hk + (h + 1) * dv])
        s0 = s_heads[h]
        qh = _cut(q_t, sk)
        att = jnp.where(incl, _dot3(qh, _cut(k_t, sk), 1, 1), 0.0)
        yield
        inter = _dot3(qh, _split(s0))
        yield
        o_out = _dot3(_split(att), vh) + inter
        yield
        dec = jnp.exp(_dot(la3[0][sk], ones, 0, 0) + _dot(la3[1][sk], ones, 0, 0) + _dot(la3[2][sk], ones, 0, 0))
        yield
        yield (o_out, s0 * dec + _dot3(_cut(k_e, sk), vh, 0, 0))

    return [head(h) for h in range(nh)]


def _gla_kernel(p0_ref, p1_ref, s0_ref, wup_ref, bgk_ref, o0_ref, o1_ref, sfin_ref, s_ref, *, c, nh, dk, dv):
    @pl.when(pl.program_id(1) == 0)
    def _():
        s_ref[...] = s0_ref[...]

    s_in = [[s_ref[d, h] for h in range(nh)] for d in range(2)]
    ga = _gla_chunk(0, c, nh, dk, dv, p0_ref, wup_ref[0], bgk_ref[0], s_in[0])
    gb = _gla_chunk(1, c, nh, dk, dv, p1_ref, wup_ref[1], bgk_ref[1], s_in[1])
    res = _staged([g for pair in zip(ga, gb) for g in pair])
    ra, rb = res[0::2], res[1::2]
    o0_ref[...] = jnp.concatenate([o for o, _ in ra], axis=1)
    o1_ref[...] = jnp.concatenate([o for o, _ in rb], axis=1)
    for h in range(nh):
        s_ref[0, h] = ra[h][1]
        s_ref[1, h] = rb[h][1]
        sfin_ref[0, h] = ra[h][1]
        sfin_ref[1, h] = rb[h][1]


def _gla(p, st, o_, wup, bgk, bsz, seq, grid_mode):
    nh, dk, dv = st.shape[3:]
    hv = nh * dv
    nc = seq // CH
    if grid_mode:
        assert seq // GRID_W == CH and nc == GRID_W
        ncol = p.shape[2] // GRID_W
        p_v = p
        blk = lambda w: (None, CH, w)
        fwd = lambda b, c: (b, 0, c)
        bwd = lambda b, c: (b, 0, nc - 1 - c)
        o_shape = jax.ShapeDtypeStruct((bsz, CH, GRID_W * hv), F32)
    else:
        ncol = p.shape[1]
        p_v = p.reshape(bsz * nc, CH, ncol)
        blk = lambda w: (None, CH, w)
        fwd = lambda b, c: (b * nc + c, 0, 0)
        bwd = lambda b, c: (b * nc + nc - 1 - c, 0, 0)
        o_shape = jax.ShapeDtypeStruct((bsz * nc, CH, hv), F32)
    st_in = pl.BlockSpec((None, None, 2, nh, dk, dv), lambda b, c: (b, o_, 0, 0, 0, 0))
    st_out = pl.BlockSpec((None, 2, nh, dk, dv), lambda b, c: (b, 0, 0, 0, 0))
    o0, o1, sfin = pl.pallas_call(
        functools.partial(_gla_kernel, c=CH, nh=nh, dk=dk, dv=dv),
        grid=(bsz, nc),
        in_specs=[pl.BlockSpec(blk(ncol), fwd), pl.BlockSpec(blk(ncol), bwd), st_in,
                  pl.BlockSpec(wup.shape, lambda b, c: (0, 0, 0)),
                  pl.BlockSpec(bgk.shape, lambda b, c: (0, 0, 0))],
        out_specs=[pl.BlockSpec(blk(hv), fwd), pl.BlockSpec(blk(hv), bwd), st_out],
        out_shape=[o_shape, o_shape, jax.ShapeDtypeStruct((bsz, 2, nh, dk, dv), F32)],
        scratch_shapes=[pltpu.VMEM((2, nh, dk, dv), F32)],
        compiler_params=_cparams(("arbitrary", "arbitrary"), VMEM_BIG),
        name="gla",
    )(p_v, p_v, st, wup, bgk)
    if not grid_mode:
        o0, o1 = o0.reshape(bsz * seq, hv), o1.reshape(bsz * seq, hv)
    return o0, o1, sfin


def _router_tail(x1, mod, g2n_ref, rw_ref, x_out, h2_out, aff_out):
    x_out[...] = x1
    h2 = _rms(x1, g2n_ref[...]) * (1.0 + mod[4:5]) + mod[3:4]
    h2_out[...] = h2.astype(BF16)
    logits = _dot_hi(rw_ref[...], h2, 1, 1)
    m = jnp.max(logits, axis=0, keepdims=True)
    ex = jnp.exp(logits - m)
    aff_out[...] = ex / jnp.sum(ex, axis=0, keepdims=True)


def _out_even_kernel(x_ref, mod_ref, ha0_ref, ha1_ref, ga_ref, y0_ref, y1_ref, g_ref, bonus_ref, lnw_ref, lnb_ref,
                     e_ref, wo_ref, g2n_ref, rw_ref, x_out, h2_out, aff_out, *, da, hd):
    mod = mod_ref[...]
    ya = (ha0_ref[...] + ha1_ref[...]) * _gelu_tanh(ga_ref[...])
    y = y0_ref[...] + y1_ref[...]
    e = e_ref[...]
    mean = _seg_sum(y, e) * (1.0 / hd)
    yc = y - mean
    var = _seg_sum(yc * yc, e) * (1.0 / hd)
    yn = yc * lax.rsqrt(var + GN_EPS_B) * lnw_ref[...] + lnb_ref[...]
    yb = (yn + bonus_ref[...]) * g_ref[...]
    out = _dot_bf(ya, wo_ref[0:da, :]) + _dot_bf(yb, wo_ref[da:, :])
    _router_tail(x_ref[...] + mod[2:3] * out, mod, g2n_ref, rw_ref, x_out, h2_out, aff_out)


def _out_odd_kernel(x_ref, mod_ref, o0_ref, o1_ref, g_ref, gw_ref, wo_ref, g2n_ref, rw_ref,
                    x_out, h2_out, aff_out, *scratch, nh, dv, grid_rows):
    mod = mod_ref[...]
    gw = gw_ref[...]
    if grid_rows:
        hv = nh * dv
        o_scr = scratch[0]
        for c in range(GRID_W):
            for j in range(hv // LANES):
                lanes = slice(c * hv + j * LANES, c * hv + (j + 1) * LANES)
                o_scr[j, pl.ds(c, grid_rows, stride=GRID_W), :] = o0_ref[:, lanes] + o1_ref[:, lanes]
        o_all = jnp.concatenate([o_scr[j] for j in range(hv // LANES)], axis=1)
    else:
        o_all = o0_ref[...] + o1_ref[...]
    parts = []
    for h in range(nh):
        o = o_all[:, h * dv:(h + 1) * dv]
        parts.append(o * lax.rsqrt(jnp.mean(o * o, axis=-1, keepdims=True) + EPS) * gw)
    y = jnp.concatenate(parts, axis=1) * _silu(g_ref[...])
    out = _dot_bf(y, wo_ref[...])
    _router_tail(x_ref[...] + mod[2:3] * out, mod, g2n_ref, rw_ref, x_out, h2_out, aff_out)


def _out_common(kern, x, modr, mod_idx, extra_args, extra_specs, norm2_g, router_wt, bsz, seq, name,
                tb=TB, scratch=()):
    n, d = x.shape
    ne = router_wt.shape[0]
    nt = seq // tb
    row = lambda b, t: (b * nt + t, 0)
    in_specs = ([pl.BlockSpec((tb, d), row), pl.BlockSpec((None, 6, d), lambda b, t: (mod_idx(b), 0, 0))]
                + extra_specs
                + [pl.BlockSpec((1, d), lambda b, t: (0, 0)), pl.BlockSpec((ne, d), lambda b, t: (0, 0))])
    return pl.pallas_call(
        kern, grid=(bsz, nt), in_specs=in_specs,
        out_specs=[pl.BlockSpec((tb, d), row), pl.BlockSpec((tb, d), row),
                   pl.BlockSpec((ne, tb), lambda b, t: (0, b * nt + t))],
        out_shape=[jax.ShapeDtypeStruct((n, d), F32), jax.ShapeDtypeStruct((n, d), BF16),
                   jax.ShapeDtypeStruct((ne, n), F32)],
        scratch_shapes=list(scratch),
        compiler_params=_cparams(("arbitrary", "arbitrary"), VMEM_BIG),
        name=name,
    )(x, modr, *extra_args, norm2_g.reshape(1, d), router_wt)


def _select_kernel(aff_ref, slot_ref, cnt_ref, *, cap, ntb, tokb):
    aff = aff_ref[...]
    ne = aff.shape[0]
    big = jnp.float32(4.0)

    def body(carry):
        lo, hi, _, _ = carry
        inr = (aff > lo) & (aff <= hi)
        mn = jnp.min(jnp.where(inr, aff, big), axis=1, keepdims=True)
        mx = jnp.max(jnp.where(inr, aff, -big), axis=1, keepdims=True)
        done = mn >= mx
        piv = mn + (mx - mn) * 0.5
        piv = jnp.where(piv >= mx, mn, piv)
        ge = jnp.sum((aff > piv).astype(jnp.int32), axis=1, keepdims=True) >= cap
        lo = jnp.where(done | ~ge, lo, piv)
        hi = jnp.where(done | ge, hi, piv)
        return lo, hi, mx, jnp.sum((~done).astype(jnp.int32))

    init = (jnp.full((ne, 1), -1.0, F32), jnp.full((ne, 1), big, F32), jnp.zeros((ne, 1), F32), jnp.int32(1))
    _, _, thr, _ = lax.while_loop(lambda carry: carry[3] > 0, body, init)
    gt = aff > thr
    eq = aff == thr
    need = cap - jnp.sum(gt.astype(jnp.int32), axis=1, keepdims=True)
    si = lax.broadcasted_iota(jnp.int32, (tokb, tokb), 0)
    ti = lax.broadcasted_iota(jnp.int32, (tokb, tokb), 1)
    ustrict = (si < ti).astype(BF16)

    def excl_count(mask_blk, carry):
        m = mask_blk.astype(BF16)
        pre = _dot(m, ustrict).astype(jnp.int32) + carry
        return pre, carry + jnp.sum(mask_blk.astype(jnp.int32), axis=1, keepdims=True)

    ceq = jnp.zeros((ne, 1), jnp.int32)
    csel = jnp.zeros((ne, 1), jnp.int32)
    lane = lax.broadcasted_iota(jnp.int32, (ne, 128), 1)
    cnt_acc = jnp.zeros((ne, 128), jnp.int32)
    for j in range(ntb):
        sl = slice(j * tokb, (j + 1) * tokb)
        eq_rank, ceq = excl_count(eq[:, sl], ceq)
        sel = gt[:, sl] | (eq[:, sl] & (eq_rank < need))
        cnt_acc = jnp.where(lane == j, csel, cnt_acc)
        rank, csel = excl_count(sel, csel)
        slot_ref[:, sl] = jnp.where(sel, rank, -1)
    cnt_acc = jnp.where(lane == ntb, csel, cnt_acc)
    cnt_ref[...] = cnt_acc


def _select(aff_t, cap):
    ne, n = aff_t.shape
    ntb = n // TOK_B
    assert ntb < 128
    return pl.pallas_call(
        functools.partial(_select_kernel, cap=cap, ntb=ntb, tokb=TOK_B),
        grid=(1,),
        in_specs=[pl.BlockSpec((ne, n), lambda i: (0, 0))],
        out_specs=[pl.BlockSpec((ne, n), lambda i: (0, 0)), pl.BlockSpec((ne, 128), lambda i: (0, 0))],
        out_shape=[jax.ShapeDtypeStruct((ne, n), jnp.int32), jax.ShapeDtypeStruct((ne, 128), jnp.int32)],
        compiler_params=_cparams(("arbitrary",)),
        name="select",
    )(aff_t)


def _gather_kernel(lo_ref, hi_ref, slot_ref, h2_ref, xe_ref, acc_ref, *, nsb, slotb, tokb):
    e = pl.program_id(0)
    sb = pl.program_id(1)
    base = sb * slotb
    sidx = lax.broadcasted_iota(jnp.int32, (slotb, tokb), 0) + base
    acc_ref[...] = jnp.zeros_like(acc_ref)

    lo = lo_ref[e * nsb + sb]
    hi = hi_ref[e * nsb + sb]

    def picked(tb, live):
        srow = jnp.where(live, slot_ref[pl.ds(tb, 1), :], -1)
        onehot = jnp.where(srow == sidx, 1.0, 0.0).astype(BF16)
        off = pl.multiple_of(tb * tokb, tokb)
        return _dot(onehot, h2_ref[pl.ds(off, tokb), :])

    def body(i, carry):
        tb0 = lo + 2 * i
        tb1 = jnp.minimum(tb0 + 1, hi)
        acc_ref[...] += picked(tb0, True) + picked(tb1, tb0 + 1 <= hi)
        return carry

    lax.fori_loop(0, (hi - lo + 2) // 2, body, 0)
    xe_ref[...] = acc_ref[...].astype(BF16)


def _gather(lo, hi, slot3, h2, cap):
    ne, ntb, tokb = slot3.shape
    n, d = h2.shape
    nsb = cap // SLOT_B
    return pl.pallas_call(
        functools.partial(_gather_kernel, nsb=nsb, slotb=SLOT_B, tokb=tokb),
        grid_spec=pltpu.PrefetchScalarGridSpec(
            num_scalar_prefetch=2, grid=(ne, nsb),
            in_specs=[pl.BlockSpec((None, ntb, tokb), lambda e, s, lo, hi: (e, 0, 0)),
                      pl.BlockSpec((n, d), lambda e, s, lo, hi: (0, 0))],
            out_specs=pl.BlockSpec((None, SLOT_B, d), lambda e, s, lo, hi: (e, s, 0)),
            scratch_shapes=[pltpu.VMEM((SLOT_B, d), F32)]),
        out_shape=jax.ShapeDtypeStruct((ne, cap, d), BF16),
        compiler_params=_cparams(("arbitrary", "arbitrary"), VMEM_BIG),
        name="moe_gather",
    )(lo, hi, slot3, h2)


def _ffn_kernel(xe_ref, wg_ref, wu_ref, wd_ref, ye_ref, acc_ref):
    f = pl.program_id(1)
    xe = xe_ref[...]
    hg = _dot(xe, wg_ref[...].astype(BF16))
    hu = _dot(xe, wu_ref[...].astype(BF16))
    hid = (_silu(hg) * hu).astype(BF16)
    part = _dot(hid, wd_ref[...].astype(BF16))

    @pl.when(f == 0)
    def _():
        acc_ref[...] = part

    @pl.when(f > 0)
    def _():
        acc_ref[...] += part

    @pl.when(f == pl.num_programs(1) - 1)
    def _():
        ye_ref[...] = acc_ref[...].astype(BF16)


def _ffn(xe, w_gate, w_up, w_down):
    ne, cap, d = xe.shape
    dff = w_gate.shape[-1]
    nf = dff // FF_B
    return pl.pallas_call(
        _ffn_kernel,
        grid=(ne, nf),
        in_specs=[pl.BlockSpec((None, cap, d), lambda e, f: (e, 0, 0)),
                  pl.BlockSpec((None, d, FF_B), lambda e, f: (e, 0, f)),
                  pl.BlockSpec((None, d, FF_B), lambda e, f: (e, 0, f)),
                  pl.BlockSpec((None, FF_B, d), lambda e, f: (e, f, 0))],
        out_specs=pl.BlockSpec((None, cap, d), lambda e, f: (e, 0, 0)),
        out_shape=jax.ShapeDtypeStruct((ne, cap, d), BF16),
        scratch_shapes=[pltpu.VMEM((cap, d), F32)],
        compiler_params=_cparams(("arbitrary", "arbitrary"), VMEM_BIG),
        name="moe_ffn",
    )(xe, w_gate, w_up, w_down)


def _scatter_kernel(lo_ref, hi_ref, slot_ref, gate_ref, ye_ref, out_ref, *, nsb, slotb, tokb):
    e = pl.program_id(1)
    sb = pl.program_id(2)

    @pl.when((e == 0) & (sb == 0))
    def _():
        out_ref[...] = jnp.zeros_like(out_ref)

    base = sb * slotb
    sidx = lax.broadcasted_iota(jnp.int32, (slotb, tokb), 0) + base
    ye = ye_ref[...]

    lo = lo_ref[e * nsb + sb]
    hi = hi_ref[e * nsb + sb]

    def spread(tb, live):
        srow = jnp.where(live, slot_ref[pl.ds(tb, 1), :], -1)
        grow = gate_ref[pl.ds(tb, 1), :]
        gm = jnp.where(srow == sidx, grow, 0.0).astype(BF16)
        return _dot(gm, ye, 0, 0)

    def body(i, carry):
        tb0 = lo + 2 * i
        tb1 = jnp.minimum(tb0 + 1, hi)
        add0 = spread(tb0, True)
        add1 = spread(tb1, tb0 + 1 <= hi)
        out_ref[pl.ds(pl.multiple_of(tb0 * tokb, tokb), tokb), :] += add0
        out_ref[pl.ds(pl.multiple_of(tb1 * tokb, tokb), tokb), :] += add1
        return carry

    lax.fori_loop(0, (hi - lo + 2) // 2, body, 0)


def _scatter(lo, hi, slot3, gate3, ye, n):
    ne, cap, d = ye.shape
    _, ntb, tokb = slot3.shape
    nsb = cap // SLOT_B
    dh = d // 2
    return pl.pallas_call(
        functools.partial(_scatter_kernel, nsb=nsb, slotb=SLOT_B, tokb=tokb),
        grid_spec=pltpu.PrefetchScalarGridSpec(
            num_scalar_prefetch=2, grid=(2, ne, nsb),
            in_specs=[pl.BlockSpec((None, ntb, tokb), lambda c, e, s, lo, hi: (e, 0, 0)),
                      pl.BlockSpec((None, ntb, tokb), lambda c, e, s, lo, hi: (e, 0, 0)),
                      pl.BlockSpec((None, SLOT_B, dh), lambda c, e, s, lo, hi: (e, s, c))],
            out_specs=pl.BlockSpec((n, dh), lambda c, e, s, lo, hi: (0, c))),
        out_shape=jax.ShapeDtypeStruct((n, d), F32),
        compiler_params=_cparams(("arbitrary", "arbitrary", "arbitrary"), VMEM_BIG),
        name="moe_scatter",
    )(lo, hi, slot3, gate3, ye)


def _moe(h2, aff_t, w_gate, w_up, w_down):
    n, d = h2.shape
    ne = aff_t.shape[0]
    cap = max(1, CAP_FACTOR * n // ne)
    assert cap % SLOT_B == 0 and n % TOK_B == 0
    ntb = n // TOK_B
    nsb = cap // SLOT_B
    slot, cnt = _select(aff_t, cap)
    cnt = cnt[:, :ntb + 1]
    starts = (jnp.arange(nsb, dtype=jnp.int32) * SLOT_B)[None, :, None]
    lo = jnp.clip(jnp.sum(cnt[:, None, 1:] <= starts, -1), 0, ntb - 1).astype(jnp.int32).reshape(-1)
    hi = jnp.clip(jnp.sum(cnt[:, None, 1:] < starts + SLOT_B, -1), 0, ntb - 1).astype(jnp.int32).reshape(-1)
    slot3 = slot.reshape(ne, ntb, TOK_B)
    gate3 = aff_t.reshape(ne, ntb, TOK_B)
    xe = _gather(lo, hi, slot3, h2, cap)
    ye = _ffn(xe, w_gate, w_up, w_down)
    return _scatter(lo, hi, slot3, gate3, ye, n)


def _final_kernel(x_ref, moe_ref, mod_ref, g_ref, o_ref):
    x = x_ref[...] + mod_ref[...][5:6] * moe_ref[...]
    o_ref[...] = _rms(x, g_ref[...])


def _final(x, moe, modr, mod_idx, gain, bsz, seq):
    n, d = x.shape
    nt = seq // TB
    row = lambda b, t: (b * nt + t, 0)
    return pl.pallas_call(
        _final_kernel, grid=(bsz, nt),
        in_specs=[pl.BlockSpec((TB, d), row), pl.BlockSpec((TB, d), row),
                  pl.BlockSpec((None, 6, d), lambda b, t: (mod_idx(b), 0, 0)),
                  pl.BlockSpec((1, d), lambda b, t: (0, 0))],
        out_specs=pl.BlockSpec((TB, d), row),
        out_shape=jax.ShapeDtypeStruct((n, d), F32),
        compiler_params=_cparams(("arbitrary", "arbitrary")),
        name="final_norm",
    )(x, moe, modr, gain.reshape(1, d))


def _block_diag(w):
    h, n, _ = w.shape
    eye = jnp.eye(h, dtype=w.dtype)
    return (eye[:, None, :, None] * w[:, :, None, :]).reshape(h * n, h * n)


def _lora_cat(w2):
    _, l, c = w2.shape
    z = jnp.zeros((l, c), w2.dtype)
    return jnp.concatenate([jnp.concatenate([w2[0], z], 1), jnp.concatenate([z, w2[1]], 1)], 0)


def _trunk(x, bsz, seq, cond_base, per_batch, st_rglru, st_rwkv, st_gla, grid_mode, modr, P):
    d_model = x.shape[1]
    depth = P['norm1_g'].shape[0]
    new_rglru, new_rwkv, new_gla = [], [], []
    moe_prev = None
    for l in range(depth):
        mod_idx = (lambda b, l=l: l * 8 + cond_base + b) if per_batch else (lambda b, l=l: l * 8 + cond_base)
        mod_prev = (lambda b, l=l: (l - 1) * 8 + cond_base + b) if per_batch else (lambda b, l=l: (l - 1) * 8 + cond_base)
        router_wt = P['router_w'][l].T
        if l % 2 == 0:
            e = l // 2
            da = P['a_conv_w'].shape[-1]
            db = P['b_k_k'].shape[-1]
            w_in = P['ev_w_in'][e].astype(BF16)
            nb_cols = w_in.shape[1] - 2 * da
            res = _inproj(x, moe_prev, mod_prev, modr, mod_idx, P['norm1_g'][l], w_in, (2 * da, nb_cols), bsz, seq)
            if moe_prev is not None:
                x, pa, pb = res
            else:
                pa, pb = res
            ha, sa = [], []
            for d in range(2):
                wg = jnp.concatenate([_block_diag(P['a_w_rg'][e, d]), _block_diag(P['a_w_ig'][e, d])], 1).astype(BF16)
                bg = jnp.concatenate([P['a_b_rg'][e, d], P['a_b_ig'][e, d]])
                h_d, hf = _rglru(pa, P['a_conv_w'][e], P['a_conv_b'][e], wg, bg, P['a_lam'][e, d],
                                 st_rglru[:, e, d], d, bsz, seq)
                ha.append(h_d)
                sa.append(hf.reshape(bsz, da))
            new_rglru.append(jnp.stack(sa, 1))
            e_bf = _block_diag(jnp.ones((db // HEAD_B, HEAD_B, HEAD_B), BF16))
            r, v, kkn, g, bonus, lw, bt, kd = _rwkv_prep(
                pb, P['b_mu'][e], P['b_w0'][e].reshape(1, 2 * db), _lora_cat(P['b_w2'][e]),
                P['b_a0'][e].reshape(1, 2 * db), _lora_cat(P['b_a2'][e]), P['b_g2'][e].astype(BF16),
                P['b_k_k'][e], P['b_k_a'][e], P['b_r_k'][e], e_bf, bsz, seq)
            y0, y1, sfin = _rwkv_scan(r, kkn, v, lw, bt, kd, st_rwkv, e, bsz, seq)
            ys = [y0, y1]
            new_rwkv.append(sfin)
            nt = seq // TB
            rowh = pl.BlockSpec((TB, da), lambda b, t: (b * nt + t, 0))
            rowg = pl.BlockSpec((TB, da), lambda b, t: (b * nt + t, 1))
            vec = pl.BlockSpec((1, db), lambda b, t: (0, 0))
            x, h2, aff_t = _out_common(
                functools.partial(_out_even_kernel, da=da, hd=HEAD_B), x, modr, mod_idx,
                [ha[0], ha[1], pa, ys[0], ys[1], g, bonus, P['b_ln_w'][e].reshape(1, db), P['b_ln_b'][e].reshape(1, db),
                 e_bf, P['ev_w_out'][e].astype(BF16)],
                [rowh, rowh, rowg, rowh, rowh, rowh, rowh, vec, vec,
                 pl.BlockSpec((db, db), lambda b, t: (0, 0)),
                 pl.BlockSpec((da + db, d_model), lambda b, t: (0, 0))],
                P['norm2_g'][l], router_wt, bsz, seq, "out_even")
        else:
            o_ = l // 2
            w_in = P['od_w_in'][o_]
            nh, dk, dv = st_gla.shape[3:]
            hv = nh * dv
            qkv = 2 * nh * dk + hv
            gin = -(2 * GATE_RANK) % 128 + 2 * GATE_RANK
            w_in = jnp.concatenate([w_in[:, :qkv], w_in[:, qkv + hv:],
                                    jnp.zeros((d_model, gin - 2 * GATE_RANK), w_in.dtype),
                                    w_in[:, qkv:qkv + hv]], axis=1).astype(BF16)
            res = _inproj(x, moe_prev, mod_prev, modr, mod_idx, P['norm1_g'][l], w_in, (qkv + gin, hv), bsz, seq,
                          grid_view=grid_mode)
            if moe_prev is not None:
                x, p, g = res
            else:
                p, g = res
            wup = jnp.stack([jnp.zeros((gin, nh * dk), F32).at[d * GATE_RANK:(d + 1) * GATE_RANK]
                             .set(P['c_w_gk_up'][o_, d]) for d in range(2)])
            o0, o1, sfin = _gla(p, st_gla, o_, wup, P['c_b_gk'][o_].reshape(2, 1, nh * dk), bsz, seq, grid_mode)
            new_gla.append(sfin)
            tb = SUBLANES * GRID_W if grid_mode else TB
            nt = seq // tb
            if grid_mode:
                rowo = pl.BlockSpec((None, SUBLANES, GRID_W * hv), lambda b, t: (b, t, 0))
                scratch = [pltpu.VMEM((hv // LANES, tb, LANES), F32)]
            else:
                rowo = pl.BlockSpec((tb, hv), lambda b, t: (b * nt + t, 0))
                scratch = []
            rowg = pl.BlockSpec((tb, hv), lambda b, t: (b * nt + t, 0))
            x, h2, aff_t = _out_common(
                functools.partial(_out_odd_kernel, nh=nh, dv=dv, grid_rows=SUBLANES if grid_mode else 0),
                x, modr, mod_idx,
                [o0, o1, g, P['c_gnorm_w'][o_].reshape(1, dv), P['od_w_out'][o_].astype(BF16)],
                [rowo, rowo, rowg, pl.BlockSpec((1, dv), lambda b, t: (0, 0)),
                 pl.BlockSpec((hv, d_model), lambda b, t: (0, 0))],
                P['norm2_g'][l], router_wt, bsz, seq, "out_odd", tb=tb, scratch=scratch)
        moe_prev = _moe(h2, aff_t, P['moe_w_gate'][l], P['moe_w_up'][l], P['moe_w_down'][l])
    mod_last = (lambda b: (depth - 1) * 8 + cond_base + b) if per_batch else (lambda b: (depth - 1) * 8 + cond_base)
    y = _final(x, moe_prev, modr, mod_last, P['final_norm_g'], bsz, seq)
    return y, jnp.stack(new_rglru, 1), jnp.stack(new_rwkv, 1), jnp.stack(new_gla, 1)


def kernel(x_prompt, x_sample, state_rglru, state_rwkv, state_gla, c, c_ctx, norm1_g, norm2_g, ada_w, ada_b, router_w, moe_w_gate, moe_w_up, moe_w_down, ev_w_in, ev_w_out, a_conv_w, a_conv_b, a_w_rg, a_b_rg, a_w_ig, a_b_ig, a_lam, b_mu, b_w0, b_w2, b_a0, b_a2, b_g2, b_k_k, b_k_a, b_r_k, b_ln_w, b_ln_b, od_w_in, od_w_out, c_w_gk_up, c_b_gk, c_gnorm_w, final_norm_g):
    P = dict(norm1_g=norm1_g, norm2_g=norm2_g, router_w=router_w,
             moe_w_gate=moe_w_gate, moe_w_up=moe_w_up, moe_w_down=moe_w_down,
             ev_w_in=ev_w_in, ev_w_out=ev_w_out, a_conv_w=a_conv_w, a_conv_b=a_conv_b,
             a_w_rg=a_w_rg, a_b_rg=a_b_rg, a_w_ig=a_w_ig, a_b_ig=a_b_ig, a_lam=a_lam,
             b_mu=b_mu, b_w0=b_w0, b_w2=b_w2, b_a0=b_a0, b_a2=b_a2, b_g2=b_g2, b_k_k=b_k_k,
             b_k_a=b_k_a, b_r_k=b_r_k, b_ln_w=b_ln_w, b_ln_b=b_ln_b,
             od_w_in=od_w_in, od_w_out=od_w_out, c_w_gk_up=c_w_gk_up, c_b_gk=c_b_gk,
             c_gnorm_w=c_gnorm_w, final_norm_g=final_norm_g)
    bp, tp, d_model = x_prompt.shape
    bs, ts, _ = x_sample.shape
    assert bs + 1 <= 8 and tp % TB == 0 and ts % TB == 0
    depth = ada_w.shape[0]
    cv8 = jnp.zeros((8, d_model), F32).at[0].set(c_ctx).at[1:1 + bs].set(c)
    modr = _modulation(cv8, ada_w, ada_b).reshape(depth * 8, 6, d_model)
    dt = x_prompt.dtype
    z_rglru = jnp.zeros((bp,) + state_rglru.shape[1:], dt)
    z_rwkv = jnp.zeros((bp,) + state_rwkv.shape[1:], dt)
    z_gla = jnp.zeros((bp,) + state_gla.shape[1:], dt)
    y_prompt, n_rglru, n_rwkv, n_gla = _trunk(x_prompt.reshape(bp * tp, d_model), bp, tp, 0, False,
                                              z_rglru, z_rwkv, z_gla, False, modr, P)
    y_sample, _, _, _ = _trunk(x_sample.reshape(bs * ts, d_model), bs, ts, 1, True,
                               state_rglru, state_rwkv, state_gla, True, modr, P)
    return (y_prompt.reshape(bp, tp, d_model), y_sample.reshape(bs, ts, d_model), n_rglru, n_rwkv, n_gla)
```

```python
import functools
import math

import numpy as np
import jax
import jax.numpy as jnp
from jax import lax
from jax.experimental import pallas as pl
from jax.experimental.pallas import tpu as pltpu

F32 = jnp.float32
BF16 = jnp.bfloat16
HI = lax.Precision.HIGHEST

GRID_W = 64
EPS = 1e-6
C_RGLRU = 8.0
GN_EPS_B = 64e-5
GATE_NORM = 16.0
CAP_FACTOR = 2
HEAD_B = 64
H_A = 8
H_C = 4
GATE_RANK = 16

SUBLANES = 8
LANES = 128
MXU_N = 256
TB = 256
CH = 64
SUB = 16
SLOT_B = 256
TOK_B = 256
FF_B = 512
VMEM_BIG = 56 * 1024 * 1024


def _dot(a, b, ca=1, cb=0, prec=None):
    return lax.dot_general(a, b, (((ca,), (cb,)), ((), ())), precision=prec,
                           preferred_element_type=F32)


def _dot_bf(a, b, ca=1, cb=0):
    return _dot(a.astype(BF16), b.astype(BF16), ca, cb)


def _dot_hi(a, b, ca=1, cb=0):
    return _dot(a.astype(F32), b.astype(F32), ca, cb, HI)


def _seg_sum(x, e_bf):
    hi = x.astype(BF16)
    lo = (x - hi.astype(F32)).astype(BF16)
    return _dot(hi, e_bf) + _dot(lo, e_bf)


def _mask_dot(mask_bf, x, cm=1, cx=0):
    hi = x.astype(BF16)
    r1 = x - hi.astype(F32)
    mid = r1.astype(BF16)
    lo = (r1 - mid.astype(F32)).astype(BF16)
    return _dot(mask_bf, hi, cm, cx) + _dot(mask_bf, mid, cm, cx) + _dot(mask_bf, lo, cm, cx)


def _sigmoid(x):
    return 1.0 / (1.0 + jnp.exp(-x))


def _softplus(x):
    return jnp.maximum(x, 0.0) + jnp.log1p(jnp.exp(-jnp.abs(x)))


def _silu(x):
    return x * _sigmoid(x)


def _gelu_tanh(x):
    return 0.5 * x * (1.0 + jnp.tanh(math.sqrt(2.0 / math.pi) * (x + 0.044715 * (x * x * x))))


def _rms(x, g):
    return x * lax.rsqrt(jnp.mean(x * x, axis=-1, keepdims=True) + EPS) * g


def _cparams(sem, vmem=None):
    return pltpu.CompilerParams(dimension_semantics=sem, vmem_limit_bytes=vmem)


def _mod_kernel(c_ref, w_ref, b_ref, o_ref):
    o_ref[...] = _dot_hi(_silu(c_ref[...]), w_ref[...]) + b_ref[...]


def _modulation(cv8, ada_w, ada_b):
    depth, d, n6 = ada_w.shape
    nb = 1536
    return pl.pallas_call(
        _mod_kernel,
        grid=(depth, n6 // nb),
        in_specs=[pl.BlockSpec((8, d), lambda l, j: (0, 0)),
                  pl.BlockSpec((None, d, nb), lambda l, j: (l, 0, j)),
                  pl.BlockSpec((None, 1, nb), lambda l, j: (l, 0, j))],
        out_specs=pl.BlockSpec((None, 8, nb), lambda l, j: (l, 0, j)),
        out_shape=jax.ShapeDtypeStruct((depth, 8, n6), F32),
        compiler_params=_cparams(("arbitrary", "arbitrary")),
        name="modulation",
    )(cv8, ada_w, ada_b.reshape(depth, 1, n6))


def _project(x, mod, g_ref, w_ref, outs, splits, grid_rows):
    h = _rms(x, g_ref[...]) * (1.0 + mod[1:2]) + mod[0:1]
    p = _dot_bf(h, w_ref[...])
    lo = 0
    if grid_rows:
        n0 = splits[0]
        scr = outs[-1]
        for j in range(n0 // LANES):
            scr[j] = p[:, j * LANES:(j + 1) * LANES]
        for c in range(GRID_W):
            for j in range(n0 // LANES):
                outs[0][:, c * n0 + j * LANES:c * n0 + (j + 1) * LANES] = scr[j, pl.ds(c, grid_rows, stride=GRID_W), :]
        lo, outs, splits = n0, outs[1:-1], splits[1:]
    for o_ref, n in zip(outs, splits):
        o_ref[...] = p[:, lo:lo + n]
        lo += n


def _inproj_kernel(x_ref, mod_ref, g_ref, w_ref, *outs, splits, grid_rows):
    _project(x_ref[...], mod_ref[...], g_ref, w_ref, outs, splits, grid_rows)


def _inproj_res_kernel(x_ref, moe_ref, modp_ref, mod_ref, g_ref, w_ref, xo_ref, *outs, splits, grid_rows):
    x = x_ref[...] + modp_ref[...][5:6] * moe_ref[...]
    xo_ref[...] = x
    _project(x, mod_ref[...], g_ref, w_ref, outs, splits, grid_rows)


def _inproj(x, moe_prev, mod_prev_idx, modr, mod_idx, gain, w_bf, splits, bsz, seq, grid_view=False):
    n, d = x.shape
    tb = SUBLANES * GRID_W if grid_view else TB
    nt = seq // tb
    ncols = w_bf.shape[1]
    with_res = moe_prev is not None
    row = lambda b, t: (b * nt + t, 0)
    in_specs = [pl.BlockSpec((tb, d), row)]
    args = [x]
    if with_res:
        in_specs += [pl.BlockSpec((tb, d), row),
                     pl.BlockSpec((None, 6, d), lambda b, t: (mod_prev_idx(b), 0, 0))]
        args += [moe_prev, modr]
    in_specs += [pl.BlockSpec((None, 6, d), lambda b, t: (mod_idx(b), 0, 0))]
    args += [modr]
    in_specs += [pl.BlockSpec((1, d), lambda b, t: (0, 0)),
                 pl.BlockSpec((d, ncols), lambda b, t: (0, 0), pipeline_mode=pl.Buffered(1))]
    args += [gain.reshape(1, d), w_bf]
    out_shape = [jax.ShapeDtypeStruct((n, c), F32) for c in splits]
    out_specs = [pl.BlockSpec((tb, c), row) for c in splits]
    scratch = []
    grid_rows = 0
    if grid_view:
        grid_rows = tb // GRID_W
        out_shape[0] = jax.ShapeDtypeStruct((bsz, seq // GRID_W, GRID_W * splits[0]), F32)
        out_specs[0] = pl.BlockSpec((None, grid_rows, GRID_W * splits[0]), lambda b, t: (b, t, 0))
        scratch = [pltpu.VMEM((splits[0] // LANES, tb, LANES), F32)]
    if with_res:
        out_shape = [jax.ShapeDtypeStruct((n, d), F32)] + out_shape
        out_specs = [pl.BlockSpec((tb, d), row)] + out_specs
        kern = functools.partial(_inproj_res_kernel, splits=splits, grid_rows=grid_rows)
    else:
        kern = functools.partial(_inproj_kernel, splits=splits, grid_rows=grid_rows)
    return pl.pallas_call(
        kern, grid=(bsz, nt), in_specs=in_specs, out_specs=out_specs, out_shape=out_shape,
        scratch_shapes=scratch,
        compiler_params=_cparams(("arbitrary", "arbitrary"), VMEM_BIG),
        name="inproj",
    )(*args)


def _fill_ext(ext_ref, x, prev_ref, next_ref, first, last, tb):
    ext_ref[0:8, :] = jnp.where(first, 0.0, prev_ref[...])
    ext_ref[8:8 + tb, :] = x
    ext_ref[8 + tb:16 + tb, :] = jnp.where(last, 0.0, next_ref[...])


def _rglru_kernel(xa_ref, prev_ref, next_ref, cw_ref, cb_ref, wg_ref, bg_ref, lam_ref, h0_ref,
                  h_ref, hfin_ref, ext_ref, carry_ref, *, d, nt, tb):
    i = pl.program_id(1)
    t = i if d == 0 else nt - 1 - i
    first = t == 0
    last = t == nt - 1
    x = xa_ref[...]
    da = x.shape[1]
    _fill_ext(ext_ref, x, prev_ref, next_ref, first, last, tb)
    cw = cw_ref[...]
    u = (cw[0:1] * ext_ref[7:7 + tb, :] + cw[1:2] * x + cw[2:3] * ext_ref[9:9 + tb, :]
         + cw[3:4] * ext_ref[10:10 + tb, :] + cb_ref[...])
    gts = _dot_bf(u, wg_ref[...]) + bg_ref[...]
    gate_r = _sigmoid(gts[:, :da])
    gate_i = _sigmoid(gts[:, da:])
    log_a = -C_RGLRU * gate_r * _softplus(-lam_ref[...])
    a = jnp.exp(log_a)
    xin = jnp.sqrt(1.0 - jnp.exp(2.0 * log_a)) * gate_i * u

    @pl.when(i == 0)
    def _():
        carry_ref[...] = h0_ref[...]

    rows = lax.broadcasted_iota(jnp.int32, (tb, da), 0)
    s = 1
    while s < tb:
        if d == 0:
            a_sh = pltpu.roll(a, s, 0)
            x_sh = pltpu.roll(xin, s, 0)
            valid = rows >= s
        else:
            a_sh = pltpu.roll(a, tb - s, 0)
            x_sh = pltpu.roll(xin, tb - s, 0)
            valid = rows < tb - s
        xin = jnp.where(valid, a * x_sh + xin, xin)
        a = jnp.where(valid, a * a_sh, a)
        s *= 2
    h = xin + a * carry_ref[...]
    h_ref[...] = h
    newc = h[tb - 1:tb, :] if d == 0 else h[0:1, :]
    carry_ref[...] = newc
    hfin_ref[...] = newc


def _rglru(pa, conv_w, conv_b, wg_d, bg_d, lam_d, h0_d, d, bsz, seq):
    n = pa.shape[0]
    da = conv_w.shape[1]
    nt = seq // TB
    r8 = TB // 8
    tt = (lambda t: t) if d == 0 else (lambda t: nt - 1 - t)
    blk = lambda b, t: b * nt + tt(t)
    return pl.pallas_call(
        functools.partial(_rglru_kernel, d=d, nt=nt, tb=TB),
        grid=(bsz, nt),
        in_specs=[pl.BlockSpec((TB, da), lambda b, t: (blk(b, t), 0)),
                  pl.BlockSpec((8, da), lambda b, t: (jnp.maximum(blk(b, t) * r8 - 1, 0), 0)),
                  pl.BlockSpec((8, da), lambda b, t: (jnp.minimum((blk(b, t) + 1) * r8, n // 8 - 1), 0)),
                  pl.BlockSpec((4, da), lambda b, t: (0, 0)),
                  pl.BlockSpec((1, da), lambda b, t: (0, 0)),
                  pl.BlockSpec((da, 2 * da), lambda b, t: (0, 0)),
                  pl.BlockSpec((1, 2 * da), lambda b, t: (0, 0)),
                  pl.BlockSpec((1, da), lambda b, t: (0, 0)),
                  pl.BlockSpec((None, 1, da), lambda b, t: (b, 0, 0))],
        out_specs=[pl.BlockSpec((TB, da), lambda b, t: (blk(b, t), 0)),
                   pl.BlockSpec((None, 1, da), lambda b, t: (b, 0, 0))],
        out_shape=[jax.ShapeDtypeStruct((n, da), F32), jax.ShapeDtypeStruct((bsz, 1, da), F32)],
        scratch_shapes=[pltpu.VMEM((TB + 16, da), F32), pltpu.VMEM((1, da), F32)],
        compiler_params=_cparams(("arbitrary", "arbitrary")),
        name=f"rglru_d{d}",
    )(pa, pa, pa, conv_w, conv_b.reshape(1, da), wg_d, bg_d.reshape(1, 2 * da), lam_d.reshape(1, da),
      h0_d.reshape(bsz, 1, da))


def _rwkv_prep_kernel(pb_ref, prev_ref, next_ref, mu_ref, w0_ref, w2_ref, a0_ref, a2_ref, g2_ref,
                      kk_ref, ka_ref, rk_ref, e_ref,
                      r_out, v_out, kk_out, g_out, bonus_out, lw_out, bt_out, kd_out, ext_ref, *, nt, tb, db, lw_, la_):
    t = pl.program_id(1)
    x = pb_ref[...]
    _fill_ext(ext_ref, x, prev_ref, next_ref, t == 0, t == nt - 1, tb)
    sh = 0.5 * (ext_ref[7:7 + tb, :] + ext_ref[9:9 + tb, :])
    z = x + mu_ref[...] * (sh - x)
    r = z[:, 0:db]
    k = z[:, db:2 * db]
    v = z[:, 2 * db:3 * db]
    o = 3 * db
    zw = z[:, o:o + 2 * lw_]
    za = z[:, o + 2 * lw_:o + 2 * lw_ + 2 * la_]
    zg = z[:, o + 2 * lw_ + 2 * la_:]
    w = -_softplus(-(w0_ref[...] + _dot_hi(jnp.tanh(zw), w2_ref[...]))) - 0.5
    lw = -jnp.exp(w)
    a = _sigmoid(a0_ref[...] + _dot_hi(za, a2_ref[...]))
    g = _dot_bf(_sigmoid(zg), g2_ref[...])
    e = e_ref[...]
    kkv = k * kk_ref[...]
    ss = _seg_sum(kkv * kkv, e)
    kkn = kkv / jnp.maximum(jnp.sqrt(ss), 1e-12)
    ka = ka_ref[...]
    kd0 = k * (1.0 + (a[:, :db] - 1.0) * ka)
    kd1 = k * (1.0 + (a[:, db:] - 1.0) * ka)
    bonus = _seg_sum(r * rk_ref[...] * (kd0 + kd1), e) * v
    r_out[...] = r
    v_out[...] = v
    kk_out[...] = kkn
    g_out[...] = g
    bonus_out[...] = bonus
    lw_out[0] = lw[:, :db]
    lw_out[1] = lw[:, db:]
    bt_out[0] = kkn * a[:, :db]
    bt_out[1] = kkn * a[:, db:]
    kd_out[0] = kd0
    kd_out[1] = kd1


def _rwkv_prep(pb, mu, w0c, w2c, a0c, a2c, g2, k_k, k_a, r_k, e_bf, bsz, seq):
    n, ncol = pb.shape
    db = k_k.shape[-1]
    nt = seq // TB
    r8 = TB // 8
    blk = lambda b, t: b * nt + t
    full = lambda shp: pl.BlockSpec(shp, lambda b, t: (0,) * len(shp))
    row = pl.BlockSpec((TB, db), lambda b, t: (blk(b, t), 0))
    row2 = pl.BlockSpec((2, TB, db), lambda b, t: (0, blk(b, t), 0))
    s1 = jax.ShapeDtypeStruct((n, db), F32)
    s2 = jax.ShapeDtypeStruct((2, n, db), F32)
    lw_ = w2c.shape[0] // 2
    la_ = a2c.shape[0] // 2
    return pl.pallas_call(
        functools.partial(_rwkv_prep_kernel, nt=nt, tb=TB, db=db, lw_=lw_, la_=la_),
        grid=(bsz, nt),
        in_specs=[pl.BlockSpec((TB, ncol), lambda b, t: (blk(b, t), 0)),
                  pl.BlockSpec((8, ncol), lambda b, t: (jnp.maximum(blk(b, t) * r8 - 1, 0), 0)),
                  pl.BlockSpec((8, ncol), lambda b, t: (jnp.minimum((blk(b, t) + 1) * r8, n // 8 - 1), 0)),
                  full((1, ncol)), full((1, 2 * db)), full(w2c.shape), full((1, 2 * db)), full(a2c.shape),
                  full(g2.shape), full((1, db)), full((1, db)), full((1, db)), full((db, db))],
        out_specs=[row, row, row, row, row, row2, row2, row2],
        out_shape=[s1, s1, s1, s1, s1, s2, s2, s2],
        scratch_shapes=[pltpu.VMEM((TB + 16, ncol), F32)],
        compiler_params=_cparams(("arbitrary", "arbitrary"), VMEM_BIG),
        name="rwkv_prep",
    )(pb, pb, pb, mu.reshape(1, ncol), w0c, w2c, a0c, a2c, g2, k_k.reshape(1, db), k_a.reshape(1, db),
      r_k.reshape(1, db), e_bf)


def _tri_masks(c, d):
    ti = lax.broadcasted_iota(jnp.int32, (c, c), 0)
    si = lax.broadcasted_iota(jnp.int32, (c, c), 1)
    if d == 0:
        return si < ti, si <= ti, ti == si
    return si > ti, si >= ti, ti == si


def _split(x):
    hi = x.astype(BF16)
    return hi, (x - hi.astype(F32)).astype(BF16)


def _cut(p, sl):
    return p[0][sl], p[1][sl]


def _dot3(a, b, ca=1, cb=0):
    m = a[0].shape[1 - ca]
    n = b[0].shape[1 - cb]
    if not (ca == 1 and cb == 1 and 2 * n == MXU_N):
        return _dot(a[0], b[0], ca, cb) + _dot(a[1], b[0], ca, cb) + _dot(a[0], b[1], ca, cb)
    q = _dot(jnp.concatenate(a, axis=0), jnp.concatenate(b, axis=0), ca, cb)
    return (q[:m, :n] + q[:m, n:]) + (q[m:, :n] + q[m:, n:])


def _staged(gens):
    results = [None] * len(gens)
    live = list(range(len(gens)))
    while live:
        still = []
        for i in live:
            try:
                out = next(gens[i])
            except StopIteration:
                continue
            if out is not None:
                results[i] = out
            still.append(i)
        live = still
    return results


def _rwkv_chunk(d, c, nh, hd, r, kk, v, lw, bt, kd, s_heads):
    strict, incl, diag = _tri_masks(c, d)
    cs = _mask_dot(incl.astype(BF16), lw)
    tot = cs[c - 1:c] if d == 0 else cs[0:1]
    einv = jnp.exp(-cs)
    eend = jnp.exp(tot - cs)
    gtot = jnp.exp(tot)
    a_t = _split(-kk * jnp.exp(cs - lw))
    r_t = _split(r * jnp.exp(cs))
    b_t = _split(bt * einv)
    k_t = _split(kd * einv)
    b_g = _split(bt * eend)
    k_g = _split(kd * eend)
    vs = _split(v)
    eye = diag.astype(F32)
    t2 = lax.broadcasted_iota(jnp.int32, (c, 2 * c), 0)
    s2 = lax.broadcasted_iota(jnp.int32, (c, 2 * c), 1)
    s2 = jnp.where(s2 >= c, s2 - c, s2)
    incl2 = (s2 <= t2) if d == 0 else (s2 >= t2)
    nsq = int(round(math.log2(SUB))) - 1
    ti = lax.broadcasted_iota(jnp.int32, (c, c), 0)
    si = lax.broadcasted_iota(jnp.int32, (c, c), 1)
    blk_masks = [ti // SUB == si // SUB]
    w = SUB
    while w < c:
        blk_masks.append((ti // (2 * w) == si // (2 * w)) & (ti // w != si // w))
        w *= 2
    def head(h):
        sl = (slice(None), slice(h * hd, (h + 1) * hd))
        s0 = s_heads[h]
        s0s = _split(s0)
        ar = tuple(jnp.concatenate([x[sl], y[sl]], axis=0) for x, y in zip(a_t, r_t))
        bk = tuple(jnp.concatenate([x[sl], y[sl]], axis=0) for x, y in zip(b_t, k_t))
        gm = _dot3(ar, bk, 1, 1)
        yield
        a_ab = jnp.where(strict, gm[:c, :c], 0.0)
        a_ak = jnp.where(strict, gm[:c, c:], 0.0)
        ars =_dot3(ar, s0s, 1, 1)
        yield
        vh = _cut(vs, sl)
        x = ars[:c] + _dot3(_split(a_ak), vh)
        yield
        nd = jnp.where(blk_masks[0], a_ab, 0.0)
        tinv = eye + nd
        p = _split(nd)
        for _ in range(nsq):
            p = _split(_dot3(p, p))
            yield
            tinv = tinv + _dot3(_split(tinv), p)
            yield
        for m in blk_masks[1:]:
            ts = _split(tinv)
            half = _split(_dot3(ts, _split(jnp.where(m, a_ab, 0.0))))
            yield
            tinv = tinv + _dot3(half, ts)
            yield
        u = _dot3(_split(tinv), _split(x))
        yield
        us = _split(u)
        uv = tuple(jnp.concatenate([x, y], axis=0) for x, y in zip(us, vh))
        y_out = ars[c:] + _dot(jnp.where(incl2, gm[c:], 0.0).astype(BF16), uv[0])
        yield
        bkg = tuple(jnp.concatenate([x[sl], y[sl]], axis=0) for x, y in zip(b_g, k_g))
        s_out = s0 * gtot[sl] + _dot3(uv, bkg, 0, 0)
        yield (y_out, s_out)

    return [head(h) for h in range(nh)]


def _rwkv_scan_kernel(r0_ref, kk0_ref, v0_ref, lw0_ref, bt0_ref, kd0_ref,
                      r1_ref, kk1_ref, v1_ref, lw1_ref, bt1_ref, kd1_ref, s0_ref,
                      y0_ref, y1_ref, sfin_ref, s_ref, *, c, nh, hd):
    @pl.when(pl.program_id(1) == 0)
    def _():
        s_ref[...] = s0_ref[...]

    s_in = [[s_ref[d, h] for h in range(nh)] for d in range(2)]
    ga = _rwkv_chunk(0, c, nh, hd, r0_ref[...], kk0_ref[...], v0_ref[...], lw0_ref[...], bt0_ref[...],
                     kd0_ref[...], s_in[0])
    gb = _rwkv_chunk(1, c, nh, hd, r1_ref[...], kk1_ref[...], v1_ref[...], lw1_ref[...], bt1_ref[...],
                     kd1_ref[...], s_in[1])
    res = _staged([g for pair in zip(ga, gb) for g in pair])
    ra, rb = res[0::2], res[1::2]
    y0_ref[...] = jnp.concatenate([y for y, _ in ra], axis=1)
    y1_ref[...] = jnp.concatenate([y for y, _ in rb], axis=1)
    for h in range(nh):
        s_ref[0, h] = ra[h][1]
        s_ref[1, h] = rb[h][1]
        sfin_ref[0, h] = ra[h][1]
        sfin_ref[1, h] = rb[h][1]


def _rwkv_scan(r, kkn, v, lw, bt, kd, st, e, bsz, seq):
    n, db = r.shape
    nh = db // HEAD_B
    nc = seq // CH
    fwd = lambda b, c: (b * nc + c, 0)
    bwd = lambda b, c: (b * nc + nc - 1 - c, 0)
    row0 = pl.BlockSpec((CH, db), fwd)
    row1 = pl.BlockSpec((CH, db), bwd)
    dir0 = pl.BlockSpec((None, CH, db), lambda b, c: (0,) + fwd(b, c))
    dir1 = pl.BlockSpec((None, CH, db), lambda b, c: (1,) + bwd(b, c))
    st_in = pl.BlockSpec((None, None, 2, nh, HEAD_B, HEAD_B), lambda b, c: (b, e, 0, 0, 0, 0))
    st_out = pl.BlockSpec((None, 2, nh, HEAD_B, HEAD_B), lambda b, c: (b, 0, 0, 0, 0))
    return pl.pallas_call(
        functools.partial(_rwkv_scan_kernel, c=CH, nh=nh, hd=HEAD_B),
        grid=(bsz, nc),
        in_specs=[row0, row0, row0, dir0, dir0, dir0, row1, row1, row1, dir1, dir1, dir1, st_in],
        out_specs=[row0, row1, st_out],
        out_shape=[jax.ShapeDtypeStruct((n, db), F32), jax.ShapeDtypeStruct((n, db), F32),
                   jax.ShapeDtypeStruct((bsz, 2, nh, HEAD_B, HEAD_B), F32)],
        scratch_shapes=[pltpu.VMEM((2, nh, HEAD_B, HEAD_B), F32)],
        compiler_params=_cparams(("arbitrary", "arbitrary")),
        name="rwkv_scan",
    )(r, kkn, v, lw, bt, kd, r, kkn, v, lw, bt, kd, st)


def _log_sigmoid(x):
    return -_softplus(-x)


def _split3(x):
    hi = x.astype(BF16)
    r1 = x - hi.astype(F32)
    mid = r1.astype(BF16)
    return hi, mid, (r1 - mid.astype(F32)).astype(BF16)


def _gla_chunk(d, c, nh, dk, dv, p_ref, wup, bgk, s_heads):
    _, incl, _ = _tri_masks(c, d)
    hk = nh * dk
    q = p_ref[:, 0:hk] * dk ** -0.5
    k = p_ref[:, hk:2 * hk]
    gk = _dot3(_split(p_ref[:, 2 * hk + nh * dv:]), _split(wup)) + bgk
    la = _log_sigmoid(gk) / GATE_NORM
    la3 = _split3(la)
    mask_bf = incl.astype(BF16)
    cs = _dot(mask_bf, la3[0]) + _dot(mask_bf, la3[1]) + _dot(mask_bf, la3[2])
    tot = cs[c - 1:c] if d == 0 else cs[0:1]
    q_t = _split(q * jnp.exp(cs))
    k_t = _split(k * jnp.exp(-cs))
    k_e = _split(k * jnp.exp(tot - cs))
    ones = jnp.ones((c, dv), BF16)

    def head(h):
        sk = (slice(None), slice(h * dk, (h + 1) * dk))
        vh = _split(p_ref[:, 2 * hk + h * dv:2 * hk + (h + 1) * dv])
        s0 = s_heads[h]
        qh = _cut(q_t, sk)
        att = jnp.where(incl, _dot3(qh, _cut(k_t, sk), 1, 1), 0.0)
        yield
        inter = _dot3(qh, _split(s0))
        yield
        o_out = _dot3(_split(att), vh) + inter
        yield
        dec = jnp.exp(_dot(la3[0][sk], ones, 0, 0) + _dot(la3[1][sk], ones, 0, 0) + _dot(la3[2][sk], ones, 0, 0))
        yield
        yield (o_out, s0 * dec + _dot3(_cut(k_e, sk), vh, 0, 0))

    return [head(h) for h in range(nh)]


def _gla_kernel(p0_ref, p1_ref, s0_ref, wup_ref, bgk_ref, o0_ref, o1_ref, sfin_ref, s_ref, *, c, nh, dk, dv):
    @pl.when(pl.program_id(1) == 0)
    def _():
        s_ref[...] = s0_ref[...]

    s_in = [[s_ref[d, h] for h in range(nh)] for d in range(2)]
    ga = _gla_chunk(0, c, nh, dk, dv, p0_ref, wup_ref[0], bgk_ref[0], s_in[0])
    gb = _gla_chunk(1, c, nh, dk, dv, p1_ref, wup_ref[1], bgk_ref[1], s_in[1])
    res = _staged([g for pair in zip(ga, gb) for g in pair])
    ra, rb = res[0::2], res[1::2]
    o0_ref[...] = jnp.concatenate([o for o, _ in ra], axis=1)
    o1_ref[...] = jnp.concatenate([o for o, _ in rb], axis=1)
    for h in range(nh):
        s_ref[0, h] = ra[h][1]
        s_ref[1, h] = rb[h][1]
        sfin_ref[0, h] = ra[h][1]
        sfin_ref[1, h] = rb[h][1]


def _gla(p, st, o_, wup, bgk, bsz, seq, grid_mode):
    nh, dk, dv = st.shape[3:]
    hv = nh * dv
    nc = seq // CH
    if grid_mode:
        assert seq // GRID_W == CH and nc == GRID_W
        ncol = p.shape[2] // GRID_W
        p_v = p
        blk = lambda w: (None, CH, w)
        fwd = lambda b, c: (b, 0, c)
        bwd = lambda b, c: (b, 0, nc - 1 - c)
        o_shape = jax.ShapeDtypeStruct((bsz, CH, GRID_W * hv), F32)
    else:
        ncol = p.shape[1]
        p_v = p.reshape(bsz * nc, CH, ncol)
        blk = lambda w: (None, CH, w)
        fwd = lambda b, c: (b * nc + c, 0, 0)
        bwd = lambda b, c: (b * nc + nc - 1 - c, 0, 0)
        o_shape = jax.ShapeDtypeStruct((bsz * nc, CH, hv), F32)
    st_in = pl.BlockSpec((None, None, 2, nh, dk, dv), lambda b, c: (b, o_, 0, 0, 0, 0))
    st_out = pl.BlockSpec((None, 2, nh, dk, dv), lambda b, c: (b, 0, 0, 0, 0))
    o0, o1, sfin = pl.pallas_call(
        functools.partial(_gla_kernel, c=CH, nh=nh, dk=dk, dv=dv),
        grid=(bsz, nc),
        in_specs=[pl.BlockSpec(blk(ncol), fwd), pl.BlockSpec(blk(ncol), bwd), st_in,
                  pl.BlockSpec(wup.shape, lambda b, c: (0, 0, 0)),
                  pl.BlockSpec(bgk.shape, lambda b, c: (0, 0, 0))],
        out_specs=[pl.BlockSpec(blk(hv), fwd), pl.BlockSpec(blk(hv), bwd), st_out],
        out_shape=[o_shape, o_shape, jax.ShapeDtypeStruct((bsz, 2, nh, dk, dv), F32)],
        scratch_shapes=[pltpu.VMEM((2, nh, dk, dv), F32)],
        compiler_params=_cparams(("arbitrary", "arbitrary"), VMEM_BIG),
        name="gla",
    )(p_v, p_v, st, wup, bgk)
    if not grid_mode:
        o0, o1 = o0.reshape(bsz * seq, hv), o1.reshape(bsz * seq, hv)
    return o0, o1, sfin


def _router_tail(x1, mod, g2n_ref, rw_ref, x_out, h2_out, aff_out):
    x_out[...] = x1
    h2 = _rms(x1, g2n_ref[...]) * (1.0 + mod[4:5]) + mod[3:4]
    h2_out[...] = h2.astype(BF16)
    logits = _dot_hi(rw_ref[...], h2, 1, 1)
    m = jnp.max(logits, axis=0, keepdims=True)
    ex = jnp.exp(logits - m)
    aff_out[...] = ex / jnp.sum(ex, axis=0, keepdims=True)


def _out_even_kernel(x_ref, mod_ref, ha0_ref, ha1_ref, ga_ref, y0_ref, y1_ref, g_ref, bonus_ref, lnw_ref, lnb_ref,
                     e_ref, wo_ref, g2n_ref, rw_ref, x_out, h2_out, aff_out, *, da, hd):
    mod = mod_ref[...]
    ya = (ha0_ref[...] + ha1_ref[...]) * _gelu_tanh(ga_ref[...])
    y = y0_ref[...] + y1_ref[...]
    e = e_ref[...]
    mean = _seg_sum(y, e) * (1.0 / hd)
    yc = y - mean
    var = _seg_sum(yc * yc, e) * (1.0 / hd)
    yn = yc * lax.rsqrt(var + GN_EPS_B) * lnw_ref[...] + lnb_ref[...]
    yb = (yn + bonus_ref[...]) * g_ref[...]
    out = _dot_bf(ya, wo_ref[0:da, :]) + _dot_bf(yb, wo_ref[da:, :])
    _router_tail(x_ref[...] + mod[2:3] * out, mod, g2n_ref, rw_ref, x_out, h2_out, aff_out)


def _out_odd_kernel(x_ref, mod_ref, o0_ref, o1_ref, g_ref, gw_ref, wo_ref, g2n_ref, rw_ref,
                    x_out, h2_out, aff_out, *scratch, nh, dv, grid_rows):
    mod = mod_ref[...]
    gw = gw_ref[...]
    if grid_rows:
        hv = nh * dv
        o_scr = scratch[0]
        for c in range(GRID_W):
            for j in range(hv // LANES):
                lanes = slice(c * hv + j * LANES, c * hv + (j + 1) * LANES)
                o_scr[j, pl.ds(c, grid_rows, stride=GRID_W), :] = o0_ref[:, lanes] + o1_ref[:, lanes]
        o_all = jnp.concatenate([o_scr[j] for j in range(hv // LANES)], axis=1)
    else:
        o_all = o0_ref[...] + o1_ref[...]
    parts = []
    for h in range(nh):
        o = o_all[:, h * dv:(h + 1) * dv]
        parts.append(o * lax.rsqrt(jnp.mean(o * o, axis=-1, keepdims=True) + EPS) * gw)
    y = jnp.concatenate(parts, axis=1) * _silu(g_ref[...])
    out = _dot_bf(y, wo_ref[...])
    _router_tail(x_ref[...] + mod[2:3] * out, mod, g2n_ref, rw_ref, x_out, h2_out, aff_out)


def _out_common(kern, x, modr, mod_idx, extra_args, extra_specs, norm2_g, router_wt, bsz, seq, name,
                tb=TB, scratch=()):
    n, d = x.shape
    ne = router_wt.shape[0]
    nt = seq // tb
    row = lambda b, t: (b * nt + t, 0)
    in_specs = ([pl.BlockSpec((tb, d), row), pl.BlockSpec((None, 6, d), lambda b, t: (mod_idx(b), 0, 0))]
                + extra_specs
                + [pl.BlockSpec((1, d), lambda b, t: (0, 0)), pl.BlockSpec((ne, d), lambda b, t: (0, 0))])
    return pl.pallas_call(
        kern, grid=(bsz, nt), in_specs=in_specs,
        out_specs=[pl.BlockSpec((tb, d), row), pl.BlockSpec((tb, d), row),
                   pl.BlockSpec((ne, tb), lambda b, t: (0, b * nt + t))],
        out_shape=[jax.ShapeDtypeStruct((n, d), F32), jax.ShapeDtypeStruct((n, d), BF16),
                   jax.ShapeDtypeStruct((ne, n), F32)],
        scratch_shapes=list(scratch),
        compiler_params=_cparams(("arbitrary", "arbitrary"), VMEM_BIG),
        name=name,
    )(x, modr, *extra_args, norm2_g.reshape(1, d), router_wt)


def _select_kernel(aff_ref, slot_ref, cnt_ref, *, cap, ntb, tokb):
    aff = aff_ref[...]
    ne = aff.shape[0]
    big = jnp.float32(4.0)

    def body(carry):
        lo, hi, _, _ = carry
        inr = (aff > lo) & (aff <= hi)
        mn = jnp.min(jnp.where(inr, aff, big), axis=1, keepdims=True)
        mx = jnp.max(jnp.where(inr, aff, -big), axis=1, keepdims=True)
        done = mn >= mx
        piv = mn + (mx - mn) * 0.5
        piv = jnp.where(piv >= mx, mn, piv)
        ge = jnp.sum((aff > piv).astype(jnp.int32), axis=1, keepdims=True) >= cap
        lo = jnp.where(done | ~ge, lo, piv)
        hi = jnp.where(done | ge, hi, piv)
        return lo, hi, mx, jnp.sum((~done).astype(jnp.int32))

    init = (jnp.full((ne, 1), -1.0, F32), jnp.full((ne, 1), big, F32), jnp.zeros((ne, 1), F32), jnp.int32(1))
    _, _, thr, _ = lax.while_loop(lambda carry: carry[3] > 0, body, init)
    gt = aff > thr
    eq = aff == thr
    need = cap - jnp.sum(gt.astype(jnp.int32), axis=1, keepdims=True)
    si = lax.broadcasted_iota(jnp.int32, (tokb, tokb), 0)
    ti = lax.broadcasted_iota(jnp.int32, (tokb, tokb), 1)
    ustrict = (si < ti).astype(BF16)

    def excl_count(mask_blk, carry):
        m = mask_blk.astype(BF16)
        pre = _dot(m, ustrict).astype(jnp.int32) + carry
        return pre, carry + jnp.sum(mask_blk.astype(jnp.int32), axis=1, keepdims=True)

    ceq = jnp.zeros((ne, 1), jnp.int32)
    csel = jnp.zeros((ne, 1), jnp.int32)
    lane = lax.broadcasted_iota(jnp.int32, (ne, 128), 1)
    cnt_acc = jnp.zeros((ne, 128), jnp.int32)
    for j in range(ntb):
        sl = slice(j * tokb, (j + 1) * tokb)
        eq_rank, ceq = excl_count(eq[:, sl], ceq)
        sel = gt[:, sl] | (eq[:, sl] & (eq_rank < need))
        cnt_acc = jnp.where(lane == j, csel, cnt_acc)
        rank, csel = excl_count(sel, csel)
        slot_ref[:, sl] = jnp.where(sel, rank, -1)
    cnt_acc = jnp.where(lane == ntb, csel, cnt_acc)
    cnt_ref[...] = cnt_acc


def _select(aff_t, cap):
    ne, n = aff_t.shape
    ntb = n // TOK_B
    assert ntb < 128
    return pl.pallas_call(
        functools.partial(_select_kernel, cap=cap, ntb=ntb, tokb=TOK_B),
        grid=(1,),
        in_specs=[pl.BlockSpec((ne, n), lambda i: (0, 0))],
        out_specs=[pl.BlockSpec((ne, n), lambda i: (0, 0)), pl.BlockSpec((ne, 128), lambda i: (0, 0))],
        out_shape=[jax.ShapeDtypeStruct((ne, n), jnp.int32), jax.ShapeDtypeStruct((ne, 128), jnp.int32)],
        compiler_params=_cparams(("arbitrary",)),
        name="select",
    )(aff_t)


def _gather_kernel(lo_ref, hi_ref, slot_ref, h2_ref, xe_ref, acc_ref, *, nsb, slotb, tokb):
    e = pl.program_id(0)
    sb = pl.program_id(1)
    base = sb * slotb
    sidx = lax.broadcasted_iota(jnp.int32, (slotb, tokb), 0) + base
    acc_ref[...] = jnp.zeros_like(acc_ref)

    lo = lo_ref[e * nsb + sb]
    hi = hi_ref[e * nsb + sb]

    def picked(tb, live):
        srow = jnp.where(live, slot_ref[pl.ds(tb, 1), :], -1)
        onehot = jnp.where(srow == sidx, 1.0, 0.0).astype(BF16)
        off = pl.multiple_of(tb * tokb, tokb)
        return _dot(onehot, h2_ref[pl.ds(off, tokb), :])

    def body(i, carry):
        tb0 = lo + 2 * i
        tb1 = jnp.minimum(tb0 + 1, hi)
        acc_ref[...] += picked(tb0, True) + picked(tb1, tb0 + 1 <= hi)
        return carry

    lax.fori_loop(0, (hi - lo + 2) // 2, body, 0)
    xe_ref[...] = acc_ref[...].astype(BF16)


def _gather(lo, hi, slot3, h2, cap):
    ne, ntb, tokb = slot3.shape
    n, d = h2.shape
    nsb = cap // SLOT_B
    return pl.pallas_call(
        functools.partial(_gather_kernel, nsb=nsb, slotb=SLOT_B, tokb=tokb),
        grid_spec=pltpu.PrefetchScalarGridSpec(
            num_scalar_prefetch=2, grid=(ne, nsb),
            in_specs=[pl.BlockSpec((None, ntb, tokb), lambda e, s, lo, hi: (e, 0, 0)),
                      pl.BlockSpec((n, d), lambda e, s, lo, hi: (0, 0), pipeline_mode=pl.Buffered(1))],
            out_specs=pl.BlockSpec((None, SLOT_B, d), lambda e, s, lo, hi: (e, s, 0)),
            scratch_shapes=[pltpu.VMEM((SLOT_B, d), F32)]),
        out_shape=jax.ShapeDtypeStruct((ne, cap, d), BF16),
        compiler_params=_cparams(("arbitrary", "arbitrary"), VMEM_BIG),
        name="moe_gather",
    )(lo, hi, slot3, h2)


def _ffn_kernel(xe_ref, wg_ref, wu_ref, wd_ref, ye_ref, acc_ref):
    f = pl.program_id(1)
    xe = xe_ref[...]
    hg = _dot(xe, wg_ref[...].astype(BF16))
    hu = _dot(xe, wu_ref[...].astype(BF16))
    hid = (_silu(hg) * hu).astype(BF16)
    part = _dot(hid, wd_ref[...].astype(BF16))

    @pl.when(f == 0)
    def _():
        acc_ref[...] = part

    @pl.when(f > 0)
    def _():
        acc_ref[...] += part

    @pl.when(f == pl.num_programs(1) - 1)
    def _():
        ye_ref[...] = acc_ref[...].astype(BF16)


def _ffn(xe, w_gate, w_up, w_down, l):
    ne, cap, d = xe.shape
    dff = w_gate.shape[-1]
    nf = dff // FF_B
    return pl.pallas_call(
        _ffn_kernel,
        grid=(ne, nf),
        in_specs=[pl.BlockSpec((None, cap, d), lambda e, f: (e, 0, 0)),
                  pl.BlockSpec((None, None, d, FF_B), lambda e, f: (l, e, 0, f)),
                  pl.BlockSpec((None, None, d, FF_B), lambda e, f: (l, e, 0, f)),
                  pl.BlockSpec((None, None, FF_B, d), lambda e, f: (l, e, f, 0))],
        out_specs=pl.BlockSpec((None, cap, d), lambda e, f: (e, 0, 0)),
        out_shape=jax.ShapeDtypeStruct((ne, cap, d), BF16),
        scratch_shapes=[pltpu.VMEM((cap, d), F32)],
        compiler_params=_cparams(("arbitrary", "arbitrary"), VMEM_BIG),
        name="moe_ffn",
    )(xe, w_gate, w_up, w_down)


def _scatter_kernel(lo_ref, hi_ref, slot_ref, gate_ref, ye_ref, out_ref, *, nsb, slotb, tokb):
    e = pl.program_id(0)
    sb = pl.program_id(1)

    @pl.when((e == 0) & (sb == 0))
    def _():
        out_ref[...] = jnp.zeros_like(out_ref)

    base = sb * slotb
    sidx = lax.broadcasted_iota(jnp.int32, (slotb, tokb), 0) + base
    ye = ye_ref[...]

    lo = lo_ref[e * nsb + sb]
    hi = hi_ref[e * nsb + sb]

    def spread(tb, live):
        srow = jnp.where(live, slot_ref[pl.ds(tb, 1), :], -1)
        grow = gate_ref[pl.ds(tb, 1), :]
        gm = jnp.where(srow == sidx, grow, 0.0).astype(BF16)
        return _dot(gm, ye, 0, 0)

    def body(i, carry):
        tb0 = lo + 2 * i
        tb1 = jnp.minimum(tb0 + 1, hi)
        add0 = spread(tb0, True)
        add1 = spread(tb1, tb0 + 1 <= hi)
        out_ref[pl.ds(pl.multiple_of(tb0 * tokb, tokb), tokb), :] += add0
        out_ref[pl.ds(pl.multiple_of(tb1 * tokb, tokb), tokb), :] += add1
        return carry

    lax.fori_loop(0, (hi - lo + 2) // 2, body, 0)


def _scatter(lo, hi, slot3, gate3, ye, n):
    ne, cap, d = ye.shape
    _, ntb, tokb = slot3.shape
    nsb = cap // SLOT_B
    return pl.pallas_call(
        functools.partial(_scatter_kernel, nsb=nsb, slotb=SLOT_B, tokb=tokb),
        grid_spec=pltpu.PrefetchScalarGridSpec(
            num_scalar_prefetch=2, grid=(ne, nsb),
            in_specs=[pl.BlockSpec((None, ntb, tokb), lambda e, s, lo, hi: (e, 0, 0)),
                      pl.BlockSpec((None, ntb, tokb), lambda e, s, lo, hi: (e, 0, 0)),
                      pl.BlockSpec((None, SLOT_B, d), lambda e, s, lo, hi: (e, s, 0))],
            out_specs=pl.BlockSpec((n, d), lambda e, s, lo, hi: (0, 0), pipeline_mode=pl.Buffered(1))),
        out_shape=jax.ShapeDtypeStruct((n, d), F32),
        compiler_params=_cparams(("arbitrary", "arbitrary"), VMEM_BIG),
        name="moe_scatter",
    )(lo, hi, slot3, gate3, ye)


def _moe(h2, aff_t, w_gate, w_up, w_down, l):
    n, d = h2.shape
    ne = aff_t.shape[0]
    cap = max(1, CAP_FACTOR * n // ne)
    assert cap % SLOT_B == 0 and n % TOK_B == 0
    ntb = n // TOK_B
    nsb = cap // SLOT_B
    slot, cnt = _select(aff_t, cap)
    cnt = cnt[:, :ntb + 1]
    starts = (jnp.arange(nsb, dtype=jnp.int32) * SLOT_B)[None, :, None]
    lo = jnp.clip(jnp.sum(cnt[:, None, 1:] <= starts, -1), 0, ntb - 1).astype(jnp.int32).reshape(-1)
    hi = jnp.clip(jnp.sum(cnt[:, None, 1:] < starts + SLOT_B, -1), 0, ntb - 1).astype(jnp.int32).reshape(-1)
    slot3 = slot.reshape(ne, ntb, TOK_B)
    gate3 = aff_t.reshape(ne, ntb, TOK_B)
    xe = _gather(lo, hi, slot3, h2, cap)
    ye = _ffn(xe, w_gate, w_up, w_down, l)
    return _scatter(lo, hi, slot3, gate3, ye, n)


def _final_kernel(x_ref, moe_ref, mod_ref, g_ref, o_ref):
    x = x_ref[...] + mod_ref[...][5:6] * moe_ref[...]
    o_ref[...] = _rms(x, g_ref[...])


def _final(x, moe, modr, mod_idx, gain, bsz, seq):
    n, d = x.shape
    nt = seq // TB
    row = lambda b, t: (b * nt + t, 0)
    return pl.pallas_call(
        _final_kernel, grid=(bsz, nt),
        in_specs=[pl.BlockSpec((TB, d), row), pl.BlockSpec((TB, d), row),
                  pl.BlockSpec((None, 6, d), lambda b, t: (mod_idx(b), 0, 0)),
                  pl.BlockSpec((1, d), lambda b, t: (0, 0))],
        out_specs=pl.BlockSpec((TB, d), row),
        out_shape=jax.ShapeDtypeStruct((n, d), F32),
        compiler_params=_cparams(("arbitrary", "arbitrary")),
        name="final_norm",
    )(x, moe, modr, gain.reshape(1, d))


def _block_diag(w):
    h, n, _ = w.shape
    eye = jnp.eye(h, dtype=w.dtype)
    return (eye[:, None, :, None] * w[:, :, None, :]).reshape(h * n, h * n)


def _lora_cat(w2):
    _, l, c = w2.shape
    z = jnp.zeros((l, c), w2.dtype)
    return jnp.concatenate([jnp.concatenate([w2[0], z], 1), jnp.concatenate([z, w2[1]], 1)], 0)


def _trunk(x, bsz, seq, cond_base, per_batch, st_rglru, st_rwkv, st_gla, grid_mode, modr, P):
    d_model = x.shape[1]
    depth = P['norm1_g'].shape[0]
    new_rglru, new_rwkv, new_gla = [], [], []
    moe_prev = None
    for l in range(depth):
        mod_idx = (lambda b, l=l: l * 8 + cond_base + b) if per_batch else (lambda b, l=l: l * 8 + cond_base)
        mod_prev = (lambda b, l=l: (l - 1) * 8 + cond_base + b) if per_batch else (lambda b, l=l: (l - 1) * 8 + cond_base)
        router_wt = P['router_w'][l].T
        if l % 2 == 0:
            e = l // 2
            da = P['a_conv_w'].shape[-1]
            db = P['b_k_k'].shape[-1]
            w_in = P['ev_w_in'][e].astype(BF16)
            nb_cols = w_in.shape[1] - 2 * da
            res = _inproj(x, moe_prev, mod_prev, modr, mod_idx, P['norm1_g'][l], w_in, (2 * da, nb_cols), bsz, seq)
            if moe_prev is not None:
                x, pa, pb = res
            else:
                pa, pb = res
            ha, sa = [], []
            for d in range(2):
                wg = jnp.concatenate([_block_diag(P['a_w_rg'][e, d]), _block_diag(P['a_w_ig'][e, d])], 1).astype(BF16)
                bg = jnp.concatenate([P['a_b_rg'][e, d], P['a_b_ig'][e, d]])
                h_d, hf = _rglru(pa, P['a_conv_w'][e], P['a_conv_b'][e], wg, bg, P['a_lam'][e, d],
                                 st_rglru[:, e, d], d, bsz, seq)
                ha.append(h_d)
                sa.append(hf.reshape(bsz, da))
            new_rglru.append(jnp.stack(sa, 1))
            e_bf = _block_diag(jnp.ones((db // HEAD_B, HEAD_B, HEAD_B), BF16))
            r, v, kkn, g, bonus, lw, bt, kd = _rwkv_prep(
                pb, P['b_mu'][e], P['b_w0'][e].reshape(1, 2 * db), _lora_cat(P['b_w2'][e]),
                P['b_a0'][e].reshape(1, 2 * db), _lora_cat(P['b_a2'][e]), P['b_g2'][e].astype(BF16),
                P['b_k_k'][e], P['b_k_a'][e], P['b_r_k'][e], e_bf, bsz, seq)
            y0, y1, sfin = _rwkv_scan(r, kkn, v, lw, bt, kd, st_rwkv, e, bsz, seq)
            ys = [y0, y1]
            new_rwkv.append(sfin)
            nt = seq // TB
            rowh = pl.BlockSpec((TB, da), lambda b, t: (b * nt + t, 0))
            rowg = pl.BlockSpec((TB, da), lambda b, t: (b * nt + t, 1))
            vec = pl.BlockSpec((1, db), lambda b, t: (0, 0))
            x, h2, aff_t = _out_common(
                functools.partial(_out_even_kernel, da=da, hd=HEAD_B), x, modr, mod_idx,
                [ha[0], ha[1], pa, ys[0], ys[1], g, bonus, P['b_ln_w'][e].reshape(1, db), P['b_ln_b'][e].reshape(1, db),
                 e_bf, P['ev_w_out'][e].astype(BF16)],
                [rowh, rowh, rowg, rowh, rowh, rowh, rowh, vec, vec,
                 pl.BlockSpec((db, db), lambda b, t: (0, 0)),
                 pl.BlockSpec((da + db, d_model), lambda b, t: (0, 0))],
                P['norm2_g'][l], router_wt, bsz, seq, "out_even")
        else:
            o_ = l // 2
            w_in = P['od_w_in'][o_]
            nh, dk, dv = st_gla.shape[3:]
            hv = nh * dv
            qkv = 2 * nh * dk + hv
            gin = -(2 * GATE_RANK) % 128 + 2 * GATE_RANK
            w_in = jnp.concatenate([w_in[:, :qkv], w_in[:, qkv + hv:],
                                    jnp.zeros((d_model, gin - 2 * GATE_RANK), w_in.dtype),
                                    w_in[:, qkv:qkv + hv]], axis=1).astype(BF16)
            res = _inproj(x, moe_prev, mod_prev, modr, mod_idx, P['norm1_g'][l], w_in, (qkv + gin, hv), bsz, seq,
                          grid_view=grid_mode)
            if moe_prev is not None:
                x, p, g = res
            else:
                p, g = res
            wup = jnp.stack([jnp.zeros((gin, nh * dk), F32).at[d * GATE_RANK:(d + 1) * GATE_RANK]
                             .set(P['c_w_gk_up'][o_, d]) for d in range(2)])
            o0, o1, sfin = _gla(p, st_gla, o_, wup, P['c_b_gk'][o_].reshape(2, 1, nh * dk), bsz, seq, grid_mode)
            new_gla.append(sfin)
            tb = SUBLANES * GRID_W if grid_mode else TB
            nt = seq // tb
            if grid_mode:
                rowo = pl.BlockSpec((None, SUBLANES, GRID_W * hv), lambda b, t: (b, t, 0))
                scratch = [pltpu.VMEM((hv // LANES, tb, LANES), F32)]
            else:
                rowo = pl.BlockSpec((tb, hv), lambda b, t: (b * nt + t, 0))
                scratch = []
            rowg = pl.BlockSpec((tb, hv), lambda b, t: (b * nt + t, 0))
            x, h2, aff_t = _out_common(
                functools.partial(_out_odd_kernel, nh=nh, dv=dv, grid_rows=SUBLANES if grid_mode else 0),
                x, modr, mod_idx,
                [o0, o1, g, P['c_gnorm_w'][o_].reshape(1, dv), P['od_w_out'][o_].astype(BF16)],
                [rowo, rowo, rowg, pl.BlockSpec((1, dv), lambda b, t: (0, 0)),
                 pl.BlockSpec((hv, d_model), lambda b, t: (0, 0))],
                P['norm2_g'][l], router_wt, bsz, seq, "out_odd", tb=tb, scratch=scratch)
        moe_prev = _moe(h2, aff_t, P['moe_w_gate'], P['moe_w_up'], P['moe_w_down'], l)
    mod_last = (lambda b: (depth - 1) * 8 + cond_base + b) if per_batch else (lambda b: (depth - 1) * 8 + cond_base)
    y = _final(x, moe_prev, modr, mod_last, P['final_norm_g'], bsz, seq)
    return y, jnp.stack(new_rglru, 1), jnp.stack(new_rwkv, 1), jnp.stack(new_gla, 1)


def kernel(x_prompt, x_sample, state_rglru, state_rwkv, state_gla, c, c_ctx, norm1_g, norm2_g, ada_w, ada_b, router_w, moe_w_gate, moe_w_up, moe_w_down, ev_w_in, ev_w_out, a_conv_w, a_conv_b, a_w_rg, a_b_rg, a_w_ig, a_b_ig, a_lam, b_mu, b_w0, b_w2, b_a0, b_a2, b_g2, b_k_k, b_k_a, b_r_k, b_ln_w, b_ln_b, od_w_in, od_w_out, c_w_gk_up, c_b_gk, c_gnorm_w, final_norm_g):
    P = dict(norm1_g=norm1_g, norm2_g=norm2_g, router_w=router_w,
             moe_w_gate=moe_w_gate, moe_w_up=moe_w_up, moe_w_down=moe_w_down,
             ev_w_in=ev_w_in, ev_w_out=ev_w_out, a_conv_w=a_conv_w, a_conv_b=a_conv_b,
             a_w_rg=a_w_rg, a_b_rg=a_b_rg, a_w_ig=a_w_ig, a_b_ig=a_b_ig, a_lam=a_lam,
             b_mu=b_mu, b_w0=b_w0, b_w2=b_w2, b_a0=b_a0, b_a2=b_a2, b_g2=b_g2, b_k_k=b_k_k,
             b_k_a=b_k_a, b_r_k=b_r_k, b_ln_w=b_ln_w, b_ln_b=b_ln_b,
             od_w_in=od_w_in, od_w_out=od_w_out, c_w_gk_up=c_w_gk_up, c_b_gk=c_b_gk,
             c_gnorm_w=c_gnorm_w, final_norm_g=final_norm_g)
    bp, tp, d_model = x_prompt.shape
    bs, ts, _ = x_sample.shape
    assert bs + 1 <= 8 and tp % TB == 0 and ts % TB == 0
    depth = ada_w.shape[0]
    cv8 = jnp.zeros((8, d_model), F32).at[0].set(c_ctx).at[1:1 + bs].set(c)
    modr = _modulation(cv8, ada_w, ada_b).reshape(depth * 8, 6, d_model)
    dt = x_prompt.dtype
    z_rglru = jnp.zeros((bp,) + state_rglru.shape[1:], dt)
    z_rwkv = jnp.zeros((bp,) + state_rwkv.shape[1:], dt)
    z_gla = jnp.zeros((bp,) + state_gla.shape[1:], dt)
    y_prompt, n_rglru, n_rwkv, n_gla = _trunk(x_prompt.reshape(bp * tp, d_model), bp, tp, 0, False,
                                              z_rglru, z_rwkv, z_gla, False, modr, P)
    y_sample, _, _, _ = _trunk(x_sample.reshape(bs * ts, d_model), bs, ts, 1, True,
                               state_rglru, state_rwkv, state_gla, True, modr, P)
    return (y_prompt.reshape(bp, tp, d_model), y_sample.reshape(bs, ts, d_model), n_rglru, n_rwkv, n_gla)
```

```python
import functools
import math

import numpy as np
import jax
import jax.numpy as jnp
from jax import lax
from jax.experimental import pallas as pl
from jax.experimental.pallas import tpu as pltpu

F32 = jnp.float32
BF16 = jnp.bfloat16
HI = lax.Precision.HIGHEST

GRID_W = 64
EPS = 1e-6
C_RGLRU = 8.0
GN_EPS_B = 64e-5
GATE_NORM = 16.0
CAP_FACTOR = 2
HEAD_B = 64
H_A = 8
H_C = 4
GATE_RANK = 16

SUBLANES = 8
LANES = 128
MXU_N = 256
TB = 256
CH = 64
SUB = 16
SLOT_B = 256
TOK_B = 512
FF_B = 512
VMEM_BIG = 56 * 1024 * 1024


def _dot(a, b, ca=1, cb=0, prec=None):
    return lax.dot_general(a, b, (((ca,), (cb,)), ((), ())), precision=prec,
                           preferred_element_type=F32)


def _dot_bf(a, b, ca=1, cb=0):
    return _dot(a.astype(BF16), b.astype(BF16), ca, cb)


def _dot_hi(a, b, ca=1, cb=0):
    return _dot(a.astype(F32), b.astype(F32), ca, cb, HI)


def _seg_sum(x, e_bf):
    hi = x.astype(BF16)
    lo = (x - hi.astype(F32)).astype(BF16)
    return _dot(hi, e_bf) + _dot(lo, e_bf)


def _mask_dot(mask_bf, x, cm=1, cx=0):
    hi = x.astype(BF16)
    r1 = x - hi.astype(F32)
    mid = r1.astype(BF16)
    lo = (r1 - mid.astype(F32)).astype(BF16)
    return _dot(mask_bf, hi, cm, cx) + _dot(mask_bf, mid, cm, cx) + _dot(mask_bf, lo, cm, cx)


def _sigmoid(x):
    return 1.0 / (1.0 + jnp.exp(-x))


def _softplus(x):
    return jnp.maximum(x, 0.0) + jnp.log1p(jnp.exp(-jnp.abs(x)))


def _silu(x):
    return x * _sigmoid(x)


def _gelu_tanh(x):
    return 0.5 * x * (1.0 + jnp.tanh(math.sqrt(2.0 / math.pi) * (x + 0.044715 * (x * x * x))))


def _rms(x, g):
    return x * lax.rsqrt(jnp.mean(x * x, axis=-1, keepdims=True) + EPS) * g


def _cparams(sem, vmem=None):
    return pltpu.CompilerParams(dimension_semantics=sem, vmem_limit_bytes=vmem)


def _mod_kernel(c_ref, w_ref, b_ref, o_ref):
    o_ref[...] = _dot_hi(_silu(c_ref[...]), w_ref[...]) + b_ref[...]


def _modulation(cv8, ada_w, ada_b):
    depth, d, n6 = ada_w.shape
    nb = 1536
    return pl.pallas_call(
        _mod_kernel,
        grid=(depth, n6 // nb),
        in_specs=[pl.BlockSpec((8, d), lambda l, j: (0, 0)),
                  pl.BlockSpec((None, d, nb), lambda l, j: (l, 0, j)),
                  pl.BlockSpec((None, 1, nb), lambda l, j: (l, 0, j))],
        out_specs=pl.BlockSpec((None, 8, nb), lambda l, j: (l, 0, j)),
        out_shape=jax.ShapeDtypeStruct((depth, 8, n6), F32),
        compiler_params=_cparams(("arbitrary", "arbitrary")),
        name="modulation",
    )(cv8, ada_w, ada_b.reshape(depth, 1, n6))


def _project(x, mod, g_ref, w_ref, outs, splits, grid_rows):
    h = _rms(x, g_ref[...]) * (1.0 + mod[1:2]) + mod[0:1]
    p = _dot_bf(h, w_ref[...])
    lo = 0
    if grid_rows:
        n0 = splits[0]
        scr = outs[-1]
        for j in range(n0 // LANES):
            scr[j] = p[:, j * LANES:(j + 1) * LANES]
        for c in range(GRID_W):
            for j in range(n0 // LANES):
                outs[0][:, c * n0 + j * LANES:c * n0 + (j + 1) * LANES] = scr[j, pl.ds(c, grid_rows, stride=GRID_W), :]
        lo, outs, splits = n0, outs[1:-1], splits[1:]
    for o_ref, n in zip(outs, splits):
        o_ref[...] = p[:, lo:lo + n]
        lo += n


def _inproj_kernel(x_ref, mod_ref, g_ref, w_ref, *outs, splits, grid_rows):
    _project(x_ref[...], mod_ref[...], g_ref, w_ref, outs, splits, grid_rows)


def _inproj_res_kernel(x_ref, moe_ref, modp_ref, mod_ref, g_ref, w_ref, xo_ref, *outs, splits, grid_rows):
    x = x_ref[...] + modp_ref[...][5:6] * moe_ref[...]
    xo_ref[...] = x
    _project(x, mod_ref[...], g_ref, w_ref, outs, splits, grid_rows)


def _inproj(x, moe_prev, mod_prev_idx, modr, mod_idx, gain, w_bf, splits, bsz, seq, grid_view=False):
    n, d = x.shape
    tb = SUBLANES * GRID_W if grid_view else TB
    nt = seq // tb
    ncols = w_bf.shape[1]
    with_res = moe_prev is not None
    row = lambda b, t: (b * nt + t, 0)
    in_specs = [pl.BlockSpec((tb, d), row)]
    args = [x]
    if with_res:
        in_specs += [pl.BlockSpec((tb, d), row),
                     pl.BlockSpec((None, 6, d), lambda b, t: (mod_prev_idx(b), 0, 0))]
        args += [moe_prev, modr]
    in_specs += [pl.BlockSpec((None, 6, d), lambda b, t: (mod_idx(b), 0, 0))]
    args += [modr]
    in_specs += [pl.BlockSpec((1, d), lambda b, t: (0, 0)),
                 pl.BlockSpec((d, ncols), lambda b, t: (0, 0), pipeline_mode=pl.Buffered(1))]
    args += [gain.reshape(1, d), w_bf]
    out_shape = [jax.ShapeDtypeStruct((n, c), F32) for c in splits]
    out_specs = [pl.BlockSpec((tb, c), row) for c in splits]
    scratch = []
    grid_rows = 0
    if grid_view:
        grid_rows = tb // GRID_W
        out_shape[0] = jax.ShapeDtypeStruct((bsz, seq // GRID_W, GRID_W * splits[0]), F32)
        out_specs[0] = pl.BlockSpec((None, grid_rows, GRID_W * splits[0]), lambda b, t: (b, t, 0))
        scratch = [pltpu.VMEM((splits[0] // LANES, tb, LANES), F32)]
    if with_res:
        out_shape = [jax.ShapeDtypeStruct((n, d), F32)] + out_shape
        out_specs = [pl.BlockSpec((tb, d), row)] + out_specs
        kern = functools.partial(_inproj_res_kernel, splits=splits, grid_rows=grid_rows)
    else:
        kern = functools.partial(_inproj_kernel, splits=splits, grid_rows=grid_rows)
    return pl.pallas_call(
        kern, grid=(bsz, nt), in_specs=in_specs, out_specs=out_specs, out_shape=out_shape,
        scratch_shapes=scratch,
        compiler_params=_cparams(("arbitrary", "arbitrary"), VMEM_BIG),
        name="inproj",
    )(*args)


def _fill_ext(ext_ref, x, prev_ref, next_ref, first, last, tb):
    ext_ref[0:8, :] = jnp.where(first, 0.0, prev_ref[...])
    ext_ref[8:8 + tb, :] = x
    ext_ref[8 + tb:16 + tb, :] = jnp.where(last, 0.0, next_ref[...])


def _rglru_kernel(xa_ref, prev_ref, next_ref, cw_ref, cb_ref, wg_ref, bg_ref, lam_ref, h0_ref,
                  h_ref, hfin_ref, ext_ref, carry_ref, *, d, nt, tb):
    i = pl.program_id(1)
    t = i if d == 0 else nt - 1 - i
    first = t == 0
    last = t == nt - 1
    x = xa_ref[...]
    da = x.shape[1]
    _fill_ext(ext_ref, x, prev_ref, next_ref, first, last, tb)
    cw = cw_ref[...]
    u = (cw[0:1] * ext_ref[7:7 + tb, :] + cw[1:2] * x + cw[2:3] * ext_ref[9:9 + tb, :]
         + cw[3:4] * ext_ref[10:10 + tb, :] + cb_ref[...])
    gts = _dot_bf(u, wg_ref[...]) + bg_ref[...]
    gate_r = _sigmoid(gts[:, :da])
    gate_i = _sigmoid(gts[:, da:])
    log_a = -C_RGLRU * gate_r * _softplus(-lam_ref[...])
    a = jnp.exp(log_a)
    xin = jnp.sqrt(1.0 - jnp.exp(2.0 * log_a)) * gate_i * u

    @pl.when(i == 0)
    def _():
        carry_ref[...] = h0_ref[...]

    rows = lax.broadcasted_iota(jnp.int32, (tb, da), 0)
    s = 1
    while s < tb:
        if d == 0:
            a_sh = pltpu.roll(a, s, 0)
            x_sh = pltpu.roll(xin, s, 0)
            valid = rows >= s
        else:
            a_sh = pltpu.roll(a, tb - s, 0)
            x_sh = pltpu.roll(xin, tb - s, 0)
            valid = rows < tb - s
        xin = jnp.where(valid, a * x_sh + xin, xin)
        a = jnp.where(valid, a * a_sh, a)
        s *= 2
    h = xin + a * carry_ref[...]
    h_ref[...] = h
    newc = h[tb - 1:tb, :] if d == 0 else h[0:1, :]
    carry_ref[...] = newc
    hfin_ref[...] = newc


def _rglru(pa, conv_w, conv_b, wg_d, bg_d, lam_d, h0_d, d, bsz, seq):
    n = pa.shape[0]
    da = conv_w.shape[1]
    nt = seq // TB
    r8 = TB // 8
    tt = (lambda t: t) if d == 0 else (lambda t: nt - 1 - t)
    blk = lambda b, t: b * nt + tt(t)
    return pl.pallas_call(
        functools.partial(_rglru_kernel, d=d, nt=nt, tb=TB),
        grid=(bsz, nt),
        in_specs=[pl.BlockSpec((TB, da), lambda b, t: (blk(b, t), 0)),
                  pl.BlockSpec((8, da), lambda b, t: (jnp.maximum(blk(b, t) * r8 - 1, 0), 0)),
                  pl.BlockSpec((8, da), lambda b, t: (jnp.minimum((blk(b, t) + 1) * r8, n // 8 - 1), 0)),
                  pl.BlockSpec((4, da), lambda b, t: (0, 0)),
                  pl.BlockSpec((1, da), lambda b, t: (0, 0)),
                  pl.BlockSpec((da, 2 * da), lambda b, t: (0, 0)),
                  pl.BlockSpec((1, 2 * da), lambda b, t: (0, 0)),
                  pl.BlockSpec((1, da), lambda b, t: (0, 0)),
                  pl.BlockSpec((None, 1, da), lambda b, t: (b, 0, 0))],
        out_specs=[pl.BlockSpec((TB, da), lambda b, t: (blk(b, t), 0)),
                   pl.BlockSpec((None, 1, da), lambda b, t: (b, 0, 0))],
        out_shape=[jax.ShapeDtypeStruct((n, da), F32), jax.ShapeDtypeStruct((bsz, 1, da), F32)],
        scratch_shapes=[pltpu.VMEM((TB + 16, da), F32), pltpu.VMEM((1, da), F32)],
        compiler_params=_cparams(("arbitrary", "arbitrary")),
        name=f"rglru_d{d}",
    )(pa, pa, pa, conv_w, conv_b.reshape(1, da), wg_d, bg_d.reshape(1, 2 * da), lam_d.reshape(1, da),
      h0_d.reshape(bsz, 1, da))


def _rwkv_prep_kernel(pb_ref, prev_ref, next_ref, mu_ref, w0_ref, w2_ref, a0_ref, a2_ref, g2_ref,
                      kk_ref, ka_ref, rk_ref, e_ref,
                      r_out, v_out, kk_out, g_out, bonus_out, lw_out, bt_out, kd_out, ext_ref, *, nt, tb, db, lw_, la_):
    t = pl.program_id(1)
    x = pb_ref[...]
    _fill_ext(ext_ref, x, prev_ref, next_ref, t == 0, t == nt - 1, tb)
    sh = 0.5 * (ext_ref[7:7 + tb, :] + ext_ref[9:9 + tb, :])
    z = x + mu_ref[...] * (sh - x)
    r = z[:, 0:db]
    k = z[:, db:2 * db]
    v = z[:, 2 * db:3 * db]
    o = 3 * db
    zw = z[:, o:o + 2 * lw_]
    za = z[:, o + 2 * lw_:o + 2 * lw_ + 2 * la_]
    zg = z[:, o + 2 * lw_ + 2 * la_:]
    w = -_softplus(-(w0_ref[...] + _dot3(_split(jnp.tanh(zw)), _split(w2_ref[...])))) - 0.5
    lw = -jnp.exp(w)
    a = _sigmoid(a0_ref[...] + _dot3(_split(za), _split(a2_ref[...])))
    g = _dot_bf(_sigmoid(zg), g2_ref[...])
    e = e_ref[...]
    kkv = k * kk_ref[...]
    ss = _seg_sum(kkv * kkv, e)
    kkn = kkv / jnp.maximum(jnp.sqrt(ss), 1e-12)
    ka = ka_ref[...]
    kd0 = k * (1.0 + (a[:, :db] - 1.0) * ka)
    kd1 = k * (1.0 + (a[:, db:] - 1.0) * ka)
    bonus = _seg_sum(r * rk_ref[...] * (kd0 + kd1), e) * v
    r_out[...] = r
    v_out[...] = v
    kk_out[...] = kkn
    g_out[...] = g
    bonus_out[...] = bonus
    lw_out[0] = lw[:, :db]
    lw_out[1] = lw[:, db:]
    bt_out[0] = kkn * a[:, :db]
    bt_out[1] = kkn * a[:, db:]
    kd_out[0] = kd0
    kd_out[1] = kd1


def _rwkv_prep(pb, mu, w0c, w2c, a0c, a2c, g2, k_k, k_a, r_k, e_bf, bsz, seq):
    n, ncol = pb.shape
    db = k_k.shape[-1]
    nt = seq // TB
    r8 = TB // 8
    blk = lambda b, t: b * nt + t
    full = lambda shp: pl.BlockSpec(shp, lambda b, t: (0,) * len(shp))
    row = pl.BlockSpec((TB, db), lambda b, t: (blk(b, t), 0))
    row2 = pl.BlockSpec((2, TB, db), lambda b, t: (0, blk(b, t), 0))
    s1 = jax.ShapeDtypeStruct((n, db), F32)
    s2 = jax.ShapeDtypeStruct((2, n, db), F32)
    lw_ = w2c.shape[0] // 2
    la_ = a2c.shape[0] // 2
    return pl.pallas_call(
        functools.partial(_rwkv_prep_kernel, nt=nt, tb=TB, db=db, lw_=lw_, la_=la_),
        grid=(bsz, nt),
        in_specs=[pl.BlockSpec((TB, ncol), lambda b, t: (blk(b, t), 0)),
                  pl.BlockSpec((8, ncol), lambda b, t: (jnp.maximum(blk(b, t) * r8 - 1, 0), 0)),
                  pl.BlockSpec((8, ncol), lambda b, t: (jnp.minimum((blk(b, t) + 1) * r8, n // 8 - 1), 0)),
                  full((1, ncol)), full((1, 2 * db)), full(w2c.shape), full((1, 2 * db)), full(a2c.shape),
                  full(g2.shape), full((1, db)), full((1, db)), full((1, db)), full((db, db))],
        out_specs=[row, row, row, row, row, row2, row2, row2],
        out_shape=[s1, s1, s1, s1, s1, s2, s2, s2],
        scratch_shapes=[pltpu.VMEM((TB + 16, ncol), F32)],
        compiler_params=_cparams(("arbitrary", "arbitrary"), VMEM_BIG),
        name="rwkv_prep",
    )(pb, pb, pb, mu.reshape(1, ncol), w0c, w2c, a0c, a2c, g2, k_k.reshape(1, db), k_a.reshape(1, db),
      r_k.reshape(1, db), e_bf)


def _tri_masks(c, d):
    ti = lax.broadcasted_iota(jnp.int32, (c, c), 0)
    si = lax.broadcasted_iota(jnp.int32, (c, c), 1)
    if d == 0:
        return si < ti, si <= ti, ti == si
    return si > ti, si >= ti, ti == si


def _split(x):
    hi = x.astype(BF16)
    return hi, (x - hi.astype(F32)).astype(BF16)


def _cut(p, sl):
    return p[0][sl], p[1][sl]


def _dot3(a, b, ca=1, cb=0):
    m = a[0].shape[1 - ca]
    n = b[0].shape[1 - cb]
    if not (ca == 1 and cb == 1 and 2 * n == MXU_N):
        return _dot(a[0], b[0], ca, cb) + _dot(a[1], b[0], ca, cb) + _dot(a[0], b[1], ca, cb)
    q = _dot(jnp.concatenate(a, axis=0), jnp.concatenate(b, axis=0), ca, cb)
    return (q[:m, :n] + q[:m, n:]) + (q[m:, :n] + q[m:, n:])


def _staged(gens):
    results = [None] * len(gens)
    live = list(range(len(gens)))
    while live:
        still = []
        for i in live:
            try:
                out = next(gens[i])
            except StopIteration:
                continue
            if out is not None:
                results[i] = out
            still.append(i)
        live = still
    return results


def _rwkv_chunk(d, c, nh, hd, r, kk, v, lw, bt, kd, s_heads):
    strict, incl, diag = _tri_masks(c, d)
    cs = _mask_dot(incl.astype(BF16), lw)
    tot = cs[c - 1:c] if d == 0 else cs[0:1]
    einv = jnp.exp(-cs)
    eend = jnp.exp(tot - cs)
    gtot = jnp.exp(tot)
    a_t = _split(-kk * jnp.exp(cs - lw))
    r_t = _split(r * jnp.exp(cs))
    b_t = _split(bt * einv)
    k_t = _split(kd * einv)
    b_g = _split(bt * eend)
    k_g = _split(kd * eend)
    vs = _split(v)
    eye = diag.astype(F32)
    t2 = lax.broadcasted_iota(jnp.int32, (c, 2 * c), 0)
    s2 = lax.broadcasted_iota(jnp.int32, (c, 2 * c), 1)
    s2 = jnp.where(s2 >= c, s2 - c, s2)
    incl2 = (s2 <= t2) if d == 0 else (s2 >= t2)
    nsq = int(round(math.log2(SUB))) - 1
    ti = lax.broadcasted_iota(jnp.int32, (c, c), 0)
    si = lax.broadcasted_iota(jnp.int32, (c, c), 1)
    blk_masks = [ti // SUB == si // SUB]
    w = SUB
    while w < c:
        blk_masks.append((ti // (2 * w) == si // (2 * w)) & (ti // w != si // w))
        w *= 2
    def head(h):
        sl = (slice(None), slice(h * hd, (h + 1) * hd))
        s0 = s_heads[h]
        s0s = _split(s0)
        ar = tuple(jnp.concatenate([x[sl], y[sl]], axis=0) for x, y in zip(a_t, r_t))
        bk = tuple(jnp.concatenate([x[sl], y[sl]], axis=0) for x, y in zip(b_t, k_t))
        gm = _dot3(ar, bk, 1, 1)
        yield
        a_ab = jnp.where(strict, gm[:c, :c], 0.0)
        a_ak = jnp.where(strict, gm[:c, c:], 0.0)
        ars =_dot3(ar, s0s, 1, 1)
        yield
        vh = _cut(vs, sl)
        x = ars[:c] + _dot3(_split(a_ak), vh)
        yield
        nd = jnp.where(blk_masks[0], a_ab, 0.0)
        tinv = eye + nd
        p = _split(nd)
        for _ in range(nsq):
            p = _split(_dot3(p, p))
            yield
            tinv = tinv + _dot3(_split(tinv), p)
            yield
        for m in blk_masks[1:]:
            ts = _split(tinv)
            half = _split(_dot3(ts, _split(jnp.where(m, a_ab, 0.0))))
            yield
            tinv = tinv + _dot3(half, ts)
            yield
        u = _dot3(_split(tinv), _split(x))
        yield
        us = _split(u)
        uv = tuple(jnp.concatenate([x, y], axis=0) for x, y in zip(us, vh))
        y_out = ars[c:] + _dot(jnp.where(incl2, gm[c:], 0.0).astype(BF16), uv[0])
        yield
        bkg = tuple(jnp.concatenate([x[sl], y[sl]], axis=0) for x, y in zip(b_g, k_g))
        s_out = s0 * gtot[sl] + _dot3(uv, bkg, 0, 0)
        yield (y_out, s_out)

    return [head(h) for h in range(nh)]


def _rwkv_scan_kernel(r0_ref, kk0_ref, v0_ref, lw0_ref, bt0_ref, kd0_ref,
                      r1_ref, kk1_ref, v1_ref, lw1_ref, bt1_ref, kd1_ref, s0_ref,
                      y0_ref, y1_ref, sfin_ref, s_ref, *, c, nh, hd):
    @pl.when(pl.program_id(1) == 0)
    def _():
        s_ref[...] = s0_ref[...]

    s_in = [[s_ref[d, h] for h in range(nh)] for d in range(2)]
    ga = _rwkv_chunk(0, c, nh, hd, r0_ref[...], kk0_ref[...], v0_ref[...], lw0_ref[...], bt0_ref[...],
                     kd0_ref[...], s_in[0])
    gb = _rwkv_chunk(1, c, nh, hd, r1_ref[...], kk1_ref[...], v1_ref[...], lw1_ref[...], bt1_ref[...],
                     kd1_ref[...], s_in[1])
    res = _staged([g for pair in zip(ga, gb) for g in pair])
    ra, rb = res[0::2], res[1::2]
    y0_ref[...] = jnp.concatenate([y for y, _ in ra], axis=1)
    y1_ref[...] = jnp.concatenate([y for y, _ in rb], axis=1)
    for h in range(nh):
        s_ref[0, h] = ra[h][1]
        s_ref[1, h] = rb[h][1]
        sfin_ref[0, h] = ra[h][1]
        sfin_ref[1, h] = rb[h][1]


def _rwkv_scan(r, kkn, v, lw, bt, kd, st, e, bsz, seq):
    n, db = r.shape
    nh = db // HEAD_B
    nc = seq // CH
    fwd = lambda b, c: (b * nc + c, 0)
    bwd = lambda b, c: (b * nc + nc - 1 - c, 0)
    row0 = pl.BlockSpec((CH, db), fwd)
    row1 = pl.BlockSpec((CH, db), bwd)
    dir0 = pl.BlockSpec((None, CH, db), lambda b, c: (0,) + fwd(b, c))
    dir1 = pl.BlockSpec((None, CH, db), lambda b, c: (1,) + bwd(b, c))
    st_in = pl.BlockSpec((None, None, 2, nh, HEAD_B, HEAD_B), lambda b, c: (b, e, 0, 0, 0, 0))
    st_out = pl.BlockSpec((None, 2, nh, HEAD_B, HEAD_B), lambda b, c: (b, 0, 0, 0, 0))
    return pl.pallas_call(
        functools.partial(_rwkv_scan_kernel, c=CH, nh=nh, hd=HEAD_B),
        grid=(bsz, nc),
        in_specs=[row0, row0, row0, dir0, dir0, dir0, row1, row1, row1, dir1, dir1, dir1, st_in],
        out_specs=[row0, row1, st_out],
        out_shape=[jax.ShapeDtypeStruct((n, db), F32), jax.ShapeDtypeStruct((n, db), F32),
                   jax.ShapeDtypeStruct((bsz, 2, nh, HEAD_B, HEAD_B), F32)],
        scratch_shapes=[pltpu.VMEM((2, nh, HEAD_B, HEAD_B), F32)],
        compiler_params=_cparams(("arbitrary", "arbitrary")),
        name="rwkv_scan",
    )(r, kkn, v, lw, bt, kd, r, kkn, v, lw, bt, kd, st)


def _log_sigmoid(x):
    return -_softplus(-x)


def _split3(x):
    hi = x.astype(BF16)
    r1 = x - hi.astype(F32)
    mid = r1.astype(BF16)
    return hi, mid, (r1 - mid.astype(F32)).astype(BF16)


def _gla_chunk(d, c, nh, dk, dv, p_ref, wup, bgk, s_heads):
    _, incl, _ = _tri_masks(c, d)
    hk = nh * dk
    q = p_ref[:, 0:hk] * dk ** -0.5
    k = p_ref[:, hk:2 * hk]
    gk = _dot3(_split(p_ref[:, 2 * hk + nh * dv:]), _split(wup)) + bgk
    la = _log_sigmoid(gk) / GATE_NORM
    la3 = _split3(la)
    mask_bf = incl.astype(BF16)
    cs = _dot(mask_bf, la3[0]) + _dot(mask_bf, la3[1]) + _dot(mask_bf, la3[2])
    tot = cs[c - 1:c] if d == 0 else cs[0:1]
    q_t = (q * jnp.exp(cs)).astype(BF16)
    k_t = (k * jnp.exp(-cs)).astype(BF16)
    k_e = _split(k * jnp.exp(tot - cs))
    ones = jnp.ones((c, dv), BF16)

    def head(h):
        sk = (slice(None), slice(h * dk, (h + 1) * dk))
        vh = _split(p_ref[:, 2 * hk + h * dv:2 * hk + (h + 1) * dv])
        s0 = s_heads[h]
        qh = q_t[sk]
        att = jnp.where(incl, _dot(qh, k_t[sk], 1, 1), 0.0)
        yield
        inter = _dot(qh, s0.astype(BF16))
        yield
        o_out = _dot(att.astype(BF16), vh[0]) + inter
        yield
        dec = jnp.exp(_dot(la3[0][sk], ones, 0, 0) + _dot(la3[1][sk], ones, 0, 0) + _dot(la3[2][sk], ones, 0, 0))
        yield
        yield (o_out, s0 * dec + _dot3(_cut(k_e, sk), vh, 0, 0))

    return [head(h) for h in range(nh)]


def _gla_kernel(p0_ref, p1_ref, s0_ref, wup_ref, bgk_ref, o0_ref, o1_ref, sfin_ref, s_ref, *, c, nh, dk, dv):
    @pl.when(pl.program_id(1) == 0)
    def _():
        s_ref[...] = s0_ref[...]

    s_in = [[s_ref[d, h] for h in range(nh)] for d in range(2)]
    ga = _gla_chunk(0, c, nh, dk, dv, p0_ref, wup_ref[0], bgk_ref[0], s_in[0])
    gb = _gla_chunk(1, c, nh, dk, dv, p1_ref, wup_ref[1], bgk_ref[1], s_in[1])
    res = _staged([g for pair in zip(ga, gb) for g in pair])
    ra, rb = res[0::2], res[1::2]
    o0_ref[...] = jnp.concatenate([o for o, _ in ra], axis=1)
    o1_ref[...] = jnp.concatenate([o for o, _ in rb], axis=1)
    for h in range(nh):
        s_ref[0, h] = ra[h][1]
        s_ref[1, h] = rb[h][1]
        sfin_ref[0, h] = ra[h][1]
        sfin_ref[1, h] = rb[h][1]


def _gla(p, st, o_, wup, bgk, bsz, seq, grid_mode):
    nh, dk, dv = st.shape[3:]
    hv = nh * dv
    nc = seq // CH
    if grid_mode:
        assert seq // GRID_W == CH and nc == GRID_W
        ncol = p.shape[2] // GRID_W
        p_v = p
        blk = lambda w: (None, CH, w)
        fwd = lambda b, c: (b, 0, c)
        bwd = lambda b, c: (b, 0, nc - 1 - c)
        o_shape = jax.ShapeDtypeStruct((bsz, CH, GRID_W * hv), F32)
    else:
        ncol = p.shape[1]
        p_v = p.reshape(bsz * nc, CH, ncol)
        blk = lambda w: (None, CH, w)
        fwd = lambda b, c: (b * nc + c, 0, 0)
        bwd = lambda b, c: (b * nc + nc - 1 - c, 0, 0)
        o_shape = jax.ShapeDtypeStruct((bsz * nc, CH, hv), F32)
    st_in = pl.BlockSpec((None, None, 2, nh, dk, dv), lambda b, c: (b, o_, 0, 0, 0, 0))
    st_out = pl.BlockSpec((None, 2, nh, dk, dv), lambda b, c: (b, 0, 0, 0, 0))
    o0, o1, sfin = pl.pallas_call(
        functools.partial(_gla_kernel, c=CH, nh=nh, dk=dk, dv=dv),
        grid=(bsz, nc),
        in_specs=[pl.BlockSpec(blk(ncol), fwd), pl.BlockSpec(blk(ncol), bwd), st_in,
                  pl.BlockSpec(wup.shape, lambda b, c: (0, 0, 0)),
                  pl.BlockSpec(bgk.shape, lambda b, c: (0, 0, 0))],
        out_specs=[pl.BlockSpec(blk(hv), fwd), pl.BlockSpec(blk(hv), bwd), st_out],
        out_shape=[o_shape, o_shape, jax.ShapeDtypeStruct((bsz, 2, nh, dk, dv), F32)],
        scratch_shapes=[pltpu.VMEM((2, nh, dk, dv), F32)],
        compiler_params=_cparams(("arbitrary", "arbitrary"), VMEM_BIG),
        name="gla",
    )(p_v, p_v, st, wup, bgk)
    if not grid_mode:
        o0, o1 = o0.reshape(bsz * seq, hv), o1.reshape(bsz * seq, hv)
    return o0, o1, sfin


def _router_tail(x1, mod, g2n_ref, rw_ref, x_out, h2_out, aff_out):
    x_out[...] = x1
    h2 = _rms(x1, g2n_ref[...]) * (1.0 + mod[4:5]) + mod[3:4]
    h2_out[...] = h2.astype(BF16)
    logits = _dot3(_split(rw_ref[...]), _split(h2), 1, 1)
    m = jnp.max(logits, axis=0, keepdims=True)
    ex = jnp.exp(logits - m)
    aff_out[...] = ex / jnp.sum(ex, axis=0, keepdims=True)


def _out_even_kernel(x_ref, mod_ref, ha0_ref, ha1_ref, ga_ref, y0_ref, y1_ref, g_ref, bonus_ref, lnw_ref, lnb_ref,
                     e_ref, wo_ref, g2n_ref, rw_ref, x_out, h2_out, aff_out, *, da, hd):
    mod = mod_ref[...]
    ya = (ha0_ref[...] + ha1_ref[...]) * _gelu_tanh(ga_ref[...])
    y = y0_ref[...] + y1_ref[...]
    e = e_ref[...]
    mean = _seg_sum(y, e) * (1.0 / hd)
    yc = y - mean
    var = _seg_sum(yc * yc, e) * (1.0 / hd)
    yn = yc * lax.rsqrt(var + GN_EPS_B) * lnw_ref[...] + lnb_ref[...]
    yb = (yn + bonus_ref[...]) * g_ref[...]
    out = _dot_bf(ya, wo_ref[0:da, :]) + _dot_bf(yb, wo_ref[da:, :])
    _router_tail(x_ref[...] + mod[2:3] * out, mod, g2n_ref, rw_ref, x_out, h2_out, aff_out)


def _out_odd_kernel(x_ref, mod_ref, o0_ref, o1_ref, g_ref, gw_ref, wo_ref, g2n_ref, rw_ref,
                    x_out, h2_out, aff_out, *scratch, nh, dv, grid_rows):
    mod = mod_ref[...]
    gw = gw_ref[...]
    if grid_rows:
        hv = nh * dv
        o_scr = scratch[0]
        for c in range(GRID_W):
            for j in range(hv // LANES):
                lanes = slice(c * hv + j * LANES, c * hv + (j + 1) * LANES)
                o_scr[j, pl.ds(c, grid_rows, stride=GRID_W), :] = o0_ref[:, lanes] + o1_ref[:, lanes]
        o_all = jnp.concatenate([o_scr[j] for j in range(hv // LANES)], axis=1)
    else:
        o_all = o0_ref[...] + o1_ref[...]
    parts = []
    for h in range(nh):
        o = o_all[:, h * dv:(h + 1) * dv]
        parts.append(o * lax.rsqrt(jnp.mean(o * o, axis=-1, keepdims=True) + EPS) * gw)
    y = jnp.concatenate(parts, axis=1) * _silu(g_ref[...])
    out = _dot_bf(y, wo_ref[...])
    _router_tail(x_ref[...] + mod[2:3] * out, mod, g2n_ref, rw_ref, x_out, h2_out, aff_out)


def _out_common(kern, x, modr, mod_idx, extra_args, extra_specs, norm2_g, router_wt, bsz, seq, name,
                tb=TB, scratch=()):
    n, d = x.shape
    ne = router_wt.shape[0]
    nt = seq // tb
    row = lambda b, t: (b * nt + t, 0)
    in_specs = ([pl.BlockSpec((tb, d), row), pl.BlockSpec((None, 6, d), lambda b, t: (mod_idx(b), 0, 0))]
                + extra_specs
                + [pl.BlockSpec((1, d), lambda b, t: (0, 0)), pl.BlockSpec((ne, d), lambda b, t: (0, 0))])
    return pl.pallas_call(
        kern, grid=(bsz, nt), in_specs=in_specs,
        out_specs=[pl.BlockSpec((tb, d), row), pl.BlockSpec((tb, d), row),
                   pl.BlockSpec((ne, tb), lambda b, t: (0, b * nt + t))],
        out_shape=[jax.ShapeDtypeStruct((n, d), F32), jax.ShapeDtypeStruct((n, d), BF16),
                   jax.ShapeDtypeStruct((ne, n), F32)],
        scratch_shapes=list(scratch),
        compiler_params=_cparams(("arbitrary", "arbitrary"), VMEM_BIG),
        name=name,
    )(x, modr, *extra_args, norm2_g.reshape(1, d), router_wt)


def _select_kernel(aff_ref, slot_ref, cnt_ref, *, cap, ntb, tokb):
    aff = aff_ref[...]
    ne = aff.shape[0]
    big = jnp.float32(4.0)

    def body(carry):
        lo, hi, _, _ = carry
        inr = (aff > lo) & (aff <= hi)
        mn = jnp.min(jnp.where(inr, aff, big), axis=1, keepdims=True)
        mx = jnp.max(jnp.where(inr, aff, -big), axis=1, keepdims=True)
        done = mn >= mx
        piv = mn + (mx - mn) * 0.5
        piv = jnp.where(piv >= mx, mn, piv)
        ge = jnp.sum((aff > piv).astype(jnp.int32), axis=1, keepdims=True) >= cap
        lo = jnp.where(done | ~ge, lo, piv)
        hi = jnp.where(done | ge, hi, piv)
        return lo, hi, mx, jnp.sum((~done).astype(jnp.int32))

    init = (jnp.full((ne, 1), -1.0, F32), jnp.full((ne, 1), big, F32), jnp.zeros((ne, 1), F32), jnp.int32(1))
    _, _, thr, _ = lax.while_loop(lambda carry: carry[3] > 0, body, init)
    gt = aff > thr
    eq = aff == thr
    need = cap - jnp.sum(gt.astype(jnp.int32), axis=1, keepdims=True)
    si = lax.broadcasted_iota(jnp.int32, (tokb, tokb), 0)
    ti = lax.broadcasted_iota(jnp.int32, (tokb, tokb), 1)
    ustrict = (si < ti).astype(BF16)

    def excl_count(mask_blk, carry):
        m = mask_blk.astype(BF16)
        pre = _dot(m, ustrict).astype(jnp.int32) + carry
        return pre, carry + jnp.sum(mask_blk.astype(jnp.int32), axis=1, keepdims=True)

    ceq = jnp.zeros((ne, 1), jnp.int32)
    csel = jnp.zeros((ne, 1), jnp.int32)
    lane = lax.broadcasted_iota(jnp.int32, (ne, 128), 1)
    cnt_acc = jnp.zeros((ne, 128), jnp.int32)
    for j in range(ntb):
        sl = slice(j * tokb, (j + 1) * tokb)
        eq_rank, ceq = excl_count(eq[:, sl], ceq)
        sel = gt[:, sl] | (eq[:, sl] & (eq_rank < need))
        cnt_acc = jnp.where(lane == j, csel, cnt_acc)
        rank, csel = excl_count(sel, csel)
        slot_ref[:, sl] = jnp.where(sel, rank, -1)
    cnt_acc = jnp.where(lane == ntb, csel, cnt_acc)
    cnt_ref[...] = cnt_acc


def _select(aff_t, cap):
    ne, n = aff_t.shape
    ntb = n // TOK_B
    assert ntb < 128
    return pl.pallas_call(
        functools.partial(_select_kernel, cap=cap, ntb=ntb, tokb=TOK_B),
        grid=(1,),
        in_specs=[pl.BlockSpec((ne, n), lambda i: (0, 0))],
        out_specs=[pl.BlockSpec((ne, n), lambda i: (0, 0)), pl.BlockSpec((ne, 128), lambda i: (0, 0))],
        out_shape=[jax.ShapeDtypeStruct((ne, n), jnp.int32), jax.ShapeDtypeStruct((ne, 128), jnp.int32)],
        compiler_params=_cparams(("arbitrary",)),
        name="select",
    )(aff_t)


def _gather_kernel(lo_ref, hi_ref, slot_ref, h2_ref, xe_ref, acc_ref, *, nsb, slotb, tokb):
    e = pl.program_id(0)
    sb = pl.program_id(1)
    base = sb * slotb
    sidx = lax.broadcasted_iota(jnp.int32, (slotb, tokb), 0) + base
    acc_ref[...] = jnp.zeros_like(acc_ref)

    lo = lo_ref[e * nsb + sb]
    hi = hi_ref[e * nsb + sb]

    def picked(tb, live):
        srow = jnp.where(live, slot_ref[pl.ds(tb, 1), :], -1)
        onehot = jnp.where(srow == sidx, 1.0, 0.0).astype(BF16)
        off = pl.multiple_of(tb * tokb, tokb)
        return _dot(onehot, h2_ref[pl.ds(off, tokb), :])

    def body(i, carry):
        tb0 = lo + 2 * i
        tb1 = jnp.minimum(tb0 + 1, hi)
        acc_ref[...] += picked(tb0, True) + picked(tb1, tb0 + 1 <= hi)
        return carry

    lax.fori_loop(0, (hi - lo + 2) // 2, body, 0)
    xe_ref[...] = acc_ref[...].astype(BF16)


def _gather(lo, hi, slot3, h2, cap):
    ne, ntb, tokb = slot3.shape
    n, d = h2.shape
    nsb = cap // SLOT_B
    return pl.pallas_call(
        functools.partial(_gather_kernel, nsb=nsb, slotb=SLOT_B, tokb=tokb),
        grid_spec=pltpu.PrefetchScalarGridSpec(
            num_scalar_prefetch=2, grid=(ne, nsb),
            in_specs=[pl.BlockSpec((None, ntb, tokb), lambda e, s, lo, hi: (e, 0, 0)),
                      pl.BlockSpec((n, d), lambda e, s, lo, hi: (0, 0), pipeline_mode=pl.Buffered(1))],
            out_specs=pl.BlockSpec((None, SLOT_B, d), lambda e, s, lo, hi: (e, s, 0)),
            scratch_shapes=[pltpu.VMEM((SLOT_B, d), F32)]),
        out_shape=jax.ShapeDtypeStruct((ne, cap, d), BF16),
        compiler_params=_cparams(("arbitrary", "arbitrary"), VMEM_BIG),
        name="moe_gather",
    )(lo, hi, slot3, h2)


def _ffn_kernel(xe_ref, wg_ref, wu_ref, wd_ref, ye_ref, acc_ref):
    f = pl.program_id(1)
    xe = xe_ref[...]
    hg = _dot(xe, wg_ref[...].astype(BF16))
    hu = _dot(xe, wu_ref[...].astype(BF16))
    hid = (_silu(hg) * hu).astype(BF16)
    part = _dot(hid, wd_ref[...].astype(BF16))

    @pl.when(f == 0)
    def _():
        acc_ref[...] = part

    @pl.when(f > 0)
    def _():
        acc_ref[...] += part

    @pl.when(f == pl.num_programs(1) - 1)
    def _():
        ye_ref[...] = acc_ref[...].astype(BF16)


def _ffn(xe, w_gate, w_up, w_down, l):
    ne, cap, d = xe.shape
    dff = w_gate.shape[-1]
    nf = dff // FF_B
    return pl.pallas_call(
        _ffn_kernel,
        grid=(ne, nf),
        in_specs=[pl.BlockSpec((None, cap, d), lambda e, f: (e, 0, 0)),
                  pl.BlockSpec((None, None, d, FF_B), lambda e, f: (l, e, 0, f)),
                  pl.BlockSpec((None, None, d, FF_B), lambda e, f: (l, e, 0, f)),
                  pl.BlockSpec((None, None, FF_B, d), lambda e, f: (l, e, f, 0))],
        out_specs=pl.BlockSpec((None, cap, d), lambda e, f: (e, 0, 0)),
        out_shape=jax.ShapeDtypeStruct((ne, cap, d), BF16),
        scratch_shapes=[pltpu.VMEM((cap, d), F32)],
        compiler_params=_cparams(("arbitrary", "arbitrary"), VMEM_BIG),
        name="moe_ffn",
    )(xe, w_gate, w_up, w_down)


def _scatter_kernel(lo_ref, hi_ref, slot_ref, gate_ref, ye_ref, out_ref, *, nsb, slotb, tokb):
    e = pl.program_id(0)
    sb = pl.program_id(1)

    @pl.when((e == 0) & (sb == 0))
    def _():
        out_ref[...] = jnp.zeros_like(out_ref)

    base = sb * slotb
    sidx = lax.broadcasted_iota(jnp.int32, (slotb, tokb), 0) + base
    ye = ye_ref[...]

    lo = lo_ref[e * nsb + sb]
    hi = hi_ref[e * nsb + sb]

    def spread(tb, live):
        srow = jnp.where(live, slot_ref[pl.ds(tb, 1), :], -1)
        grow = gate_ref[pl.ds(tb, 1), :]
        gm = jnp.where(srow == sidx, grow, 0.0).astype(BF16)
        return _dot(gm, ye, 0, 0)

    def body(i, carry):
        tb0 = lo + 2 * i
        tb1 = jnp.minimum(tb0 + 1, hi)
        add0 = spread(tb0, True)
        add1 = spread(tb1, tb0 + 1 <= hi)
        out_ref[pl.ds(pl.multiple_of(tb0 * tokb, tokb), tokb), :] += add0
        out_ref[pl.ds(pl.multiple_of(tb1 * tokb, tokb), tokb), :] += add1
        return carry

    lax.fori_loop(0, (hi - lo + 2) // 2, body, 0)


def _scatter(lo, hi, slot3, gate3, ye, n):
    ne, cap, d = ye.shape
    _, ntb, tokb = slot3.shape
    nsb = cap // SLOT_B
    return pl.pallas_call(
        functools.partial(_scatter_kernel, nsb=nsb, slotb=SLOT_B, tokb=tokb),
        grid_spec=pltpu.PrefetchScalarGridSpec(
            num_scalar_prefetch=2, grid=(ne, nsb),
            in_specs=[pl.BlockSpec((None, ntb, tokb), lambda e, s, lo, hi: (e, 0, 0)),
                      pl.BlockSpec((None, ntb, tokb), lambda e, s, lo, hi: (e, 0, 0)),
                      pl.BlockSpec((None, SLOT_B, d), lambda e, s, lo, hi: (e, s, 0))],
            out_specs=pl.BlockSpec((n, d), lambda e, s, lo, hi: (0, 0), pipeline_mode=pl.Buffered(1))),
        out_shape=jax.ShapeDtypeStruct((n, d), F32),
        compiler_params=_cparams(("arbitrary", "arbitrary"), VMEM_BIG),
        name="moe_scatter",
    )(lo, hi, slot3, gate3, ye)


def _moe(h2, aff_t, w_gate, w_up, w_down, l):
    n, d = h2.shape
    ne = aff_t.shape[0]
    cap = max(1, CAP_FACTOR * n // ne)
    assert cap % SLOT_B == 0 and n % TOK_B == 0
    ntb = n // TOK_B
    nsb = cap // SLOT_B
    slot, cnt = _select(aff_t, cap)
    cnt = cnt[:, :ntb + 1]
    starts = (jnp.arange(nsb, dtype=jnp.int32) * SLOT_B)[None, :, None]
    lo = jnp.clip(jnp.sum(cnt[:, None, 1:] <= starts, -1), 0, ntb - 1).astype(jnp.int32).reshape(-1)
    hi = jnp.clip(jnp.sum(cnt[:, None, 1:] < starts + SLOT_B, -1), 0, ntb - 1).astype(jnp.int32).reshape(-1)
    slot3 = slot.reshape(ne, ntb, TOK_B)
    gate3 = aff_t.reshape(ne, ntb, TOK_B)
    xe = _gather(lo, hi, slot3, h2, cap)
    ye = _ffn(xe, w_gate, w_up, w_down, l)
    return _scatter(lo, hi, slot3, gate3, ye, n)


def _final_kernel(x_ref, moe_ref, mod_ref, g_ref, o_ref):
    x = x_ref[...] + mod_ref[...][5:6] * moe_ref[...]
    o_ref[...] = _rms(x, g_ref[...])


def _final(x, moe, modr, mod_idx, gain, bsz, seq):
    n, d = x.shape
    nt = seq // TB
    row = lambda b, t: (b * nt + t, 0)
    return pl.pallas_call(
        _final_kernel, grid=(bsz, nt),
        in_specs=[pl.BlockSpec((TB, d), row), pl.BlockSpec((TB, d), row),
                  pl.BlockSpec((None, 6, d), lambda b, t: (mod_idx(b), 0, 0)),
                  pl.BlockSpec((1, d), lambda b, t: (0, 0))],
        out_specs=pl.BlockSpec((TB, d), row),
        out_shape=jax.ShapeDtypeStruct((n, d), F32),
        compiler_params=_cparams(("arbitrary", "arbitrary")),
        name="final_norm",
    )(x, moe, modr, gain.reshape(1, d))


def _block_diag(w):
    h, n, _ = w.shape
    eye = jnp.eye(h, dtype=w.dtype)
    return (eye[:, None, :, None] * w[:, :, None, :]).reshape(h * n, h * n)


def _lora_cat(w2):
    _, l, c = w2.shape
    z = jnp.zeros((l, c), w2.dtype)
    return jnp.concatenate([jnp.concatenate([w2[0], z], 1), jnp.concatenate([z, w2[1]], 1)], 0)


def _trunk(x, bsz, seq, cond_base, per_batch, st_rglru, st_rwkv, st_gla, grid_mode, modr, P):
    d_model = x.shape[1]
    depth = P['norm1_g'].shape[0]
    new_rglru, new_rwkv, new_gla = [], [], []
    moe_prev = None
    for l in range(depth):
        mod_idx = (lambda b, l=l: l * 8 + cond_base + b) if per_batch else (lambda b, l=l: l * 8 + cond_base)
        mod_prev = (lambda b, l=l: (l - 1) * 8 + cond_base + b) if per_batch else (lambda b, l=l: (l - 1) * 8 + cond_base)
        router_wt = P['router_w'][l].T
        if l % 2 == 0:
            e = l // 2
            da = P['a_conv_w'].shape[-1]
            db = P['b_k_k'].shape[-1]
            w_in = P['ev_w_in'][e].astype(BF16)
            nb_cols = w_in.shape[1] - 2 * da
            res = _inproj(x, moe_prev, mod_prev, modr, mod_idx, P['norm1_g'][l], w_in, (2 * da, nb_cols), bsz, seq)
            if moe_prev is not None:
                x, pa, pb = res
            else:
                pa, pb = res
            ha, sa = [], []
            for d in range(2):
                wg = jnp.concatenate([_block_diag(P['a_w_rg'][e, d]), _block_diag(P['a_w_ig'][e, d])], 1).astype(BF16)
                bg = jnp.concatenate([P['a_b_rg'][e, d], P['a_b_ig'][e, d]])
                h_d, hf = _rglru(pa, P['a_conv_w'][e], P['a_conv_b'][e], wg, bg, P['a_lam'][e, d],
                                 st_rglru[:, e, d], d, bsz, seq)
                ha.append(h_d)
                sa.append(hf.reshape(bsz, da))
            new_rglru.append(jnp.stack(sa, 1))
            e_bf = _block_diag(jnp.ones((db // HEAD_B, HEAD_B, HEAD_B), BF16))
            r, v, kkn, g, bonus, lw, bt, kd = _rwkv_prep(
                pb, P['b_mu'][e], P['b_w0'][e].reshape(1, 2 * db), _lora_cat(P['b_w2'][e]),
                P['b_a0'][e].reshape(1, 2 * db), _lora_cat(P['b_a2'][e]), P['b_g2'][e].astype(BF16),
                P['b_k_k'][e], P['b_k_a'][e], P['b_r_k'][e], e_bf, bsz, seq)
            y0, y1, sfin = _rwkv_scan(r, kkn, v, lw, bt, kd, st_rwkv, e, bsz, seq)
            ys = [y0, y1]
            new_rwkv.append(sfin)
            nt = seq // TB
            rowh = pl.BlockSpec((TB, da), lambda b, t: (b * nt + t, 0))
            rowg = pl.BlockSpec((TB, da), lambda b, t: (b * nt + t, 1))
            vec = pl.BlockSpec((1, db), lambda b, t: (0, 0))
            x, h2, aff_t = _out_common(
                functools.partial(_out_even_kernel, da=da, hd=HEAD_B), x, modr, mod_idx,
                [ha[0], ha[1], pa, ys[0], ys[1], g, bonus, P['b_ln_w'][e].reshape(1, db), P['b_ln_b'][e].reshape(1, db),
                 e_bf, P['ev_w_out'][e].astype(BF16)],
                [rowh, rowh, rowg, rowh, rowh, rowh, rowh, vec, vec,
                 pl.BlockSpec((db, db), lambda b, t: (0, 0)),
                 pl.BlockSpec((da + db, d_model), lambda b, t: (0, 0))],
                P['norm2_g'][l], router_wt, bsz, seq, "out_even")
        else:
            o_ = l // 2
            w_in = P['od_w_in'][o_]
            nh, dk, dv = st_gla.shape[3:]
            hv = nh * dv
            qkv = 2 * nh * dk + hv
            gin = -(2 * GATE_RANK) % 128 + 2 * GATE_RANK
            w_in = jnp.concatenate([w_in[:, :qkv], w_in[:, qkv + hv:],
                                    jnp.zeros((d_model, gin - 2 * GATE_RANK), w_in.dtype),
                                    w_in[:, qkv:qkv + hv]], axis=1).astype(BF16)
            res = _inproj(x, moe_prev, mod_prev, modr, mod_idx, P['norm1_g'][l], w_in, (qkv + gin, hv), bsz, seq,
                          grid_view=grid_mode)
            if moe_prev is not None:
                x, p, g = res
            else:
                p, g = res
            wup = jnp.stack([jnp.zeros((gin, nh * dk), F32).at[d * GATE_RANK:(d + 1) * GATE_RANK]
                             .set(P['c_w_gk_up'][o_, d]) for d in range(2)])
            o0, o1, sfin = _gla(p, st_gla, o_, wup, P['c_b_gk'][o_].reshape(2, 1, nh * dk), bsz, seq, grid_mode)
            new_gla.append(sfin)
            tb = SUBLANES * GRID_W if grid_mode else TB
            nt = seq // tb
            if grid_mode:
                rowo = pl.BlockSpec((None, SUBLANES, GRID_W * hv), lambda b, t: (b, t, 0))
                scratch = [pltpu.VMEM((hv // LANES, tb, LANES), F32)]
            else:
                rowo = pl.BlockSpec((tb, hv), lambda b, t: (b * nt + t, 0))
                scratch = []
            rowg = pl.BlockSpec((tb, hv), lambda b, t: (b * nt + t, 0))
            x, h2, aff_t = _out_common(
                functools.partial(_out_odd_kernel, nh=nh, dv=dv, grid_rows=SUBLANES if grid_mode else 0),
                x, modr, mod_idx,
                [o0, o1, g, P['c_gnorm_w'][o_].reshape(1, dv), P['od_w_out'][o_].astype(BF16)],
                [rowo, rowo, rowg, pl.BlockSpec((1, dv), lambda b, t: (0, 0)),
                 pl.BlockSpec((hv, d_model), lambda b, t: (0, 0))],
                P['norm2_g'][l], router_wt, bsz, seq, "out_odd", tb=tb, scratch=scratch)
        moe_prev = _moe(h2, aff_t, P['moe_w_gate'], P['moe_w_up'], P['moe_w_down'], l)
    mod_last = (lambda b: (depth - 1) * 8 + cond_base + b) if per_batch else (lambda b: (depth - 1) * 8 + cond_base)
    y = _final(x, moe_prev, modr, mod_last, P['final_norm_g'], bsz, seq)
    return y, jnp.stack(new_rglru, 1), jnp.stack(new_rwkv, 1), jnp.stack(new_gla, 1)


def kernel(x_prompt, x_sample, state_rglru, state_rwkv, state_gla, c, c_ctx, norm1_g, norm2_g, ada_w, ada_b, router_w, moe_w_gate, moe_w_up, moe_w_down, ev_w_in, ev_w_out, a_conv_w, a_conv_b, a_w_rg, a_b_rg, a_w_ig, a_b_ig, a_lam, b_mu, b_w0, b_w2, b_a0, b_a2, b_g2, b_k_k, b_k_a, b_r_k, b_ln_w, b_ln_b, od_w_in, od_w_out, c_w_gk_up, c_b_gk, c_gnorm_w, final_norm_g):
    P = dict(norm1_g=norm1_g, norm2_g=norm2_g, router_w=router_w,
             moe_w_gate=moe_w_gate, moe_w_up=moe_w_up, moe_w_down=moe_w_down,
             ev_w_in=ev_w_in, ev_w_out=ev_w_out, a_conv_w=a_conv_w, a_conv_b=a_conv_b,
             a_w_rg=a_w_rg, a_b_rg=a_b_rg, a_w_ig=a_w_ig, a_b_ig=a_b_ig, a_lam=a_lam,
             b_mu=b_mu, b_w0=b_w0, b_w2=b_w2, b_a0=b_a0, b_a2=b_a2, b_g2=b_g2, b_k_k=b_k_k,
             b_k_a=b_k_a, b_r_k=b_r_k, b_ln_w=b_ln_w, b_ln_b=b_ln_b,
             od_w_in=od_w_in, od_w_out=od_w_out, c_w_gk_up=c_w_gk_up, c_b_gk=c_b_gk,
             c_gnorm_w=c_gnorm_w, final_norm_g=final_norm_g)
    bp, tp, d_model = x_prompt.shape
    bs, ts, _ = x_sample.shape
    assert bs + 1 <= 8 and tp % TB == 0 and ts % TB == 0
    depth = ada_w.shape[0]
    cv8 = jnp.zeros((8, d_model), F32).at[0].set(c_ctx).at[1:1 + bs].set(c)
    modr = _modulation(cv8, ada_w, ada_b).reshape(depth * 8, 6, d_model)
    dt = x_prompt.dtype
    z_rglru = jnp.zeros((bp,) + state_rglru.shape[1:], dt)
    z_rwkv = jnp.zeros((bp,) + state_rwkv.shape[1:], dt)
    z_gla = jnp.zeros((bp,) + state_gla.shape[1:], dt)
    y_prompt, n_rglru, n_rwkv, n_gla = _trunk(x_prompt.reshape(bp * tp, d_model), bp, tp, 0, False,
                                              z_rglru, z_rwkv, z_gla, False, modr, P)
    y_sample, _, _, _ = _trunk(x_sample.reshape(bs * ts, d_model), bs, ts, 1, True,
                               state_rglru, state_rwkv, state_gla, True, modr, P)
    return (y_prompt.reshape(bp, tp, d_model), y_sample.reshape(bs, ts, d_model), n_rglru, n_rwkv, n_gla)
```

```python
import functools
import math

import numpy as np
import jax
import jax.numpy as jnp
from jax import lax
from jax.experimental import pallas as pl
from jax.experimental.pallas import tpu as pltpu

F32 = jnp.float32
BF16 = jnp.bfloat16
HI = lax.Precision.HIGHEST

GRID_W = 64
EPS = 1e-6
C_RGLRU = 8.0
GN_EPS_B = 64e-5
GATE_NORM = 16.0
CAP_FACTOR = 2
HEAD_B = 64
H_A = 8
H_C = 4
GATE_RANK = 16

SUBLANES = 8
LANES = 128
MXU_N = 256
PITCH_PAD = 4
TB = 256
CH = 64
SUB = 16
SLOT_W = 128
SLOT_ALIGN = 16
TOK_B = 512
FF_B = 512
VMEM_BIG = 56 * 1024 * 1024


def _dot(a, b, ca=1, cb=0, prec=None):
    return lax.dot_general(a, b, (((ca,), (cb,)), ((), ())), precision=prec,
                           preferred_element_type=F32)


def _dot_bf(a, b, ca=1, cb=0):
    return _dot(a.astype(BF16), b.astype(BF16), ca, cb)


def _dot_hi(a, b, ca=1, cb=0):
    return _dot(a.astype(F32), b.astype(F32), ca, cb, HI)


def _seg_sum(x, e_bf):
    hi = x.astype(BF16)
    lo = (x - hi.astype(F32)).astype(BF16)
    return _dot(hi, e_bf) + _dot(lo, e_bf)


def _mask_dot(mask_bf, x, cm=1, cx=0):
    hi = x.astype(BF16)
    r1 = x - hi.astype(F32)
    mid = r1.astype(BF16)
    lo = (r1 - mid.astype(F32)).astype(BF16)
    return _dot(mask_bf, hi, cm, cx) + _dot(mask_bf, mid, cm, cx) + _dot(mask_bf, lo, cm, cx)


def _sigmoid(x):
    return 1.0 / (1.0 + jnp.exp(-x))


def _softplus(x):
    return jnp.maximum(x, 0.0) + jnp.log1p(jnp.exp(-jnp.abs(x)))


def _silu(x):
    return x * _sigmoid(x)


def _gelu_tanh(x):
    return 0.5 * x * (1.0 + jnp.tanh(math.sqrt(2.0 / math.pi) * (x + 0.044715 * (x * x * x))))


def _rms(x, g):
    return x * lax.rsqrt(jnp.mean(x * x, axis=-1, keepdims=True) + EPS) * g


def _cparams(sem, vmem=None):
    return pltpu.CompilerParams(dimension_semantics=sem, vmem_limit_bytes=vmem)


def _mod_kernel(c_ref, w_ref, b_ref, o_ref):
    o_ref[...] = _dot_hi(_silu(c_ref[...]), w_ref[...]) + b_ref[...]


def _modulation(cv8, ada_w, ada_b):
    depth, d, n6 = ada_w.shape
    nb = 1536
    return pl.pallas_call(
        _mod_kernel,
        grid=(depth, n6 // nb),
        in_specs=[pl.BlockSpec((8, d), lambda l, j: (0, 0)),
                  pl.BlockSpec((None, d, nb), lambda l, j: (l, 0, j)),
                  pl.BlockSpec((None, 1, nb), lambda l, j: (l, 0, j))],
        out_specs=pl.BlockSpec((None, 8, nb), lambda l, j: (l, 0, j)),
        out_shape=jax.ShapeDtypeStruct((depth, 8, n6), F32),
        compiler_params=_cparams(("arbitrary", "arbitrary")),
        name="modulation",
    )(cv8, ada_w, ada_b.reshape(depth, 1, n6))


def _project(x, mod, g_ref, w_ref, outs, splits, grid_rows):
    h = _rms(x, g_ref[...]) * (1.0 + mod[1:2]) + mod[0:1]
    p = _dot_bf(h, w_ref[...])
    lo = 0
    if grid_rows:
        n0 = splits[0]
        scr = outs[-1]
        pitch = GRID_W + PITCH_PAD
        for j in range(n0 // LANES):
            for r in range(grid_rows):
                scr[j, r * pitch:r * pitch + GRID_W, :] = p[r * GRID_W:(r + 1) * GRID_W, j * LANES:(j + 1) * LANES]
        for c in range(GRID_W):
            for j in range(n0 // LANES):
                outs[0][:, c * n0 + j * LANES:c * n0 + (j + 1) * LANES] = scr[j, pl.ds(c, grid_rows, stride=pitch), :]
        lo, outs, splits = n0, outs[1:-1], splits[1:]
    for o_ref, n in zip(outs, splits):
        o_ref[...] = p[:, lo:lo + n]
        lo += n


def _inproj_kernel(x_ref, mod_ref, g_ref, w_ref, *outs, splits, grid_rows):
    _project(x_ref[...], mod_ref[...], g_ref, w_ref, outs, splits, grid_rows)


def _inproj(x, modr, mod_idx, gain, w_bf, splits, bsz, seq, grid_view=False):
    n, d = x.shape
    tb = SUBLANES * GRID_W if grid_view else TB
    nt = seq // tb
    ncols = w_bf.shape[1]
    row = lambda b, t: (b * nt + t, 0)
    in_specs = [pl.BlockSpec((tb, d), row),
                pl.BlockSpec((None, 6, d), lambda b, t: (mod_idx(b), 0, 0)),
                pl.BlockSpec((1, d), lambda b, t: (0, 0)),
                pl.BlockSpec((d, ncols), lambda b, t: (0, 0), pipeline_mode=pl.Buffered(1))]
    args = [x, modr, gain.reshape(1, d), w_bf]
    out_shape = [jax.ShapeDtypeStruct((n, c), F32) for c in splits]
    out_specs = [pl.BlockSpec((tb, c), row) for c in splits]
    scratch = []
    grid_rows = 0
    if grid_view:
        grid_rows = tb // GRID_W
        out_shape[0] = jax.ShapeDtypeStruct((bsz, seq // GRID_W, GRID_W * splits[0]), F32)
        out_specs[0] = pl.BlockSpec((None, grid_rows, GRID_W * splits[0]), lambda b, t: (b, t, 0))
        scratch = [pltpu.VMEM((splits[0] // LANES, grid_rows * (GRID_W + PITCH_PAD), LANES), F32)]
    return pl.pallas_call(
        functools.partial(_inproj_kernel, splits=splits, grid_rows=grid_rows),
        grid=(bsz, nt), in_specs=in_specs, out_specs=out_specs, out_shape=out_shape,
        scratch_shapes=scratch,
        compiler_params=_cparams(("arbitrary", "arbitrary"), VMEM_BIG),
        name="inproj",
    )(*args)


def _fill_ext(ext_ref, x, prev_ref, next_ref, first, last, tb):
    ext_ref[0:8, :] = jnp.where(first, 0.0, prev_ref[...])
    ext_ref[8:8 + tb, :] = x
    ext_ref[8 + tb:16 + tb, :] = jnp.where(last, 0.0, next_ref[...])


def _rglru_kernel(xa_ref, prev_ref, next_ref, cw_ref, cb_ref, wg_ref, bg_ref, lam_ref, h0_ref,
                  h_ref, hfin_ref, ext_ref, carry_ref, *, d, nt, tb):
    i = pl.program_id(1)
    t = i if d == 0 else nt - 1 - i
    first = t == 0
    last = t == nt - 1
    x = xa_ref[...]
    da = x.shape[1]
    _fill_ext(ext_ref, x, prev_ref, next_ref, first, last, tb)
    cw = cw_ref[...]
    u = (cw[0:1] * ext_ref[7:7 + tb, :] + cw[1:2] * x + cw[2:3] * ext_ref[9:9 + tb, :]
         + cw[3:4] * ext_ref[10:10 + tb, :] + cb_ref[...])
    gts = _dot_bf(u, wg_ref[...]) + bg_ref[...]
    gate_r = _sigmoid(gts[:, :da])
    gate_i = _sigmoid(gts[:, da:])
    log_a = -C_RGLRU * gate_r * _softplus(-lam_ref[...])
    a = jnp.exp(log_a)
    xin = jnp.sqrt(1.0 - jnp.exp(2.0 * log_a)) * gate_i * u

    @pl.when(i == 0)
    def _():
        carry_ref[...] = h0_ref[...]

    rows = lax.broadcasted_iota(jnp.int32, (tb, da), 0)
    s = 1
    while s < tb:
        if d == 0:
            a_sh = pltpu.roll(a, s, 0)
            x_sh = pltpu.roll(xin, s, 0)
            valid = rows >= s
        else:
            a_sh = pltpu.roll(a, tb - s, 0)
            x_sh = pltpu.roll(xin, tb - s, 0)
            valid = rows < tb - s
        xin = jnp.where(valid, a * x_sh + xin, xin)
        a = jnp.where(valid, a * a_sh, a)
        s *= 2
    h = xin + a * carry_ref[...]
    h_ref[...] = h
    newc = h[tb - 1:tb, :] if d == 0 else h[0:1, :]
    carry_ref[...] = newc
    hfin_ref[...] = newc


def _rglru(pa, conv_w, conv_b, wg_d, bg_d, lam_d, h0_d, d, bsz, seq):
    n = pa.shape[0]
    da = conv_w.shape[1]
    nt = seq // TB
    r8 = TB // 8
    tt = (lambda t: t) if d == 0 else (lambda t: nt - 1 - t)
    blk = lambda b, t: b * nt + tt(t)
    return pl.pallas_call(
        functools.partial(_rglru_kernel, d=d, nt=nt, tb=TB),
        grid=(bsz, nt),
        in_specs=[pl.BlockSpec((TB, da), lambda b, t: (blk(b, t), 0)),
                  pl.BlockSpec((8, da), lambda b, t: (jnp.maximum(blk(b, t) * r8 - 1, 0), 0)),
                  pl.BlockSpec((8, da), lambda b, t: (jnp.minimum((blk(b, t) + 1) * r8, n // 8 - 1), 0)),
                  pl.BlockSpec((4, da), lambda b, t: (0, 0)),
                  pl.BlockSpec((1, da), lambda b, t: (0, 0)),
                  pl.BlockSpec((da, 2 * da), lambda b, t: (0, 0)),
                  pl.BlockSpec((1, 2 * da), lambda b, t: (0, 0)),
                  pl.BlockSpec((1, da), lambda b, t: (0, 0)),
                  pl.BlockSpec((None, 1, da), lambda b, t: (b, 0, 0))],
        out_specs=[pl.BlockSpec((TB, da), lambda b, t: (blk(b, t), 0)),
                   pl.BlockSpec((None, 1, da), lambda b, t: (b, 0, 0))],
        out_shape=[jax.ShapeDtypeStruct((n, da), F32), jax.ShapeDtypeStruct((bsz, 1, da), F32)],
        scratch_shapes=[pltpu.VMEM((TB + 16, da), F32), pltpu.VMEM((1, da), F32)],
        compiler_params=_cparams(("arbitrary", "arbitrary")),
        name=f"rglru_d{d}",
    )(pa, pa, pa, conv_w, conv_b.reshape(1, da), wg_d, bg_d.reshape(1, 2 * da), lam_d.reshape(1, da),
      h0_d.reshape(bsz, 1, da))


def _rwkv_prep_kernel(pb_ref, prev_ref, next_ref, mu_ref, w0_ref, w2_ref, a0_ref, a2_ref, g2_ref,
                      kk_ref, ka_ref, rk_ref, e_ref,
                      r_out, v_out, kk_out, g_out, bonus_out, lw_out, bt_out, kd_out, ext_ref, *, nt, tb, db, lw_, la_):
    t = pl.program_id(1)
    x = pb_ref[...]
    _fill_ext(ext_ref, x, prev_ref, next_ref, t == 0, t == nt - 1, tb)
    sh = 0.5 * (ext_ref[7:7 + tb, :] + ext_ref[9:9 + tb, :])
    z = x + mu_ref[...] * (sh - x)
    r = z[:, 0:db]
    k = z[:, db:2 * db]
    v = z[:, 2 * db:3 * db]
    o = 3 * db
    zw = z[:, o:o + 2 * lw_]
    za = z[:, o + 2 * lw_:o + 2 * lw_ + 2 * la_]
    zg = z[:, o + 2 * lw_ + 2 * la_:]
    w = -_softplus(-(w0_ref[...] + _dot3(_split(jnp.tanh(zw)), _split(w2_ref[...])))) - 0.5
    lw = -jnp.exp(w)
    a = _sigmoid(a0_ref[...] + _dot3(_split(za), _split(a2_ref[...])))
    g = _dot_bf(_sigmoid(zg), g2_ref[...])
    e = e_ref[...]
    kkv = k * kk_ref[...]
    ss = _seg_sum(kkv * kkv, e)
    kkn = kkv / jnp.maximum(jnp.sqrt(ss), 1e-12)
    ka = ka_ref[...]
    kd0 = k * (1.0 + (a[:, :db] - 1.0) * ka)
    kd1 = k * (1.0 + (a[:, db:] - 1.0) * ka)
    bonus = _seg_sum(r * rk_ref[...] * (kd0 + kd1), e) * v
    r_out[...] = r
    v_out[...] = v
    kk_out[...] = kkn
    g_out[...] = g
    bonus_out[...] = bonus
    lw_out[0] = lw[:, :db]
    lw_out[1] = lw[:, db:]
    bt_out[0] = kkn * a[:, :db]
    bt_out[1] = kkn * a[:, db:]
    kd_out[0] = kd0
    kd_out[1] = kd1


def _rwkv_prep(pb, mu, w0c, w2c, a0c, a2c, g2, k_k, k_a, r_k, e_bf, bsz, seq):
    n, ncol = pb.shape
    db = k_k.shape[-1]
    nt = seq // TB
    r8 = TB // 8
    blk = lambda b, t: b * nt + t
    full = lambda shp: pl.BlockSpec(shp, lambda b, t: (0,) * len(shp))
    row = pl.BlockSpec((TB, db), lambda b, t: (blk(b, t), 0))
    row2 = pl.BlockSpec((2, TB, db), lambda b, t: (0, blk(b, t), 0))
    s1 = jax.ShapeDtypeStruct((n, db), F32)
    s2 = jax.ShapeDtypeStruct((2, n, db), F32)
    lw_ = w2c.shape[0] // 2
    la_ = a2c.shape[0] // 2
    return pl.pallas_call(
        functools.partial(_rwkv_prep_kernel, nt=nt, tb=TB, db=db, lw_=lw_, la_=la_),
        grid=(bsz, nt),
        in_specs=[pl.BlockSpec((TB, ncol), lambda b, t: (blk(b, t), 0)),
                  pl.BlockSpec((8, ncol), lambda b, t: (jnp.maximum(blk(b, t) * r8 - 1, 0), 0)),
                  pl.BlockSpec((8, ncol), lambda b, t: (jnp.minimum((blk(b, t) + 1) * r8, n // 8 - 1), 0)),
                  full((1, ncol)), full((1, 2 * db)), full(w2c.shape), full((1, 2 * db)), full(a2c.shape),
                  full(g2.shape), full((1, db)), full((1, db)), full((1, db)), full((db, db))],
        out_specs=[row, row, row, row, row, row2, row2, row2],
        out_shape=[s1, s1, s1, s1, s1, s2, s2, s2],
        scratch_shapes=[pltpu.VMEM((TB + 16, ncol), F32)],
        compiler_params=_cparams(("arbitrary", "arbitrary"), VMEM_BIG),
        name="rwkv_prep",
    )(pb, pb, pb, mu.reshape(1, ncol), w0c, w2c, a0c, a2c, g2, k_k.reshape(1, db), k_a.reshape(1, db),
      r_k.reshape(1, db), e_bf)


def _tri_masks(c, d):
    ti = lax.broadcasted_iota(jnp.int32, (c, c), 0)
    si = lax.broadcasted_iota(jnp.int32, (c, c), 1)
    if d == 0:
        return si < ti, si <= ti, ti == si
    return si > ti, si >= ti, ti == si


def _split(x):
    hi = x.astype(BF16)
    return hi, (x - hi.astype(F32)).astype(BF16)


def _cut(p, sl):
    return p[0][sl], p[1][sl]


def _dot3(a, b, ca=1, cb=0):
    m = a[0].shape[1 - ca]
    n = b[0].shape[1 - cb]
    if not (ca == 1 and cb == 1 and 2 * n == MXU_N):
        return _dot(a[0], b[0], ca, cb) + _dot(a[1], b[0], ca, cb) + _dot(a[0], b[1], ca, cb)
    q = _dot(jnp.concatenate(a, axis=0), jnp.concatenate(b, axis=0), ca, cb)
    return (q[:m, :n] + q[:m, n:]) + (q[m:, :n] + q[m:, n:])


def _staged(gens):
    results = [None] * len(gens)
    live = list(range(len(gens)))
    while live:
        still = []
        for i in live:
            try:
                out = next(gens[i])
            except StopIteration:
                continue
            if out is not None:
                results[i] = out
            still.append(i)
        live = still
    return results


def _rwkv_chunk(d, c, nh, hd, r, kk, v, lw, bt, kd, s_heads):
    strict, incl, diag = _tri_masks(c, d)
    cs = _mask_dot(incl.astype(BF16), lw)
    tot = cs[c - 1:c] if d == 0 else cs[0:1]
    einv = jnp.exp(-cs)
    eend = jnp.exp(tot - cs)
    gtot = jnp.exp(tot)
    a_t = _split(-kk * jnp.exp(cs - lw))
    r_t = _split(r * jnp.exp(cs))
    b_t = _split(bt * einv)
    k_t = _split(kd * einv)
    b_g = _split(bt * eend)
    k_g = _split(kd * eend)
    vs = _split(v)
    eye = diag.astype(F32)
    t2 = lax.broadcasted_iota(jnp.int32, (c, 2 * c), 0)
    s2 = lax.broadcasted_iota(jnp.int32, (c, 2 * c), 1)
    s2 = jnp.where(s2 >= c, s2 - c, s2)
    incl2 = (s2 <= t2) if d == 0 else (s2 >= t2)
    nsq = int(round(math.log2(SUB))) - 1
    ti = lax.broadcasted_iota(jnp.int32, (c, c), 0)
    si = lax.broadcasted_iota(jnp.int32, (c, c), 1)
    blk_masks = [ti // SUB == si // SUB]
    w = SUB
    while w < c:
        blk_masks.append((ti // (2 * w) == si // (2 * w)) & (ti // w != si // w))
        w *= 2
    def head(h):
        sl = (slice(None), slice(h * hd, (h + 1) * hd))
        s0 = s_heads[h]
        s0s = _split(s0)
        ar = tuple(jnp.concatenate([x[sl], y[sl]], axis=0) for x, y in zip(a_t, r_t))
        bk = tuple(jnp.concatenate([x[sl], y[sl]], axis=0) for x, y in zip(b_t, k_t))
        gm = _dot3(ar, bk, 1, 1)
        yield
        a_ab = jnp.where(strict, gm[:c, :c], 0.0)
        a_ak = jnp.where(strict, gm[:c, c:], 0.0)
        ars =_dot3(ar, s0s, 1, 1)
        yield
        vh = _cut(vs, sl)
        x = ars[:c] + _dot3(_split(a_ak), vh)
        yield
        nd = jnp.where(blk_masks[0], a_ab, 0.0)
        tinv = eye + nd
        p = _split(nd)
        for _ in range(nsq):
            p = _split(_dot3(p, p))
            yield
            tinv = tinv + _dot3(_split(tinv), p)
            yield
        for m in blk_masks[1:]:
            ts = _split(tinv)
            half = _split(_dot3(ts, _split(jnp.where(m, a_ab, 0.0))))
            yield
            tinv = tinv + _dot3(half, ts)
            yield
        u = _dot3(_split(tinv), _split(x))
        yield
        us = _split(u)
        uv = tuple(jnp.concatenate([x, y], axis=0) for x, y in zip(us, vh))
        y_out = ars[c:] + _dot(jnp.where(incl2, gm[c:], 0.0).astype(BF16), uv[0])
        yield
        bkg = tuple(jnp.concatenate([x[sl], y[sl]], axis=0) for x, y in zip(b_g, k_g))
        s_out = s0 * gtot[sl] + _dot3(uv, bkg, 0, 0)
        yield (y_out, s_out)

    return [head(h) for h in range(nh)]


def _rwkv_scan_kernel(r0_ref, kk0_ref, v0_ref, lw0_ref, bt0_ref, kd0_ref,
                      r1_ref, kk1_ref, v1_ref, lw1_ref, bt1_ref, kd1_ref, s0_ref,
                      y0_ref, y1_ref, sfin_ref, s_ref, *, c, nh, hd):
    @pl.when(pl.program_id(1) == 0)
    def _():
        s_ref[...] = s0_ref[...]

    s_in = [[s_ref[d, h] for h in range(nh)] for d in range(2)]
    ga = _rwkv_chunk(0, c, nh, hd, r0_ref[...], kk0_ref[...], v0_ref[...], lw0_ref[...], bt0_ref[...],
                     kd0_ref[...], s_in[0])
    gb = _rwkv_chunk(1, c, nh, hd, r1_ref[...], kk1_ref[...], v1_ref[...], lw1_ref[...], bt1_ref[...],
                     kd1_ref[...], s_in[1])
    res = _staged([g for pair in zip(ga, gb) for g in pair])
    ra, rb = res[0::2], res[1::2]
    y0_ref[...] = jnp.concatenate([y for y, _ in ra], axis=1)
    y1_ref[...] = jnp.concatenate([y for y, _ in rb], axis=1)
    for h in range(nh):
        s_ref[0, h] = ra[h][1]
        s_ref[1, h] = rb[h][1]
        sfin_ref[0, h] = ra[h][1]
        sfin_ref[1, h] = rb[h][1]


def _rwkv_scan(r, kkn, v, lw, bt, kd, st, e, bsz, seq):
    n, db = r.shape
    nh = db // HEAD_B
    nc = seq // CH
    fwd = lambda b, c: (b * nc + c, 0)
    bwd = lambda b, c: (b * nc + nc - 1 - c, 0)
    row0 = pl.BlockSpec((CH, db), fwd)
    row1 = pl.BlockSpec((CH, db), bwd)
    dir0 = pl.BlockSpec((None, CH, db), lambda b, c: (0,) + fwd(b, c))
    dir1 = pl.BlockSpec((None, CH, db), lambda b, c: (1,) + bwd(b, c))
    st_in = pl.BlockSpec((None, None, 2, nh, HEAD_B, HEAD_B), lambda b, c: (b, e, 0, 0, 0, 0))
    st_out = pl.BlockSpec((None, 2, nh, HEAD_B, HEAD_B), lambda b, c: (b, 0, 0, 0, 0))
    return pl.pallas_call(
        functools.partial(_rwkv_scan_kernel, c=CH, nh=nh, hd=HEAD_B),
        grid=(bsz, nc),
        in_specs=[row0, row0, row0, dir0, dir0, dir0, row1, row1, row1, dir1, dir1, dir1, st_in],
        out_specs=[row0, row1, st_out],
        out_shape=[jax.ShapeDtypeStruct((n, db), F32), jax.ShapeDtypeStruct((n, db), F32),
                   jax.ShapeDtypeStruct((bsz, 2, nh, HEAD_B, HEAD_B), F32)],
        scratch_shapes=[pltpu.VMEM((2, nh, HEAD_B, HEAD_B), F32)],
        compiler_params=_cparams(("arbitrary", "arbitrary")),
        name="rwkv_scan",
    )(r, kkn, v, lw, bt, kd, r, kkn, v, lw, bt, kd, st)


def _log_sigmoid(x):
    return -_softplus(-x)


def _split3(x):
    hi = x.astype(BF16)
    r1 = x - hi.astype(F32)
    mid = r1.astype(BF16)
    return hi, mid, (r1 - mid.astype(F32)).astype(BF16)


def _gla_chunk(d, c, nh, dk, dv, p_ref, wup, bgk, s_heads):
    _, incl, _ = _tri_masks(c, d)
    hk = nh * dk
    q = p_ref[:, 0:hk] * dk ** -0.5
    k = p_ref[:, hk:2 * hk]
    gk = _dot3(_split(p_ref[:, 2 * hk + nh * dv:]), _split(wup)) + bgk
    la = _log_sigmoid(gk) / GATE_NORM
    la3 = _split3(la)
    mask_bf = incl.astype(BF16)
    cs = _dot(mask_bf, la3[0]) + _dot(mask_bf, la3[1]) + _dot(mask_bf, la3[2])
    tot = cs[c - 1:c] if d == 0 else cs[0:1]
    q_t = (q * jnp.exp(cs)).astype(BF16)
    k_t = (k * jnp.exp(-cs)).astype(BF16)
    k_e = _split(k * jnp.exp(tot - cs))
    ones = jnp.ones((c, dv), BF16)

    def head(h):
        sk = (slice(None), slice(h * dk, (h + 1) * dk))
        vh = _split(p_ref[:, 2 * hk + h * dv:2 * hk + (h + 1) * dv])
        s0 = s_heads[h]
        qh = q_t[sk]
        att = jnp.where(incl, _dot(qh, k_t[sk], 1, 1), 0.0)
        yield
        inter = _dot(qh, s0.astype(BF16))
        yield
        o_out = _dot(att.astype(BF16), vh[0]) + inter
        yield
        dec = jnp.exp(_dot(la3[0][sk], ones, 0, 0) + _dot(la3[1][sk], ones, 0, 0) + _dot(la3[2][sk], ones, 0, 0))
        yield
        yield (o_out, s0 * dec + _dot3(_cut(k_e, sk), vh, 0, 0))

    return [head(h) for h in range(nh)]


def _gla_kernel(p0_ref, p1_ref, s0_ref, wup_ref, bgk_ref, o0_ref, o1_ref, sfin_ref, s_ref, *, c, nh, dk, dv):
    @pl.when(pl.program_id(1) == 0)
    def _():
        s_ref[...] = s0_ref[...]

    s_in = [[s_ref[d, h] for h in range(nh)] for d in range(2)]
    ga = _gla_chunk(0, c, nh, dk, dv, p0_ref, wup_ref[0], bgk_ref[0], s_in[0])
    gb = _gla_chunk(1, c, nh, dk, dv, p1_ref, wup_ref[1], bgk_ref[1], s_in[1])
    res = _staged([g for pair in zip(ga, gb) for g in pair])
    ra, rb = res[0::2], res[1::2]
    o0_ref[...] = jnp.concatenate([o for o, _ in ra], axis=1)
    o1_ref[...] = jnp.concatenate([o for o, _ in rb], axis=1)
    for h in range(nh):
        s_ref[0, h] = ra[h][1]
        s_ref[1, h] = rb[h][1]
        sfin_ref[0, h] = ra[h][1]
        sfin_ref[1, h] = rb[h][1]


def _gla(p, st, o_, wup, bgk, bsz, seq, grid_mode):
    nh, dk, dv = st.shape[3:]
    hv = nh * dv
    nc = seq // CH
    if grid_mode:
        assert seq // GRID_W == CH and nc == GRID_W
        ncol = p.shape[2] // GRID_W
        p_v = p
        blk = lambda w: (None, CH, w)
        fwd = lambda b, c: (b, 0, c)
        bwd = lambda b, c: (b, 0, nc - 1 - c)
        o_shape = jax.ShapeDtypeStruct((bsz, CH, GRID_W * hv), F32)
    else:
        ncol = p.shape[1]
        p_v = p.reshape(bsz * nc, CH, ncol)
        blk = lambda w: (None, CH, w)
        fwd = lambda b, c: (b * nc + c, 0, 0)
        bwd = lambda b, c: (b * nc + nc - 1 - c, 0, 0)
        o_shape = jax.ShapeDtypeStruct((bsz * nc, CH, hv), F32)
    st_in = pl.BlockSpec((None, None, 2, nh, dk, dv), lambda b, c: (b, o_, 0, 0, 0, 0))
    st_out = pl.BlockSpec((None, 2, nh, dk, dv), lambda b, c: (b, 0, 0, 0, 0))
    o0, o1, sfin = pl.pallas_call(
        functools.partial(_gla_kernel, c=CH, nh=nh, dk=dk, dv=dv),
        grid=(bsz, nc),
        in_specs=[pl.BlockSpec(blk(ncol), fwd), pl.BlockSpec(blk(ncol), bwd), st_in,
                  pl.BlockSpec(wup.shape, lambda b, c: (0, 0, 0)),
                  pl.BlockSpec(bgk.shape, lambda b, c: (0, 0, 0))],
        out_specs=[pl.BlockSpec(blk(hv), fwd), pl.BlockSpec(blk(hv), bwd), st_out],
        out_shape=[o_shape, o_shape, jax.ShapeDtypeStruct((bsz, 2, nh, dk, dv), F32)],
        scratch_shapes=[pltpu.VMEM((2, nh, dk, dv), F32)],
        compiler_params=_cparams(("arbitrary", "arbitrary"), VMEM_BIG),
        name="gla",
    )(p_v, p_v, st, wup, bgk)
    if not grid_mode:
        o0, o1 = o0.reshape(bsz * seq, hv), o1.reshape(bsz * seq, hv)
    return o0, o1, sfin


def _router_tail(x1, mod, g2n_ref, rw_ref, x_out, h2_out, aff_out):
    x_out[...] = x1
    h2 = _rms(x1, g2n_ref[...]) * (1.0 + mod[4:5]) + mod[3:4]
    h2_out[...] = h2.astype(BF16)
    logits = _dot3(_split(rw_ref[...]), _split(h2), 1, 1)
    m = jnp.max(logits, axis=0, keepdims=True)
    ex = jnp.exp(logits - m)
    aff_out[...] = ex / jnp.sum(ex, axis=0, keepdims=True)


def _out_even_kernel(x_ref, mod_ref, ha0_ref, ha1_ref, ga_ref, y0_ref, y1_ref, g_ref, bonus_ref, lnw_ref, lnb_ref,
                     e_ref, wo_ref, g2n_ref, rw_ref, x_out, h2_out, aff_out, *, da, hd):
    mod = mod_ref[...]
    ya = (ha0_ref[...] + ha1_ref[...]) * _gelu_tanh(ga_ref[...])
    y = y0_ref[...] + y1_ref[...]
    e = e_ref[...]
    mean = _seg_sum(y, e) * (1.0 / hd)
    yc = y - mean
    var = _seg_sum(yc * yc, e) * (1.0 / hd)
    yn = yc * lax.rsqrt(var + GN_EPS_B) * lnw_ref[...] + lnb_ref[...]
    yb = (yn + bonus_ref[...]) * g_ref[...]
    out = _dot_bf(ya, wo_ref[0:da, :]) + _dot_bf(yb, wo_ref[da:, :])
    _router_tail(x_ref[...] + mod[2:3] * out, mod, g2n_ref, rw_ref, x_out, h2_out, aff_out)


def _out_odd_kernel(x_ref, mod_ref, o0_ref, o1_ref, g_ref, gw_ref, wo_ref, g2n_ref, rw_ref,
                    x_out, h2_out, aff_out, *scratch, nh, dv, grid_rows):
    mod = mod_ref[...]
    gw = gw_ref[...]
    if grid_rows:
        hv = nh * dv
        o_scr = scratch[0]
        for c in range(GRID_W):
            for j in range(hv // LANES):
                lanes = slice(c * hv + j * LANES, c * hv + (j + 1) * LANES)
                o_scr[j, pl.ds(c, grid_rows, stride=GRID_W), :] = o0_ref[:, lanes] + o1_ref[:, lanes]
        o_all = jnp.concatenate([o_scr[j] for j in range(hv // LANES)], axis=1)
    else:
        o_all = o0_ref[...] + o1_ref[...]
    parts = []
    for h in range(nh):
        o = o_all[:, h * dv:(h + 1) * dv]
        parts.append(o * lax.rsqrt(jnp.mean(o * o, axis=-1, keepdims=True) + EPS) * gw)
    y = jnp.concatenate(parts, axis=1) * _silu(g_ref[...])
    out = _dot_bf(y, wo_ref[...])
    _router_tail(x_ref[...] + mod[2:3] * out, mod, g2n_ref, rw_ref, x_out, h2_out, aff_out)


def _out_common(kern, x, modr, mod_idx, extra_args, extra_specs, norm2_g, router_wt, bsz, seq, name,
                tb=TB, scratch=()):
    n, d = x.shape
    ne = router_wt.shape[0]
    nt = seq // tb
    row = lambda b, t: (b * nt + t, 0)
    in_specs = ([pl.BlockSpec((tb, d), row), pl.BlockSpec((None, 6, d), lambda b, t: (mod_idx(b), 0, 0))]
                + extra_specs
                + [pl.BlockSpec((1, d), lambda b, t: (0, 0)), pl.BlockSpec((ne, d), lambda b, t: (0, 0))])
    return pl.pallas_call(
        kern, grid=(bsz, nt), in_specs=in_specs,
        out_specs=[pl.BlockSpec((tb, d), row), pl.BlockSpec((tb, d), row),
                   pl.BlockSpec((ne, tb), lambda b, t: (0, b * nt + t))],
        out_shape=[jax.ShapeDtypeStruct((n, d), F32), jax.ShapeDtypeStruct((n, d), BF16),
                   jax.ShapeDtypeStruct((ne, n), F32)],
        scratch_shapes=list(scratch),
        compiler_params=_cparams(("arbitrary", "arbitrary"), VMEM_BIG),
        name=name,
    )(x, modr, *extra_args, norm2_g.reshape(1, d), router_wt)


def _select_kernel(aff_ref, slot_ref, cnt_ref, *, cap, ntb, tokb):
    aff = aff_ref[...]
    ne = aff.shape[0]
    big = jnp.float32(4.0)

    def body(carry):
        lo, hi, _, _ = carry
        inr = (aff > lo) & (aff <= hi)
        mn = jnp.min(jnp.where(inr, aff, big), axis=1, keepdims=True)
        mx = jnp.max(jnp.where(inr, aff, -big), axis=1, keepdims=True)
        done = mn >= mx
        piv = mn + (mx - mn) * 0.5
        piv = jnp.where(piv >= mx, mn, piv)
        ge = jnp.sum((aff > piv).astype(jnp.int32), axis=1, keepdims=True) >= cap
        lo = jnp.where(done | ~ge, lo, piv)
        hi = jnp.where(done | ge, hi, piv)
        return lo, hi, mx, jnp.sum((~done).astype(jnp.int32))

    init = (jnp.full((ne, 1), -1.0, F32), jnp.full((ne, 1), big, F32), jnp.zeros((ne, 1), F32), jnp.int32(1))
    _, _, thr, _ = lax.while_loop(lambda carry: carry[3] > 0, body, init)
    gt = aff > thr
    eq = aff == thr
    need = cap - jnp.sum(gt.astype(jnp.int32), axis=1, keepdims=True)
    si = lax.broadcasted_iota(jnp.int32, (tokb, tokb), 0)
    ti = lax.broadcasted_iota(jnp.int32, (tokb, tokb), 1)
    ustrict = (si < ti).astype(BF16)

    def excl_count(mask_blk, carry):
        m = mask_blk.astype(BF16)
        pre = _dot(m, ustrict).astype(jnp.int32) + carry
        return pre, carry + jnp.sum(mask_blk.astype(jnp.int32), axis=1, keepdims=True)

    ceq = jnp.zeros((ne, 1), jnp.int32)
    csel = jnp.zeros((ne, 1), jnp.int32)
    lane = lax.broadcasted_iota(jnp.int32, (ne, 128), 1)
    cnt_acc = jnp.zeros((ne, 128), jnp.int32)
    for j in range(ntb):
        sl = slice(j * tokb, (j + 1) * tokb)
        eq_rank, ceq = excl_count(eq[:, sl], ceq)
        sel = gt[:, sl] | (eq[:, sl] & (eq_rank < need))
        cnt_acc = jnp.where(lane == j, csel, cnt_acc)
        rank, csel = excl_count(sel, csel)
        slot_ref[:, sl] = jnp.where(sel, rank, -1)
    cnt_acc = jnp.where(lane == ntb, csel, cnt_acc)
    cnt_ref[...] = cnt_acc


def _select(aff_t, cap):
    ne, n = aff_t.shape
    ntb = n // TOK_B
    assert ntb < 128
    return pl.pallas_call(
        functools.partial(_select_kernel, cap=cap, ntb=ntb, tokb=TOK_B),
        grid=(1,),
        in_specs=[pl.BlockSpec((ne, n), lambda i: (0, 0))],
        out_specs=[pl.BlockSpec((ne, n), lambda i: (0, 0)), pl.BlockSpec((ne, 128), lambda i: (0, 0))],
        out_shape=[jax.ShapeDtypeStruct((ne, n), jnp.int32), jax.ShapeDtypeStruct((ne, 128), jnp.int32)],
        compiler_params=_cparams(("arbitrary",)),
        name="select",
    )(aff_t)


def _windows(cnt_ref, tb, ntb, ne, cap, w, rnd):
    out = []
    for e in range(ne):
        first = (cnt_ref[e * (ntb + 1) + tb] // SLOT_ALIGN) * SLOT_ALIGN + rnd * w
        out.append((pl.multiple_of(jnp.minimum(first, cap - w), SLOT_ALIGN), first))
    return out


def _onehot_rows(slot_ref, e, base, first, w, tokb):
    sidx = lax.broadcasted_iota(jnp.int32, (w, tokb), 0) + base
    return (slot_ref[e:e + 1, :] == sidx) & (sidx >= first)


def _gather_kernel(cnt_ref, nr_ref, slot_ref, h2_ref, xe_ref, lhs_ref, *, ntb, ne, cap, w, tokb):
    tb = pl.program_id(0)

    @pl.when(tb == 0)
    def _():
        xe_ref[...] = jnp.zeros_like(xe_ref)

    def body(rnd, carry):
        wins = _windows(cnt_ref, tb, ntb, ne, cap, w, rnd)
        for e, (base, first) in enumerate(wins):
            lhs_ref[e * w:(e + 1) * w, :] = jnp.where(_onehot_rows(slot_ref, e, base, first, w, tokb),
                                                      1.0, 0.0).astype(BF16)
        rows = _dot(lhs_ref[...], h2_ref[...])
        for e, (base, _) in enumerate(wins):
            xe_ref[e, pl.ds(base, w), :] += rows[e * w:(e + 1) * w].astype(BF16)
        return carry

    lax.fori_loop(0, nr_ref[tb], body, 0)


def _gather(cnt, nr, slot_t, h2, cap):
    ntb, ne, tokb = slot_t.shape
    n, d = h2.shape
    return pl.pallas_call(
        functools.partial(_gather_kernel, ntb=ntb, ne=ne, cap=cap, w=SLOT_W, tokb=tokb),
        grid_spec=pltpu.PrefetchScalarGridSpec(
            num_scalar_prefetch=2, grid=(ntb,),
            in_specs=[pl.BlockSpec((None, ne, tokb), lambda t, cnt, nr: (t, 0, 0)),
                      pl.BlockSpec((tokb, d), lambda t, cnt, nr: (t, 0))],
            out_specs=pl.BlockSpec((ne, cap, d), lambda t, cnt, nr: (0, 0, 0), pipeline_mode=pl.Buffered(1)),
            scratch_shapes=[pltpu.VMEM((ne * SLOT_W, tokb), BF16)]),
        out_shape=jax.ShapeDtypeStruct((ne, cap, d), BF16),
        compiler_params=_cparams(("arbitrary",), VMEM_BIG),
        name="moe_gather",
    )(cnt, nr, slot_t, h2)


def _ffn_kernel(xe_ref, wg_ref, wu_ref, wd_ref, ye_ref, acc_ref):
    f = pl.program_id(1)
    xe = xe_ref[...]
    hg = _dot(xe, wg_ref[...].astype(BF16))
    hu = _dot(xe, wu_ref[...].astype(BF16))
    hid = (_silu(hg) * hu).astype(BF16)
    part = _dot(hid, wd_ref[...].astype(BF16))

    @pl.when(f == 0)
    def _():
        acc_ref[...] = part

    @pl.when(f > 0)
    def _():
        acc_ref[...] += part

    @pl.when(f == pl.num_programs(1) - 1)
    def _():
        ye_ref[...] = acc_ref[...].astype(BF16)


def _ffn(xe, w_gate, w_up, w_down, l):
    ne, cap, d = xe.shape
    dff = w_gate.shape[-1]
    nf = dff // FF_B
    return pl.pallas_call(
        _ffn_kernel,
        grid=(ne, nf),
        in_specs=[pl.BlockSpec((None, cap, d), lambda e, f: (e, 0, 0)),
                  pl.BlockSpec((None, None, d, FF_B), lambda e, f: (l, e, 0, f)),
                  pl.BlockSpec((None, None, d, FF_B), lambda e, f: (l, e, 0, f)),
                  pl.BlockSpec((None, None, FF_B, d), lambda e, f: (l, e, f, 0))],
        out_specs=pl.BlockSpec((None, cap, d), lambda e, f: (e, 0, 0)),
        out_shape=jax.ShapeDtypeStruct((ne, cap, d), BF16),
        scratch_shapes=[pltpu.VMEM((cap, d), F32)],
        compiler_params=_cparams(("arbitrary", "arbitrary"), VMEM_BIG),
        name="moe_ffn",
    )(xe, w_gate, w_up, w_down)


def _scatter_kernel(cnt_ref, nr_ref, slot_ref, gate_ref, ye_ref, x_ref, mod_ref, fg_ref, out_ref, g_ref, ywin_ref,
                    *, ntb, ne, cap, w, tokb, final):
    tb = pl.program_id(0)
    out_ref[...] = jnp.zeros_like(out_ref)

    def body(rnd, carry):
        wins = _windows(cnt_ref, tb, ntb, ne, cap, w, rnd)
        for e, (base, first) in enumerate(wins):
            g_ref[e * w:(e + 1) * w, :] = jnp.where(_onehot_rows(slot_ref, e, base, first, w, tokb),
                                                    gate_ref[e:e + 1, :], 0.0).astype(BF16)
            ywin_ref[e * w:(e + 1) * w, :] = ye_ref[e, pl.ds(base, w), :]
        out_ref[...] += _dot(g_ref[...], ywin_ref[...], 0, 0)
        return carry

    lax.fori_loop(0, nr_ref[tb], body, 0)
    x = x_ref[...] + mod_ref[...][5:6] * out_ref[...]
    out_ref[...] = _rms(x, fg_ref[...]) if final else x


def _scatter(cnt, nr, slot_t, gate_t, ye, x, modr, mod_tok, final_gain, final):
    ne, cap, d = ye.shape
    ntb, _, tokb = slot_t.shape
    n = ntb * tokb
    tok = pl.BlockSpec((None, ne, tokb), lambda t, cnt, nr: (t, 0, 0))
    row = pl.BlockSpec((tokb, d), lambda t, cnt, nr: (t, 0))
    return pl.pallas_call(
        functools.partial(_scatter_kernel, ntb=ntb, ne=ne, cap=cap, w=SLOT_W, tokb=tokb, final=final),
        grid_spec=pltpu.PrefetchScalarGridSpec(
            num_scalar_prefetch=2, grid=(ntb,),
            in_specs=[tok, tok,
                      pl.BlockSpec((ne, cap, d), lambda t, cnt, nr: (0, 0, 0), pipeline_mode=pl.Buffered(1)),
                      row,
                      pl.BlockSpec((None, 6, d), lambda t, cnt, nr: (mod_tok(t), 0, 0)),
                      pl.BlockSpec((1, d), lambda t, cnt, nr: (0, 0))],
            out_specs=row,
            scratch_shapes=[pltpu.VMEM((ne * SLOT_W, tokb), BF16), pltpu.VMEM((ne * SLOT_W, d), BF16)]),
        out_shape=jax.ShapeDtypeStruct((n, d), F32),
        compiler_params=_cparams(("arbitrary",), VMEM_BIG),
        name="moe_scatter",
    )(cnt, nr, slot_t, gate_t, ye, x, modr, final_gain.reshape(1, d))


def _moe(h2, aff_t, w_gate, w_up, w_down, l, x, modr, mod_tok, final_gain, final):
    n, d = h2.shape
    ne = aff_t.shape[0]
    cap = max(1, CAP_FACTOR * n // ne)
    assert cap % SLOT_W == 0 and n % TOK_B == 0
    ntb = n // TOK_B
    slot, cnt = _select(aff_t, cap)
    cnt = cnt[:, :ntb + 1]
    span = cnt[:, 1:] - (cnt[:, :-1] // SLOT_ALIGN) * SLOT_ALIGN
    nr = jnp.max(jnp.where(cnt[:, 1:] > cnt[:, :-1], (span + SLOT_W - 1) // SLOT_W, 0), axis=0).astype(jnp.int32)
    slot_t = slot.reshape(ne, ntb, TOK_B).transpose(1, 0, 2)
    gate_t = aff_t.reshape(ne, ntb, TOK_B).transpose(1, 0, 2)
    xe = _gather(cnt.reshape(-1), nr, slot_t, h2, cap)
    ye = _ffn(xe, w_gate, w_up, w_down, l)
    return _scatter(cnt.reshape(-1), nr, slot_t, gate_t, ye, x, modr, mod_tok, final_gain, final)


def _block_diag(w):
    h, n, _ = w.shape
    eye = jnp.eye(h, dtype=w.dtype)
    return (eye[:, None, :, None] * w[:, :, None, :]).reshape(h * n, h * n)


def _lora_cat(w2):
    _, l, c = w2.shape
    z = jnp.zeros((l, c), w2.dtype)
    return jnp.concatenate([jnp.concatenate([w2[0], z], 1), jnp.concatenate([z, w2[1]], 1)], 0)


def _trunk(x, bsz, seq, cond_base, per_batch, st_rglru, st_rwkv, st_gla, grid_mode, modr, P):
    d_model = x.shape[1]
    depth = P['norm1_g'].shape[0]
    new_rglru, new_rwkv, new_gla = [], [], []
    assert not per_batch or seq % TOK_B == 0
    for l in range(depth):
        mod_idx = (lambda b, l=l: l * 8 + cond_base + b) if per_batch else (lambda b, l=l: l * 8 + cond_base)
        mod_tok = (lambda t, f=mod_idx: f(t // (seq // TOK_B))) if per_batch else (lambda t, f=mod_idx: f(0))
        router_wt = P['router_w'][l].T
        if l % 2 == 0:
            e = l // 2
            da = P['a_conv_w'].shape[-1]
            db = P['b_k_k'].shape[-1]
            w_in = P['ev_w_in'][e].astype(BF16)
            nb_cols = w_in.shape[1] - 2 * da
            pa, pb = _inproj(x, modr, mod_idx, P['norm1_g'][l], w_in, (2 * da, nb_cols), bsz, seq)
            ha, sa = [], []
            for d in range(2):
                wg = jnp.concatenate([_block_diag(P['a_w_rg'][e, d]), _block_diag(P['a_w_ig'][e, d])], 1).astype(BF16)
                bg = jnp.concatenate([P['a_b_rg'][e, d], P['a_b_ig'][e, d]])
                h_d, hf = _rglru(pa, P['a_conv_w'][e], P['a_conv_b'][e], wg, bg, P['a_lam'][e, d],
                                 st_rglru[:, e, d], d, bsz, seq)
                ha.append(h_d)
                sa.append(hf.reshape(bsz, da))
            new_rglru.append(jnp.stack(sa, 1))
            e_bf = _block_diag(jnp.ones((db // HEAD_B, HEAD_B, HEAD_B), BF16))
            r, v, kkn, g, bonus, lw, bt, kd = _rwkv_prep(
                pb, P['b_mu'][e], P['b_w0'][e].reshape(1, 2 * db), _lora_cat(P['b_w2'][e]),
                P['b_a0'][e].reshape(1, 2 * db), _lora_cat(P['b_a2'][e]), P['b_g2'][e].astype(BF16),
                P['b_k_k'][e], P['b_k_a'][e], P['b_r_k'][e], e_bf, bsz, seq)
            y0, y1, sfin = _rwkv_scan(r, kkn, v, lw, bt, kd, st_rwkv, e, bsz, seq)
            ys = [y0, y1]
            new_rwkv.append(sfin)
            nt = seq // TB
            rowh = pl.BlockSpec((TB, da), lambda b, t: (b * nt + t, 0))
            rowg = pl.BlockSpec((TB, da), lambda b, t: (b * nt + t, 1))
            vec = pl.BlockSpec((1, db), lambda b, t: (0, 0))
            x, h2, aff_t = _out_common(
                functools.partial(_out_even_kernel, da=da, hd=HEAD_B), x, modr, mod_idx,
                [ha[0], ha[1], pa, ys[0], ys[1], g, bonus, P['b_ln_w'][e].reshape(1, db), P['b_ln_b'][e].reshape(1, db),
                 e_bf, P['ev_w_out'][e].astype(BF16)],
                [rowh, rowh, rowg, rowh, rowh, rowh, rowh, vec, vec,
                 pl.BlockSpec((db, db), lambda b, t: (0, 0)),
                 pl.BlockSpec((da + db, d_model), lambda b, t: (0, 0))],
                P['norm2_g'][l], router_wt, bsz, seq, "out_even")
        else:
            o_ = l // 2
            w_in = P['od_w_in'][o_]
            nh, dk, dv = st_gla.shape[3:]
            hv = nh * dv
            qkv = 2 * nh * dk + hv
            gin = -(2 * GATE_RANK) % 128 + 2 * GATE_RANK
            w_in = jnp.concatenate([w_in[:, :qkv], w_in[:, qkv + hv:],
                                    jnp.zeros((d_model, gin - 2 * GATE_RANK), w_in.dtype),
                                    w_in[:, qkv:qkv + hv]], axis=1).astype(BF16)
            p, g = _inproj(x, modr, mod_idx, P['norm1_g'][l], w_in, (qkv + gin, hv), bsz, seq, grid_view=grid_mode)
            wup = jnp.stack([jnp.zeros((gin, nh * dk), F32).at[d * GATE_RANK:(d + 1) * GATE_RANK]
                             .set(P['c_w_gk_up'][o_, d]) for d in range(2)])
            o0, o1, sfin = _gla(p, st_gla, o_, wup, P['c_b_gk'][o_].reshape(2, 1, nh * dk), bsz, seq, grid_mode)
            new_gla.append(sfin)
            tb = SUBLANES * GRID_W if grid_mode else TB
            nt = seq // tb
            if grid_mode:
                rowo = pl.BlockSpec((None, SUBLANES, GRID_W * hv), lambda b, t: (b, t, 0))
                scratch = [pltpu.VMEM((hv // LANES, tb, LANES), F32)]
            else:
                rowo = pl.BlockSpec((tb, hv), lambda b, t: (b * nt + t, 0))
                scratch = []
            rowg = pl.BlockSpec((tb, hv), lambda b, t: (b * nt + t, 0))
            x, h2, aff_t = _out_common(
                functools.partial(_out_odd_kernel, nh=nh, dv=dv, grid_rows=SUBLANES if grid_mode else 0),
                x, modr, mod_idx,
                [o0, o1, g, P['c_gnorm_w'][o_].reshape(1, dv), P['od_w_out'][o_].astype(BF16)],
                [rowo, rowo, rowg, pl.BlockSpec((1, dv), lambda b, t: (0, 0)),
                 pl.BlockSpec((hv, d_model), lambda b, t: (0, 0))],
                P['norm2_g'][l], router_wt, bsz, seq, "out_odd", tb=tb, scratch=scratch)
        x = _moe(h2, aff_t, P['moe_w_gate'], P['moe_w_up'], P['moe_w_down'], l, x, modr, mod_tok,
                 P['final_norm_g'], l == depth - 1)
    return x, jnp.stack(new_rglru, 1), jnp.stack(new_rwkv, 1), jnp.stack(new_gla, 1)


def kernel(x_prompt, x_sample, state_rglru, state_rwkv, state_gla, c, c_ctx, norm1_g, norm2_g, ada_w, ada_b, router_w, moe_w_gate, moe_w_up, moe_w_down, ev_w_in, ev_w_out, a_conv_w, a_conv_b, a_w_rg, a_b_rg, a_w_ig, a_b_ig, a_lam, b_mu, b_w0, b_w2, b_a0, b_a2, b_g2, b_k_k, b_k_a, b_r_k, b_ln_w, b_ln_b, od_w_in, od_w_out, c_w_gk_up, c_b_gk, c_gnorm_w, final_norm_g):
    P = dict(norm1_g=norm1_g, norm2_g=norm2_g, router_w=router_w,
             moe_w_gate=moe_w_gate, moe_w_up=moe_w_up, moe_w_down=moe_w_down,
             ev_w_in=ev_w_in, ev_w_out=ev_w_out, a_conv_w=a_conv_w, a_conv_b=a_conv_b,
             a_w_rg=a_w_rg, a_b_rg=a_b_rg, a_w_ig=a_w_ig, a_b_ig=a_b_ig, a_lam=a_lam,
             b_mu=b_mu, b_w0=b_w0, b_w2=b_w2, b_a0=b_a0, b_a2=b_a2, b_g2=b_g2, b_k_k=b_k_k,
             b_k_a=b_k_a, b_r_k=b_r_k, b_ln_w=b_ln_w, b_ln_b=b_ln_b,
             od_w_in=od_w_in, od_w_out=od_w_out, c_w_gk_up=c_w_gk_up, c_b_gk=c_b_gk,
             c_gnorm_w=c_gnorm_w, final_norm_g=final_norm_g)
    bp, tp, d_model = x_prompt.shape
    bs, ts, _ = x_sample.shape
    assert bs + 1 <= 8 and tp % TB == 0 and ts % TB == 0
    depth = ada_w.shape[0]
    cv8 = jnp.zeros((8, d_model), F32).at[0].set(c_ctx).at[1:1 + bs].set(c)
    modr = _modulation(cv8, ada_w, ada_b).reshape(depth * 8, 6, d_model)
    dt = x_prompt.dtype
    z_rglru = jnp.zeros((bp,) + state_rglru.shape[1:], dt)
    z_rwkv = jnp.zeros((bp,) + state_rwkv.shape[1:], dt)
    z_gla = jnp.zeros((bp,) + state_gla.shape[1:], dt)
    y_prompt, n_rglru, n_rwkv, n_gla = _trunk(x_prompt.reshape(bp * tp, d_model), bp, tp, 0, False,
                                              z_rglru, z_rwkv, z_gla, False, modr, P)
    y_sample, _, _, _ = _trunk(x_sample.reshape(bs * ts, d_model), bs, ts, 1, True,
                               state_rglru, state_rwkv, state_gla, True, modr, P)
    return (y_prompt.reshape(bp, tp, d_model), y_sample.reshape(bs, ts, d_model), n_rglru, n_rwkv, n_gla)
```

```python
import functools
import math

import numpy as np
import jax
import jax.numpy as jnp
from jax import lax
from jax.experimental import pallas as pl
from jax.experimental.pallas import tpu as pltpu

F32 = jnp.float32
BF16 = jnp.bfloat16
HI = lax.Precision.HIGHEST

GRID_W = 64
EPS = 1e-6
C_RGLRU = 8.0
GN_EPS_B = 64e-5
GATE_NORM = 16.0
CAP_FACTOR = 2
HEAD_B = 64
H_A = 8
H_C = 4
GATE_RANK = 16

SUBLANES = 8
LANES = 128
MXU_N = 256
PITCH_PAD = 4
TB = 256
CH = 64
SUB = 16
SLOT_W = 128
SLOT_ALIGN = 16
TOK_B = 512
FF_B = 512
VMEM_BIG = 56 * 1024 * 1024


def _dot(a, b, ca=1, cb=0, prec=None):
    return lax.dot_general(a, b, (((ca,), (cb,)), ((), ())), precision=prec,
                           preferred_element_type=F32)


def _dot_bf(a, b, ca=1, cb=0):
    return _dot(a.astype(BF16), b.astype(BF16), ca, cb)


def _dot_hi(a, b, ca=1, cb=0):
    return _dot(a.astype(F32), b.astype(F32), ca, cb, HI)


def _seg_sum(x, e_bf):
    hi = x.astype(BF16)
    lo = (x - hi.astype(F32)).astype(BF16)
    return _dot(hi, e_bf) + _dot(lo, e_bf)


def _mask_dot(mask_bf, x, cm=1, cx=0):
    hi = x.astype(BF16)
    r1 = x - hi.astype(F32)
    mid = r1.astype(BF16)
    lo = (r1 - mid.astype(F32)).astype(BF16)
    return _dot(mask_bf, hi, cm, cx) + _dot(mask_bf, mid, cm, cx) + _dot(mask_bf, lo, cm, cx)


def _sigmoid(x):
    return 1.0 / (1.0 + jnp.exp(-x))


def _softplus(x):
    return jnp.maximum(x, 0.0) + jnp.log1p(jnp.exp(-jnp.abs(x)))


def _silu(x):
    return x * _sigmoid(x)


def _gelu_tanh(x):
    return 0.5 * x * (1.0 + jnp.tanh(math.sqrt(2.0 / math.pi) * (x + 0.044715 * (x * x * x))))


def _rms(x, g):
    return x * lax.rsqrt(jnp.mean(x * x, axis=-1, keepdims=True) + EPS) * g


def _cparams(sem, vmem=None):
    return pltpu.CompilerParams(dimension_semantics=sem, vmem_limit_bytes=vmem)


def _mod_kernel(c_ref, w_ref, b_ref, o_ref):
    o_ref[...] = _dot_hi(_silu(c_ref[...]), w_ref[...]) + b_ref[...]


def _modulation(cv8, ada_w, ada_b):
    depth, d, n6 = ada_w.shape
    nb = 1536
    return pl.pallas_call(
        _mod_kernel,
        grid=(depth, n6 // nb),
        in_specs=[pl.BlockSpec((8, d), lambda l, j: (0, 0)),
                  pl.BlockSpec((None, d, nb), lambda l, j: (l, 0, j)),
                  pl.BlockSpec((None, 1, nb), lambda l, j: (l, 0, j))],
        out_specs=pl.BlockSpec((None, 8, nb), lambda l, j: (l, 0, j)),
        out_shape=jax.ShapeDtypeStruct((depth, 8, n6), F32),
        compiler_params=_cparams(("arbitrary", "arbitrary")),
        name="modulation",
    )(cv8, ada_w, ada_b.reshape(depth, 1, n6))


def _project(x, mod, g_ref, w_ref, outs, splits, grid_rows):
    h = _rms(x, g_ref[...]) * (1.0 + mod[1:2]) + mod[0:1]
    p = _dot_bf(h, w_ref[...])
    lo = 0
    if grid_rows:
        n0 = splits[0]
        scr = outs[-1]
        pitch = GRID_W + PITCH_PAD
        for j in range(n0 // LANES):
            for r in range(grid_rows):
                scr[j, r * pitch:r * pitch + GRID_W, :] = p[r * GRID_W:(r + 1) * GRID_W, j * LANES:(j + 1) * LANES]
        for c in range(GRID_W):
            for j in range(n0 // LANES):
                outs[0][:, c * n0 + j * LANES:c * n0 + (j + 1) * LANES] = scr[j, pl.ds(c, grid_rows, stride=pitch), :]
        lo, outs, splits = n0, outs[1:-1], splits[1:]
    for o_ref, n in zip(outs, splits):
        o_ref[...] = p[:, lo:lo + n]
        lo += n


def _inproj_kernel(x_ref, mod_ref, g_ref, w_ref, *outs, splits, grid_rows):
    _project(x_ref[...], mod_ref[...], g_ref, w_ref, outs, splits, grid_rows)


def _tok_block(grid_mode):
    return SUBLANES * GRID_W if grid_mode else TOK_B


def _inproj(x, modr, mod_of, gain, w_bf, splits, bsz, seq, grid_view=False):
    n, d = x.shape
    tb = _tok_block(grid_view)
    assert n % tb == 0
    nt = seq // tb
    ncols = w_bf.shape[1]
    mod_blk = mod_of(tb)
    row = lambda i: (i, 0)
    in_specs = [pl.BlockSpec((tb, d), row),
                pl.BlockSpec((None, 6, d), lambda i: (mod_blk(i), 0, 0)),
                pl.BlockSpec((1, d), lambda i: (0, 0)),
                pl.BlockSpec((d, ncols), lambda i: (0, 0), pipeline_mode=pl.Buffered(1))]
    args = [x, modr, gain.reshape(1, d), w_bf]
    out_shape = [jax.ShapeDtypeStruct((n, c), F32) for c in splits]
    out_specs = [pl.BlockSpec((tb, c), row) for c in splits]
    scratch = []
    grid_rows = 0
    if grid_view:
        grid_rows = tb // GRID_W
        out_shape[0] = jax.ShapeDtypeStruct((bsz, seq // GRID_W, GRID_W * splits[0]), F32)
        out_specs[0] = pl.BlockSpec((None, grid_rows, GRID_W * splits[0]), lambda i: (i // nt, i % nt, 0))
        scratch = [pltpu.VMEM((splits[0] // LANES, grid_rows * (GRID_W + PITCH_PAD), LANES), F32)]
    return pl.pallas_call(
        functools.partial(_inproj_kernel, splits=splits, grid_rows=grid_rows),
        grid=(n // tb,), in_specs=in_specs, out_specs=out_specs, out_shape=out_shape,
        scratch_shapes=scratch,
        compiler_params=_cparams(("arbitrary",), VMEM_BIG),
        name="inproj",
    )(*args)


def _fill_ext(ext_ref, x, prev_ref, next_ref, first, last, tb):
    ext_ref[0:8, :] = jnp.where(first, 0.0, prev_ref[...])
    ext_ref[8:8 + tb, :] = x
    ext_ref[8 + tb:16 + tb, :] = jnp.where(last, 0.0, next_ref[...])


def _rglru_kernel(xa_ref, prev_ref, next_ref, cw_ref, cb_ref, wg_ref, bg_ref, lam_ref, h0_ref,
                  h_ref, hfin_ref, ext_ref, carry_ref, *, d, nt, tb):
    i = pl.program_id(1)
    t = i if d == 0 else nt - 1 - i
    first = t == 0
    last = t == nt - 1
    x = xa_ref[...]
    da = x.shape[1]
    _fill_ext(ext_ref, x, prev_ref, next_ref, first, last, tb)
    cw = cw_ref[...]
    u = (cw[0:1] * ext_ref[7:7 + tb, :] + cw[1:2] * x + cw[2:3] * ext_ref[9:9 + tb, :]
         + cw[3:4] * ext_ref[10:10 + tb, :] + cb_ref[...])
    gts = _dot_bf(u, wg_ref[...]) + bg_ref[...]
    gate_r = _sigmoid(gts[:, :da])
    gate_i = _sigmoid(gts[:, da:])
    log_a = -C_RGLRU * gate_r * _softplus(-lam_ref[...])
    a = jnp.exp(log_a)
    xin = jnp.sqrt(1.0 - jnp.exp(2.0 * log_a)) * gate_i * u

    @pl.when(i == 0)
    def _():
        carry_ref[...] = h0_ref[...]

    rows = lax.broadcasted_iota(jnp.int32, (tb, da), 0)
    s = 1
    while s < tb:
        if d == 0:
            a_sh = pltpu.roll(a, s, 0)
            x_sh = pltpu.roll(xin, s, 0)
            valid = rows >= s
        else:
            a_sh = pltpu.roll(a, tb - s, 0)
            x_sh = pltpu.roll(xin, tb - s, 0)
            valid = rows < tb - s
        xin = jnp.where(valid, a * x_sh + xin, xin)
        a = jnp.where(valid, a * a_sh, a)
        s *= 2
    h = xin + a * carry_ref[...]
    h_ref[...] = h
    newc = h[tb - 1:tb, :] if d == 0 else h[0:1, :]
    carry_ref[...] = newc
    hfin_ref[...] = newc


def _rglru(pa, conv_w, conv_b, wg_d, bg_d, lam_d, h0_d, d, bsz, seq):
    n = pa.shape[0]
    da = conv_w.shape[1]
    nt = seq // TB
    r8 = TB // 8
    tt = (lambda t: t) if d == 0 else (lambda t: nt - 1 - t)
    blk = lambda b, t: b * nt + tt(t)
    return pl.pallas_call(
        functools.partial(_rglru_kernel, d=d, nt=nt, tb=TB),
        grid=(bsz, nt),
        in_specs=[pl.BlockSpec((TB, da), lambda b, t: (blk(b, t), 0)),
                  pl.BlockSpec((8, da), lambda b, t: (jnp.maximum(blk(b, t) * r8 - 1, 0), 0)),
                  pl.BlockSpec((8, da), lambda b, t: (jnp.minimum((blk(b, t) + 1) * r8, n // 8 - 1), 0)),
                  pl.BlockSpec((4, da), lambda b, t: (0, 0)),
                  pl.BlockSpec((1, da), lambda b, t: (0, 0)),
                  pl.BlockSpec((da, 2 * da), lambda b, t: (0, 0)),
                  pl.BlockSpec((1, 2 * da), lambda b, t: (0, 0)),
                  pl.BlockSpec((1, da), lambda b, t: (0, 0)),
                  pl.BlockSpec((None, 1, da), lambda b, t: (b, 0, 0))],
        out_specs=[pl.BlockSpec((TB, da), lambda b, t: (blk(b, t), 0)),
                   pl.BlockSpec((None, 1, da), lambda b, t: (b, 0, 0))],
        out_shape=[jax.ShapeDtypeStruct((n, da), F32), jax.ShapeDtypeStruct((bsz, 1, da), F32)],
        scratch_shapes=[pltpu.VMEM((TB + 16, da), F32), pltpu.VMEM((1, da), F32)],
        compiler_params=_cparams(("arbitrary", "arbitrary")),
        name=f"rglru_d{d}",
    )(pa, pa, pa, conv_w, conv_b.reshape(1, da), wg_d, bg_d.reshape(1, 2 * da), lam_d.reshape(1, da),
      h0_d.reshape(bsz, 1, da))


def _rwkv_prep_kernel(pb_ref, prev_ref, next_ref, mu_ref, w0_ref, w2_ref, a0_ref, a2_ref, g2_ref,
                      kk_ref, ka_ref, rk_ref, e_ref,
                      r_out, v_out, kk_out, g_out, bonus_out, lw_out, bt_out, kd_out, ext_ref, *, nt, tb, db, lw_, la_):
    t = pl.program_id(1)
    x = pb_ref[...]
    _fill_ext(ext_ref, x, prev_ref, next_ref, t == 0, t == nt - 1, tb)
    sh = 0.5 * (ext_ref[7:7 + tb, :] + ext_ref[9:9 + tb, :])
    z = x + mu_ref[...] * (sh - x)
    r = z[:, 0:db]
    k = z[:, db:2 * db]
    v = z[:, 2 * db:3 * db]
    o = 3 * db
    zw = z[:, o:o + 2 * lw_]
    za = z[:, o + 2 * lw_:o + 2 * lw_ + 2 * la_]
    zg = z[:, o + 2 * lw_ + 2 * la_:]
    w = -_softplus(-(w0_ref[...] + _dot3(_split(jnp.tanh(zw)), _split(w2_ref[...])))) - 0.5
    lw = -jnp.exp(w)
    a = _sigmoid(a0_ref[...] + _dot3(_split(za), _split(a2_ref[...])))
    g = _dot_bf(_sigmoid(zg), g2_ref[...])
    e = e_ref[...]
    kkv = k * kk_ref[...]
    ss = _seg_sum(kkv * kkv, e)
    kkn = kkv / jnp.maximum(jnp.sqrt(ss), 1e-12)
    ka = ka_ref[...]
    kd0 = k * (1.0 + (a[:, :db] - 1.0) * ka)
    kd1 = k * (1.0 + (a[:, db:] - 1.0) * ka)
    bonus = _seg_sum(r * rk_ref[...] * (kd0 + kd1), e) * v
    r_out[...] = r
    v_out[...] = v
    kk_out[...] = kkn
    g_out[...] = g
    bonus_out[...] = bonus
    lw_out[0] = lw[:, :db]
    lw_out[1] = lw[:, db:]
    bt_out[0] = kkn * a[:, :db]
    bt_out[1] = kkn * a[:, db:]
    kd_out[0] = kd0
    kd_out[1] = kd1


def _rwkv_prep(pb, mu, w0c, w2c, a0c, a2c, g2, k_k, k_a, r_k, e_bf, bsz, seq):
    n, ncol = pb.shape
    db = k_k.shape[-1]
    nt = seq // TB
    r8 = TB // 8
    blk = lambda b, t: b * nt + t
    full = lambda shp: pl.BlockSpec(shp, lambda b, t: (0,) * len(shp))
    row = pl.BlockSpec((TB, db), lambda b, t: (blk(b, t), 0))
    row2 = pl.BlockSpec((2, TB, db), lambda b, t: (0, blk(b, t), 0))
    s1 = jax.ShapeDtypeStruct((n, db), F32)
    s2 = jax.ShapeDtypeStruct((2, n, db), F32)
    lw_ = w2c.shape[0] // 2
    la_ = a2c.shape[0] // 2
    return pl.pallas_call(
        functools.partial(_rwkv_prep_kernel, nt=nt, tb=TB, db=db, lw_=lw_, la_=la_),
        grid=(bsz, nt),
        in_specs=[pl.BlockSpec((TB, ncol), lambda b, t: (blk(b, t), 0)),
                  pl.BlockSpec((8, ncol), lambda b, t: (jnp.maximum(blk(b, t) * r8 - 1, 0), 0)),
                  pl.BlockSpec((8, ncol), lambda b, t: (jnp.minimum((blk(b, t) + 1) * r8, n // 8 - 1), 0)),
                  full((1, ncol)), full((1, 2 * db)), full(w2c.shape), full((1, 2 * db)), full(a2c.shape),
                  full(g2.shape), full((1, db)), full((1, db)), full((1, db)), full((db, db))],
        out_specs=[row, row, row, row, row, row2, row2, row2],
        out_shape=[s1, s1, s1, s1, s1, s2, s2, s2],
        scratch_shapes=[pltpu.VMEM((TB + 16, ncol), F32)],
        compiler_params=_cparams(("arbitrary", "arbitrary"), VMEM_BIG),
        name="rwkv_prep",
    )(pb, pb, pb, mu.reshape(1, ncol), w0c, w2c, a0c, a2c, g2, k_k.reshape(1, db), k_a.reshape(1, db),
      r_k.reshape(1, db), e_bf)


def _tri_masks(c, d):
    ti = lax.broadcasted_iota(jnp.int32, (c, c), 0)
    si = lax.broadcasted_iota(jnp.int32, (c, c), 1)
    if d == 0:
        return si < ti, si <= ti, ti == si
    return si > ti, si >= ti, ti == si


def _split(x):
    hi = x.astype(BF16)
    return hi, (x - hi.astype(F32)).astype(BF16)


def _cut(p, sl):
    return p[0][sl], p[1][sl]


def _dot3(a, b, ca=1, cb=0):
    m = a[0].shape[1 - ca]
    n = b[0].shape[1 - cb]
    if not (ca == 1 and cb == 1 and 2 * n == MXU_N):
        return _dot(a[0], b[0], ca, cb) + _dot(a[1], b[0], ca, cb) + _dot(a[0], b[1], ca, cb)
    q = _dot(jnp.concatenate(a, axis=0), jnp.concatenate(b, axis=0), ca, cb)
    return (q[:m, :n] + q[:m, n:]) + (q[m:, :n] + q[m:, n:])


def _staged(gens):
    results = [None] * len(gens)
    live = list(range(len(gens)))
    while live:
        still = []
        for i in live:
            try:
                out = next(gens[i])
            except StopIteration:
                continue
            if out is not None:
                results[i] = out
            still.append(i)
        live = still
    return results


def _rwkv_chunk(d, c, nh, hd, r, kk, v, lw, bt, kd, s_heads):
    strict, incl, diag = _tri_masks(c, d)
    cs = _mask_dot(incl.astype(BF16), lw)
    tot = cs[c - 1:c] if d == 0 else cs[0:1]
    einv = jnp.exp(-cs)
    eend = jnp.exp(tot - cs)
    gtot = jnp.exp(tot)
    a_t = _split(-kk * jnp.exp(cs - lw))
    r_t = _split(r * jnp.exp(cs))
    b_t = _split(bt * einv)
    k_t = _split(kd * einv)
    b_g = _split(bt * eend)
    k_g = _split(kd * eend)
    vs = _split(v)
    eye = diag.astype(F32)
    t2 = lax.broadcasted_iota(jnp.int32, (c, 2 * c), 0)
    s2 = lax.broadcasted_iota(jnp.int32, (c, 2 * c), 1)
    s2 = jnp.where(s2 >= c, s2 - c, s2)
    incl2 = (s2 <= t2) if d == 0 else (s2 >= t2)
    nsq = int(round(math.log2(SUB))) - 1
    ti = lax.broadcasted_iota(jnp.int32, (c, c), 0)
    si = lax.broadcasted_iota(jnp.int32, (c, c), 1)
    blk_masks = [ti // SUB == si // SUB]
    w = SUB
    while w < c:
        blk_masks.append((ti // (2 * w) == si // (2 * w)) & (ti // w != si // w))
        w *= 2
    def head(h):
        sl = (slice(None), slice(h * hd, (h + 1) * hd))
        s0 = s_heads[h]
        s0s = _split(s0)
        ar = tuple(jnp.concatenate([x[sl], y[sl]], axis=0) for x, y in zip(a_t, r_t))
        bk = tuple(jnp.concatenate([x[sl], y[sl]], axis=0) for x, y in zip(b_t, k_t))
        gm = _dot3(ar, bk, 1, 1)
        yield
        a_ab = jnp.where(strict, gm[:c, :c], 0.0)
        a_ak = jnp.where(strict, gm[:c, c:], 0.0)
        ars =_dot3(ar, s0s, 1, 1)
        yield
        vh = _cut(vs, sl)
        x = ars[:c] + _dot3(_split(a_ak), vh)
        yield
        nd = jnp.where(blk_masks[0], a_ab, 0.0)
        tinv = eye + nd
        p = _split(nd)
        p = _split(_dot3(p, p))
        yield
        tinv = tinv + _dot3(_split(tinv), p)
        yield
        pb = p[0]
        for _ in range(nsq - 1):
            pb = _dot(pb, pb).astype(BF16)
            yield
            tinv = tinv + _dot(tinv.astype(BF16), pb)
            yield
        for m in blk_masks[1:]:
            tb16 = tinv.astype(BF16)
            half = _dot(tb16, jnp.where(m, a_ab, 0.0).astype(BF16))
            yield
            tinv = tinv + _dot(half.astype(BF16), tb16)
            yield
        u = _dot3(_split(tinv), _split(x))
        yield
        us = _split(u)
        uv = tuple(jnp.concatenate([x, y], axis=0) for x, y in zip(us, vh))
        y_out = ars[c:] + _dot(jnp.where(incl2, gm[c:], 0.0).astype(BF16), uv[0])
        yield
        bkg = tuple(jnp.concatenate([x[sl], y[sl]], axis=0) for x, y in zip(b_g, k_g))
        s_out = s0 * gtot[sl] + _dot3(uv, bkg, 0, 0)
        yield (y_out, s_out)

    return [head(h) for h in range(nh)]


def _rwkv_scan_kernel(r0_ref, kk0_ref, v0_ref, lw0_ref, bt0_ref, kd0_ref,
                      r1_ref, kk1_ref, v1_ref, lw1_ref, bt1_ref, kd1_ref, s0_ref,
                      y0_ref, y1_ref, sfin_ref, s_ref, *, c, nh, hd):
    @pl.when(pl.program_id(1) == 0)
    def _():
        s_ref[...] = s0_ref[...]

    s_in = [[s_ref[d, h] for h in range(nh)] for d in range(2)]
    ga = _rwkv_chunk(0, c, nh, hd, r0_ref[...], kk0_ref[...], v0_ref[...], lw0_ref[...], bt0_ref[...],
                     kd0_ref[...], s_in[0])
    gb = _rwkv_chunk(1, c, nh, hd, r1_ref[...], kk1_ref[...], v1_ref[...], lw1_ref[...], bt1_ref[...],
                     kd1_ref[...], s_in[1])
    res = _staged([g for pair in zip(ga, gb) for g in pair])
    ra, rb = res[0::2], res[1::2]
    y0_ref[...] = jnp.concatenate([y for y, _ in ra], axis=1)
    y1_ref[...] = jnp.concatenate([y for y, _ in rb], axis=1)
    for h in range(nh):
        s_ref[0, h] = ra[h][1]
        s_ref[1, h] = rb[h][1]
        sfin_ref[0, h] = ra[h][1]
        sfin_ref[1, h] = rb[h][1]


def _rwkv_scan(r, kkn, v, lw, bt, kd, st, e, bsz, seq):
    n, db = r.shape
    nh = db // HEAD_B
    nc = seq // CH
    fwd = lambda b, c: (b * nc + c, 0)
    bwd = lambda b, c: (b * nc + nc - 1 - c, 0)
    row0 = pl.BlockSpec((CH, db), fwd)
    row1 = pl.BlockSpec((CH, db), bwd)
    dir0 = pl.BlockSpec((None, CH, db), lambda b, c: (0,) + fwd(b, c))
    dir1 = pl.BlockSpec((None, CH, db), lambda b, c: (1,) + bwd(b, c))
    st_in = pl.BlockSpec((None, None, 2, nh, HEAD_B, HEAD_B), lambda b, c: (b, e, 0, 0, 0, 0))
    st_out = pl.BlockSpec((None, 2, nh, HEAD_B, HEAD_B), lambda b, c: (b, 0, 0, 0, 0))
    return pl.pallas_call(
        functools.partial(_rwkv_scan_kernel, c=CH, nh=nh, hd=HEAD_B),
        grid=(bsz, nc),
        in_specs=[row0, row0, row0, dir0, dir0, dir0, row1, row1, row1, dir1, dir1, dir1, st_in],
        out_specs=[row0, row1, st_out],
        out_shape=[jax.ShapeDtypeStruct((n, db), F32), jax.ShapeDtypeStruct((n, db), F32),
                   jax.ShapeDtypeStruct((bsz, 2, nh, HEAD_B, HEAD_B), F32)],
        scratch_shapes=[pltpu.VMEM((2, nh, HEAD_B, HEAD_B), F32)],
        compiler_params=_cparams(("arbitrary", "arbitrary")),
        name="rwkv_scan",
    )(r, kkn, v, lw, bt, kd, r, kkn, v, lw, bt, kd, st)


def _log_sigmoid(x):
    return -_softplus(-x)


def _split3(x):
    hi = x.astype(BF16)
    r1 = x - hi.astype(F32)
    mid = r1.astype(BF16)
    return hi, mid, (r1 - mid.astype(F32)).astype(BF16)


def _gla_chunk(d, c, nh, dk, dv, p_ref, wup, bgk, s_heads):
    _, incl, _ = _tri_masks(c, d)
    hk = nh * dk
    q = p_ref[:, 0:hk] * dk ** -0.5
    k = p_ref[:, hk:2 * hk]
    gk = _dot3(_split(p_ref[:, 2 * hk + nh * dv:]), _split(wup)) + bgk
    la = _log_sigmoid(gk) / GATE_NORM
    la3 = _split3(la)
    mask_bf = incl.astype(BF16)
    cs = _dot(mask_bf, la3[0]) + _dot(mask_bf, la3[1]) + _dot(mask_bf, la3[2])
    tot = cs[c - 1:c] if d == 0 else cs[0:1]
    q_t = (q * jnp.exp(cs)).astype(BF16)
    k_t = (k * jnp.exp(-cs)).astype(BF16)
    k_e = _split(k * jnp.exp(tot - cs))
    ones = jnp.ones((c, dv), BF16)

    def head(h):
        sk = (slice(None), slice(h * dk, (h + 1) * dk))
        vh = _split(p_ref[:, 2 * hk + h * dv:2 * hk + (h + 1) * dv])
        s0 = s_heads[h]
        qh = q_t[sk]
        att = jnp.where(incl, _dot(qh, k_t[sk], 1, 1), 0.0)
        yield
        inter = _dot(qh, s0.astype(BF16))
        yield
        o_out = _dot(att.astype(BF16), vh[0]) + inter
        yield
        dec = jnp.exp(_dot(la3[0][sk], ones, 0, 0) + _dot(la3[1][sk], ones, 0, 0) + _dot(la3[2][sk], ones, 0, 0))
        yield
        yield (o_out, s0 * dec + _dot3(_cut(k_e, sk), vh, 0, 0))

    return [head(h) for h in range(nh)]


def _gla_kernel(p0_ref, p1_ref, s0_ref, wup_ref, bgk_ref, o0_ref, o1_ref, sfin_ref, s_ref, *, c, nh, dk, dv):
    @pl.when(pl.program_id(1) == 0)
    def _():
        s_ref[...] = s0_ref[...]

    s_in = [[s_ref[d, h] for h in range(nh)] for d in range(2)]
    ga = _gla_chunk(0, c, nh, dk, dv, p0_ref, wup_ref[0], bgk_ref[0], s_in[0])
    gb = _gla_chunk(1, c, nh, dk, dv, p1_ref, wup_ref[1], bgk_ref[1], s_in[1])
    res = _staged([g for pair in zip(ga, gb) for g in pair])
    ra, rb = res[0::2], res[1::2]
    o0_ref[...] = jnp.concatenate([o for o, _ in ra], axis=1)
    o1_ref[...] = jnp.concatenate([o for o, _ in rb], axis=1)
    for h in range(nh):
        s_ref[0, h] = ra[h][1]
        s_ref[1, h] = rb[h][1]
        sfin_ref[0, h] = ra[h][1]
        sfin_ref[1, h] = rb[h][1]


def _gla(p, st, o_, wup, bgk, bsz, seq, grid_mode):
    nh, dk, dv = st.shape[3:]
    hv = nh * dv
    nc = seq // CH
    if grid_mode:
        assert seq // GRID_W == CH and nc == GRID_W
        ncol = p.shape[2] // GRID_W
        p_v = p
        blk = lambda w: (None, CH, w)
        fwd = lambda b, c: (b, 0, c)
        bwd = lambda b, c: (b, 0, nc - 1 - c)
        o_shape = jax.ShapeDtypeStruct((bsz, CH, GRID_W * hv), F32)
    else:
        ncol = p.shape[1]
        p_v = p.reshape(bsz * nc, CH, ncol)
        blk = lambda w: (None, CH, w)
        fwd = lambda b, c: (b * nc + c, 0, 0)
        bwd = lambda b, c: (b * nc + nc - 1 - c, 0, 0)
        o_shape = jax.ShapeDtypeStruct((bsz * nc, CH, hv), F32)
    st_in = pl.BlockSpec((None, None, 2, nh, dk, dv), lambda b, c: (b, o_, 0, 0, 0, 0))
    st_out = pl.BlockSpec((None, 2, nh, dk, dv), lambda b, c: (b, 0, 0, 0, 0))
    o0, o1, sfin = pl.pallas_call(
        functools.partial(_gla_kernel, c=CH, nh=nh, dk=dk, dv=dv),
        grid=(bsz, nc),
        in_specs=[pl.BlockSpec(blk(ncol), fwd), pl.BlockSpec(blk(ncol), bwd), st_in,
                  pl.BlockSpec(wup.shape, lambda b, c: (0, 0, 0)),
                  pl.BlockSpec(bgk.shape, lambda b, c: (0, 0, 0))],
        out_specs=[pl.BlockSpec(blk(hv), fwd), pl.BlockSpec(blk(hv), bwd), st_out],
        out_shape=[o_shape, o_shape, jax.ShapeDtypeStruct((bsz, 2, nh, dk, dv), F32)],
        scratch_shapes=[pltpu.VMEM((2, nh, dk, dv), F32)],
        compiler_params=_cparams(("arbitrary", "arbitrary"), VMEM_BIG),
        name="gla",
    )(p_v, p_v, st, wup, bgk)
    if not grid_mode:
        o0, o1 = o0.reshape(bsz * seq, hv), o1.reshape(bsz * seq, hv)
    return o0, o1, sfin


def _router_tail(x1, mod, g2n_ref, rw_ref, x_out, h2_out, aff_out):
    x_out[...] = x1
    h2 = _rms(x1, g2n_ref[...]) * (1.0 + mod[4:5]) + mod[3:4]
    h2_out[...] = h2.astype(BF16)
    logits = _dot3(_split(rw_ref[...]), _split(h2), 1, 1)
    m = jnp.max(logits, axis=0, keepdims=True)
    ex = jnp.exp(logits - m)
    aff_out[...] = ex / jnp.sum(ex, axis=0, keepdims=True)


def _out_even_kernel(x_ref, mod_ref, ha0_ref, ha1_ref, ga_ref, y0_ref, y1_ref, g_ref, bonus_ref, lnw_ref, lnb_ref,
                     e_ref, wo_ref, g2n_ref, rw_ref, x_out, h2_out, aff_out, *, da, hd):
    mod = mod_ref[...]
    ya = (ha0_ref[...] + ha1_ref[...]) * _gelu_tanh(ga_ref[...])
    y = y0_ref[...] + y1_ref[...]
    e = e_ref[...]
    mean = _seg_sum(y, e) * (1.0 / hd)
    yc = y - mean
    var = _seg_sum(yc * yc, e) * (1.0 / hd)
    yn = yc * lax.rsqrt(var + GN_EPS_B) * lnw_ref[...] + lnb_ref[...]
    yb = (yn + bonus_ref[...]) * g_ref[...]
    out = _dot_bf(ya, wo_ref[0:da, :]) + _dot_bf(yb, wo_ref[da:, :])
    _router_tail(x_ref[...] + mod[2:3] * out, mod, g2n_ref, rw_ref, x_out, h2_out, aff_out)


def _out_odd_kernel(x_ref, mod_ref, o0_ref, o1_ref, g_ref, gw_ref, wo_ref, g2n_ref, rw_ref,
                    x_out, h2_out, aff_out, *scratch, nh, dv, grid_rows):
    mod = mod_ref[...]
    gw = gw_ref[...]
    if grid_rows:
        hv = nh * dv
        o_scr = scratch[0]
        for c in range(GRID_W):
            for j in range(hv // LANES):
                lanes = slice(c * hv + j * LANES, c * hv + (j + 1) * LANES)
                o_scr[j, pl.ds(c, grid_rows, stride=GRID_W), :] = o0_ref[:, lanes] + o1_ref[:, lanes]
        o_all = jnp.concatenate([o_scr[j] for j in range(hv // LANES)], axis=1)
    else:
        o_all = o0_ref[...] + o1_ref[...]
    parts = []
    for h in range(nh):
        o = o_all[:, h * dv:(h + 1) * dv]
        parts.append(o * lax.rsqrt(jnp.mean(o * o, axis=-1, keepdims=True) + EPS) * gw)
    y = jnp.concatenate(parts, axis=1) * _silu(g_ref[...])
    out = _dot_bf(y, wo_ref[...])
    _router_tail(x_ref[...] + mod[2:3] * out, mod, g2n_ref, rw_ref, x_out, h2_out, aff_out)


def _out_common(kern, x, modr, mod_of, extra_args, extra_specs, norm2_g, router_wt, tb, name, scratch=()):
    n, d = x.shape
    ne = router_wt.shape[0]
    assert n % tb == 0
    mod_blk = mod_of(tb)
    row = lambda i: (i, 0)
    in_specs = ([pl.BlockSpec((tb, d), row), pl.BlockSpec((None, 6, d), lambda i: (mod_blk(i), 0, 0))]
                + extra_specs
                + [pl.BlockSpec((1, d), lambda i: (0, 0)), pl.BlockSpec((ne, d), lambda i: (0, 0))])
    return pl.pallas_call(
        kern, grid=(n // tb,), in_specs=in_specs,
        out_specs=[pl.BlockSpec((tb, d), row), pl.BlockSpec((tb, d), row),
                   pl.BlockSpec((ne, tb), lambda i: (0, i))],
        out_shape=[jax.ShapeDtypeStruct((n, d), F32), jax.ShapeDtypeStruct((n, d), BF16),
                   jax.ShapeDtypeStruct((ne, n), F32)],
        scratch_shapes=list(scratch),
        compiler_params=_cparams(("arbitrary",), VMEM_BIG),
        name=name,
    )(x, modr, *extra_args, norm2_g.reshape(1, d), router_wt)


def _select_kernel(aff_ref, slot_ref, cnt_ref, *, cap, ntb, tokb):
    aff = aff_ref[...]
    ne = aff.shape[0]
    big = jnp.float32(4.0)

    def body(carry):
        lo, hi, _, _ = carry
        inr = (aff > lo) & (aff <= hi)
        mn = jnp.min(jnp.where(inr, aff, big), axis=1, keepdims=True)
        mx = jnp.max(jnp.where(inr, aff, -big), axis=1, keepdims=True)
        done = mn >= mx
        piv = mn + (mx - mn) * 0.5
        piv = jnp.where(piv >= mx, mn, piv)
        ge = jnp.sum((aff > piv).astype(jnp.int32), axis=1, keepdims=True) >= cap
        lo = jnp.where(done | ~ge, lo, piv)
        hi = jnp.where(done | ge, hi, piv)
        return lo, hi, mx, jnp.sum((~done).astype(jnp.int32))

    init = (jnp.full((ne, 1), -1.0, F32), jnp.full((ne, 1), big, F32), jnp.zeros((ne, 1), F32), jnp.int32(1))
    _, _, thr, _ = lax.while_loop(lambda carry: carry[3] > 0, body, init)
    gt = aff > thr
    eq = aff == thr
    need = cap - jnp.sum(gt.astype(jnp.int32), axis=1, keepdims=True)
    si = lax.broadcasted_iota(jnp.int32, (tokb, tokb), 0)
    ti = lax.broadcasted_iota(jnp.int32, (tokb, tokb), 1)
    ustrict = (si < ti).astype(BF16)

    def excl_count(mask_blk, carry):
        m = mask_blk.astype(BF16)
        pre = _dot(m, ustrict).astype(jnp.int32) + carry
        return pre, carry + jnp.sum(mask_blk.astype(jnp.int32), axis=1, keepdims=True)

    ceq = jnp.zeros((ne, 1), jnp.int32)
    csel = jnp.zeros((ne, 1), jnp.int32)
    lane = lax.broadcasted_iota(jnp.int32, (ne, 128), 1)
    cnt_acc = jnp.zeros((ne, 128), jnp.int32)
    for j in range(ntb):
        sl = slice(j * tokb, (j + 1) * tokb)
        eq_rank, ceq = excl_count(eq[:, sl], ceq)
        sel = gt[:, sl] | (eq[:, sl] & (eq_rank < need))
        cnt_acc = jnp.where(lane == j, csel, cnt_acc)
        rank, csel = excl_count(sel, csel)
        slot_ref[:, sl] = jnp.where(sel, rank, -1)
    cnt_acc = jnp.where(lane == ntb, csel, cnt_acc)
    cnt_ref[...] = cnt_acc


def _select(aff_t, cap):
    ne, n = aff_t.shape
    ntb = n // TOK_B
    assert ntb < 128
    return pl.pallas_call(
        functools.partial(_select_kernel, cap=cap, ntb=ntb, tokb=TOK_B),
        grid=(1,),
        in_specs=[pl.BlockSpec((ne, n), lambda i: (0, 0))],
        out_specs=[pl.BlockSpec((ne, n), lambda i: (0, 0)), pl.BlockSpec((ne, 128), lambda i: (0, 0))],
        out_shape=[jax.ShapeDtypeStruct((ne, n), jnp.int32), jax.ShapeDtypeStruct((ne, 128), jnp.int32)],
        compiler_params=_cparams(("arbitrary",)),
        name="select",
    )(aff_t)


def _windows(cnt_ref, tb, ntb, ne, cap, w, rnd):
    out = []
    for e in range(ne):
        first = (cnt_ref[e * (ntb + 1) + tb] // SLOT_ALIGN) * SLOT_ALIGN + rnd * w
        out.append((pl.multiple_of(jnp.minimum(first, cap - w), SLOT_ALIGN), first))
    return out


def _onehot_rows(slot_ref, e, base, first, w, tokb):
    sidx = lax.broadcasted_iota(jnp.int32, (w, tokb), 0) + base
    return (slot_ref[e:e + 1, :] == sidx) & (sidx >= first)


def _gather_kernel(cnt_ref, nr_ref, slot_ref, h2_ref, xe_ref, lhs_ref, *, ntb, ne, cap, w, tokb):
    tb = pl.program_id(0)

    @pl.when(tb == 0)
    def _():
        xe_ref[...] = jnp.zeros_like(xe_ref)

    def body(rnd, carry):
        wins = _windows(cnt_ref, tb, ntb, ne, cap, w, rnd)
        for e, (base, first) in enumerate(wins):
            lhs_ref[e * w:(e + 1) * w, :] = jnp.where(_onehot_rows(slot_ref, e, base, first, w, tokb),
                                                      1.0, 0.0).astype(BF16)
        rows = _dot(lhs_ref[...], h2_ref[...])
        for e, (base, _) in enumerate(wins):
            xe_ref[e, pl.ds(base, w), :] += rows[e * w:(e + 1) * w].astype(BF16)
        return carry

    lax.fori_loop(0, nr_ref[tb], body, 0)


def _gather(cnt, nr, slot_t, h2, cap):
    ntb, ne, tokb = slot_t.shape
    n, d = h2.shape
    return pl.pallas_call(
        functools.partial(_gather_kernel, ntb=ntb, ne=ne, cap=cap, w=SLOT_W, tokb=tokb),
        grid_spec=pltpu.PrefetchScalarGridSpec(
            num_scalar_prefetch=2, grid=(ntb,),
            in_specs=[pl.BlockSpec((None, ne, tokb), lambda t, cnt, nr: (t, 0, 0)),
                      pl.BlockSpec((tokb, d), lambda t, cnt, nr: (t, 0))],
            out_specs=pl.BlockSpec((ne, cap, d), lambda t, cnt, nr: (0, 0, 0), pipeline_mode=pl.Buffered(1)),
            scratch_shapes=[pltpu.VMEM((ne * SLOT_W, tokb), BF16)]),
        out_shape=jax.ShapeDtypeStruct((ne, cap, d), BF16),
        compiler_params=_cparams(("arbitrary",), VMEM_BIG),
        name="moe_gather",
    )(cnt, nr, slot_t, h2)


def _ffn_kernel(xe_ref, wg_ref, wu_ref, wd_ref, ye_ref, acc_ref):
    f = pl.program_id(1)
    xe = xe_ref[...]
    hg = _dot(xe, wg_ref[...].astype(BF16))
    hu = _dot(xe, wu_ref[...].astype(BF16))
    hid = (_silu(hg) * hu).astype(BF16)
    part = _dot(hid, wd_ref[...].astype(BF16))

    @pl.when(f == 0)
    def _():
        acc_ref[...] = part

    @pl.when(f > 0)
    def _():
        acc_ref[...] += part

    @pl.when(f == pl.num_programs(1) - 1)
    def _():
        ye_ref[...] = acc_ref[...].astype(BF16)


def _ffn(xe, w_gate, w_up, w_down, l):
    ne, cap, d = xe.shape
    dff = w_gate.shape[-1]
    nf = dff // FF_B
    return pl.pallas_call(
        _ffn_kernel,
        grid=(ne, nf),
        in_specs=[pl.BlockSpec((None, cap, d), lambda e, f: (e, 0, 0)),
                  pl.BlockSpec((None, None, d, FF_B), lambda e, f: (l, e, 0, f)),
                  pl.BlockSpec((None, None, d, FF_B), lambda e, f: (l, e, 0, f)),
                  pl.BlockSpec((None, None, FF_B, d), lambda e, f: (l, e, f, 0))],
        out_specs=pl.BlockSpec((None, cap, d), lambda e, f: (e, 0, 0)),
        out_shape=jax.ShapeDtypeStruct((ne, cap, d), BF16),
        scratch_shapes=[pltpu.VMEM((cap, d), F32)],
        compiler_params=_cparams(("arbitrary", "arbitrary"), VMEM_BIG),
        name="moe_ffn",
    )(xe, w_gate, w_up, w_down)


def _scatter_kernel(cnt_ref, nr_ref, slot_ref, gate_ref, ye_ref, x_ref, mod_ref, fg_ref, out_ref, g_ref, ywin_ref,
                    *, ntb, ne, cap, w, tokb, final):
    tb = pl.program_id(0)
    out_ref[...] = jnp.zeros_like(out_ref)

    def body(rnd, carry):
        wins = _windows(cnt_ref, tb, ntb, ne, cap, w, rnd)
        for e, (base, first) in enumerate(wins):
            g_ref[e * w:(e + 1) * w, :] = jnp.where(_onehot_rows(slot_ref, e, base, first, w, tokb),
                                                    gate_ref[e:e + 1, :], 0.0).astype(BF16)
            ywin_ref[e * w:(e + 1) * w, :] = ye_ref[e, pl.ds(base, w), :]
        out_ref[...] += _dot(g_ref[...], ywin_ref[...], 0, 0)
        return carry

    lax.fori_loop(0, nr_ref[tb], body, 0)
    x = x_ref[...] + mod_ref[...][5:6] * out_ref[...]
    out_ref[...] = _rms(x, fg_ref[...]) if final else x


def _scatter(cnt, nr, slot_t, gate_t, ye, x, modr, mod_tok, final_gain, final):
    ne, cap, d = ye.shape
    ntb, _, tokb = slot_t.shape
    n = ntb * tokb
    tok = pl.BlockSpec((None, ne, tokb), lambda t, cnt, nr: (t, 0, 0))
    row = pl.BlockSpec((tokb, d), lambda t, cnt, nr: (t, 0))
    return pl.pallas_call(
        functools.partial(_scatter_kernel, ntb=ntb, ne=ne, cap=cap, w=SLOT_W, tokb=tokb, final=final),
        grid_spec=pltpu.PrefetchScalarGridSpec(
            num_scalar_prefetch=2, grid=(ntb,),
            in_specs=[tok, tok,
                      pl.BlockSpec((ne, cap, d), lambda t, cnt, nr: (0, 0, 0), pipeline_mode=pl.Buffered(1)),
                      row,
                      pl.BlockSpec((None, 6, d), lambda t, cnt, nr: (mod_tok(t), 0, 0)),
                      pl.BlockSpec((1, d), lambda t, cnt, nr: (0, 0))],
            out_specs=row,
            scratch_shapes=[pltpu.VMEM((ne * SLOT_W, tokb), BF16), pltpu.VMEM((ne * SLOT_W, d), BF16)]),
        out_shape=jax.ShapeDtypeStruct((n, d), F32),
        compiler_params=_cparams(("arbitrary",), VMEM_BIG),
        name="moe_scatter",
    )(cnt, nr, slot_t, gate_t, ye, x, modr, final_gain.reshape(1, d))


def _moe(h2, aff_t, w_gate, w_up, w_down, l, x, modr, mod_tok, final_gain, final):
    n, d = h2.shape
    ne = aff_t.shape[0]
    cap = max(1, CAP_FACTOR * n // ne)
    assert cap % SLOT_W == 0 and n % TOK_B == 0
    ntb = n // TOK_B
    slot, cnt = _select(aff_t, cap)
    cnt = cnt[:, :ntb + 1]
    span = cnt[:, 1:] - (cnt[:, :-1] // SLOT_ALIGN) * SLOT_ALIGN
    nr = jnp.max(jnp.where(cnt[:, 1:] > cnt[:, :-1], (span + SLOT_W - 1) // SLOT_W, 0), axis=0).astype(jnp.int32)
    slot_t = slot.reshape(ne, ntb, TOK_B).transpose(1, 0, 2)
    gate_t = aff_t.reshape(ne, ntb, TOK_B).transpose(1, 0, 2)
    xe = _gather(cnt.reshape(-1), nr, slot_t, h2, cap)
    ye = _ffn(xe, w_gate, w_up, w_down, l)
    return _scatter(cnt.reshape(-1), nr, slot_t, gate_t, ye, x, modr, mod_tok, final_gain, final)


def _block_diag(w):
    h, n, _ = w.shape
    eye = jnp.eye(h, dtype=w.dtype)
    return (eye[:, None, :, None] * w[:, :, None, :]).reshape(h * n, h * n)


def _lora_cat(w2):
    _, l, c = w2.shape
    z = jnp.zeros((l, c), w2.dtype)
    return jnp.concatenate([jnp.concatenate([w2[0], z], 1), jnp.concatenate([z, w2[1]], 1)], 0)


def _trunk(x, bsz, seq, cond_base, per_batch, st_rglru, st_rwkv, st_gla, grid_mode, modr, P):
    d_model = x.shape[1]
    depth = P['norm1_g'].shape[0]
    new_rglru, new_rwkv, new_gla = [], [], []
    for l in range(depth):
        if per_batch:
            mod_of = lambda tb, l=l: (lambda i: l * 8 + cond_base + (i * tb) // seq)
        else:
            mod_of = lambda tb, l=l: (lambda i: l * 8 + cond_base)
        router_wt = P['router_w'][l].T
        if l % 2 == 0:
            e = l // 2
            da = P['a_conv_w'].shape[-1]
            db = P['b_k_k'].shape[-1]
            w_in = P['ev_w_in'][e].astype(BF16)
            nb_cols = w_in.shape[1] - 2 * da
            pa, pb = _inproj(x, modr, mod_of, P['norm1_g'][l], w_in, (2 * da, nb_cols), bsz, seq)
            ha, sa = [], []
            for d in range(2):
                wg = jnp.concatenate([_block_diag(P['a_w_rg'][e, d]), _block_diag(P['a_w_ig'][e, d])], 1).astype(BF16)
                bg = jnp.concatenate([P['a_b_rg'][e, d], P['a_b_ig'][e, d]])
                h_d, hf = _rglru(pa, P['a_conv_w'][e], P['a_conv_b'][e], wg, bg, P['a_lam'][e, d],
                                 st_rglru[:, e, d], d, bsz, seq)
                ha.append(h_d)
                sa.append(hf.reshape(bsz, da))
            new_rglru.append(jnp.stack(sa, 1))
            e_bf = _block_diag(jnp.ones((db // HEAD_B, HEAD_B, HEAD_B), BF16))
            r, v, kkn, g, bonus, lw, bt, kd = _rwkv_prep(
                pb, P['b_mu'][e], P['b_w0'][e].reshape(1, 2 * db), _lora_cat(P['b_w2'][e]),
                P['b_a0'][e].reshape(1, 2 * db), _lora_cat(P['b_a2'][e]), P['b_g2'][e].astype(BF16),
                P['b_k_k'][e], P['b_k_a'][e], P['b_r_k'][e], e_bf, bsz, seq)
            y0, y1, sfin = _rwkv_scan(r, kkn, v, lw, bt, kd, st_rwkv, e, bsz, seq)
            ys = [y0, y1]
            new_rwkv.append(sfin)
            tb = _tok_block(False)
            rowh = pl.BlockSpec((tb, da), lambda i: (i, 0))
            rowg = pl.BlockSpec((tb, da), lambda i: (i, 1))
            vec = pl.BlockSpec((1, db), lambda i: (0, 0))
            x, h2, aff_t = _out_common(
                functools.partial(_out_even_kernel, da=da, hd=HEAD_B), x, modr, mod_of,
                [ha[0], ha[1], pa, ys[0], ys[1], g, bonus, P['b_ln_w'][e].reshape(1, db), P['b_ln_b'][e].reshape(1, db),
                 e_bf, P['ev_w_out'][e].astype(BF16)],
                [rowh, rowh, rowg, rowh, rowh, rowh, rowh, vec, vec,
                 pl.BlockSpec((db, db), lambda i: (0, 0)),
                 pl.BlockSpec((da + db, d_model), lambda i: (0, 0))],
                P['norm2_g'][l], router_wt, tb, "out_even")
        else:
            o_ = l // 2
            w_in = P['od_w_in'][o_]
            nh, dk, dv = st_gla.shape[3:]
            hv = nh * dv
            qkv = 2 * nh * dk + hv
            gin = -(2 * GATE_RANK) % 128 + 2 * GATE_RANK
            w_in = jnp.concatenate([w_in[:, :qkv], w_in[:, qkv + hv:],
                                    jnp.zeros((d_model, gin - 2 * GATE_RANK), w_in.dtype),
                                    w_in[:, qkv:qkv + hv]], axis=1).astype(BF16)
            p, g = _inproj(x, modr, mod_of, P['norm1_g'][l], w_in, (qkv + gin, hv), bsz, seq, grid_view=grid_mode)
            wup = jnp.stack([jnp.zeros((gin, nh * dk), F32).at[d * GATE_RANK:(d + 1) * GATE_RANK]
                             .set(P['c_w_gk_up'][o_, d]) for d in range(2)])
            o0, o1, sfin = _gla(p, st_gla, o_, wup, P['c_b_gk'][o_].reshape(2, 1, nh * dk), bsz, seq, grid_mode)
            new_gla.append(sfin)
            tb = _tok_block(grid_mode)
            nt = seq // tb
            if grid_mode:
                rowo = pl.BlockSpec((None, SUBLANES, GRID_W * hv), lambda i: (i // nt, i % nt, 0))
                scratch = [pltpu.VMEM((hv // LANES, tb, LANES), F32)]
            else:
                rowo = pl.BlockSpec((tb, hv), lambda i: (i, 0))
                scratch = []
            rowg = pl.BlockSpec((tb, hv), lambda i: (i, 0))
            x, h2, aff_t = _out_common(
                functools.partial(_out_odd_kernel, nh=nh, dv=dv, grid_rows=SUBLANES if grid_mode else 0),
                x, modr, mod_of,
                [o0, o1, g, P['c_gnorm_w'][o_].reshape(1, dv), P['od_w_out'][o_].astype(BF16)],
                [rowo, rowo, rowg, pl.BlockSpec((1, dv), lambda i: (0, 0)),
                 pl.BlockSpec((hv, d_model), lambda i: (0, 0))],
                P['norm2_g'][l], router_wt, tb, "out_odd", scratch=scratch)
        x = _moe(h2, aff_t, P['moe_w_gate'], P['moe_w_up'], P['moe_w_down'], l, x, modr, mod_of(TOK_B),
                 P['final_norm_g'], l == depth - 1)
    return x, jnp.stack(new_rglru, 1), jnp.stack(new_rwkv, 1), jnp.stack(new_gla, 1)


def kernel(x_prompt, x_sample, state_rglru, state_rwkv, state_gla, c, c_ctx, norm1_g, norm2_g, ada_w, ada_b, router_w, moe_w_gate, moe_w_up, moe_w_down, ev_w_in, ev_w_out, a_conv_w, a_conv_b, a_w_rg, a_b_rg, a_w_ig, a_b_ig, a_lam, b_mu, b_w0, b_w2, b_a0, b_a2, b_g2, b_k_k, b_k_a, b_r_k, b_ln_w, b_ln_b, od_w_in, od_w_out, c_w_gk_up, c_b_gk, c_gnorm_w, final_norm_g):
    P = dict(norm1_g=norm1_g, norm2_g=norm2_g, router_w=router_w,
             moe_w_gate=moe_w_gate, moe_w_up=moe_w_up, moe_w_down=moe_w_down,
             ev_w_in=ev_w_in, ev_w_out=ev_w_out, a_conv_w=a_conv_w, a_conv_b=a_conv_b,
             a_w_rg=a_w_rg, a_b_rg=a_b_rg, a_w_ig=a_w_ig, a_b_ig=a_b_ig, a_lam=a_lam,
             b_mu=b_mu, b_w0=b_w0, b_w2=b_w2, b_a0=b_a0, b_a2=b_a2, b_g2=b_g2, b_k_k=b_k_k,
             b_k_a=b_k_a, b_r_k=b_r_k, b_ln_w=b_ln_w, b_ln_b=b_ln_b,
             od_w_in=od_w_in, od_w_out=od_w_out, c_w_gk_up=c_w_gk_up, c_b_gk=c_b_gk,
             c_gnorm_w=c_gnorm_w, final_norm_g=final_norm_g)
    bp, tp, d_model = x_prompt.shape
    bs, ts, _ = x_sample.shape
    assert bs + 1 <= 8 and tp % TB == 0 and ts % TB == 0
    depth = ada_w.shape[0]
    cv8 = jnp.zeros((8, d_model), F32).at[0].set(c_ctx).at[1:1 + bs].set(c)
    modr = _modulation(cv8, ada_w, ada_b).reshape(depth * 8, 6, d_model)
    dt = x_prompt.dtype
    z_rglru = jnp.zeros((bp,) + state_rglru.shape[1:], dt)
    z_rwkv = jnp.zeros((bp,) + state_rwkv.shape[1:], dt)
    z_gla = jnp.zeros((bp,) + state_gla.shape[1:], dt)
    y_prompt, n_rglru, n_rwkv, n_gla = _trunk(x_prompt.reshape(bp * tp, d_model), bp, tp, 0, False,
                                              z_rglru, z_rwkv, z_gla, False, modr, P)
    y_sample, _, _, _ = _trunk(x_sample.reshape(bs * ts, d_model), bs, ts, 1, True,
                               state_rglru, state_rwkv, state_gla, True, modr, P)
    return (y_prompt.reshape(bp, tp, d_model), y_sample.reshape(bs, ts, d_model), n_rglru, n_rwkv, n_gla)
```

```python
import functools
import math

import jax
import jax.numpy as jnp
from jax import lax
from jax.experimental import pallas as pl
from jax.experimental.pallas import tpu as pltpu

F32 = jnp.float32
BF16 = jnp.bfloat16
HI = lax.Precision.HIGHEST

GRID_W = 64
EPS = 1e-6
C_RGLRU = 8.0
GN_EPS_B = 64e-5
GATE_NORM = 16.0
CAP_FACTOR = 2
HEAD_B = 64
GATE_RANK = 16

SUBLANES = 8
LANES = 128
MXU_N = 256
PITCH_PAD = 4
TB = 256
CH = 64
SUB = 16
SEQ_PER_STEP = 2
SLOT_W = 128
SLOT_ALIGN = 16
TOK_B = 512
FF_B = 512
VMEM_BIG = 56 * 1024 * 1024


def _dot(a, b, ca=1, cb=0, prec=None):
    return lax.dot_general(a, b, (((ca,), (cb,)), ((), ())), precision=prec,
                           preferred_element_type=F32)


def _dot_bf(a, b, ca=1, cb=0):
    return _dot(a.astype(BF16), b.astype(BF16), ca, cb)


def _dot_hi(a, b, ca=1, cb=0):
    return _dot(a.astype(F32), b.astype(F32), ca, cb, HI)


def _seg_sum(x, e_bf):
    hi = x.astype(BF16)
    lo = (x - hi.astype(F32)).astype(BF16)
    return _dot(hi, e_bf) + _dot(lo, e_bf)


def _mask_dot(mask_bf, x, cm=1, cx=0):
    hi = x.astype(BF16)
    r1 = x - hi.astype(F32)
    mid = r1.astype(BF16)
    lo = (r1 - mid.astype(F32)).astype(BF16)
    return _dot(mask_bf, hi, cm, cx) + _dot(mask_bf, mid, cm, cx) + _dot(mask_bf, lo, cm, cx)


def _sigmoid(x):
    return 1.0 / (1.0 + jnp.exp(-x))


def _softplus(x):
    return jnp.maximum(x, 0.0) + jnp.log1p(jnp.exp(-jnp.abs(x)))


def _silu(x):
    return x * _sigmoid(x)


def _gelu_tanh(x):
    return 0.5 * x * (1.0 + jnp.tanh(math.sqrt(2.0 / math.pi) * (x + 0.044715 * (x * x * x))))


def _rms(x, g):
    return x * lax.rsqrt(jnp.mean(x * x, axis=-1, keepdims=True) + EPS) * g


def _cparams(sem, vmem=None):
    return pltpu.CompilerParams(dimension_semantics=sem, vmem_limit_bytes=vmem)


def _mod_kernel(c_ref, w_ref, b_ref, o_ref):
    o_ref[...] = _dot_hi(_silu(c_ref[...]), w_ref[...]) + b_ref[...]


def _modulation(cv8, ada_w, ada_b):
    depth, d, n6 = ada_w.shape
    nb = 1536
    return pl.pallas_call(
        _mod_kernel,
        grid=(depth, n6 // nb),
        in_specs=[pl.BlockSpec((8, d), lambda l, j: (0, 0)),
                  pl.BlockSpec((None, d, nb), lambda l, j: (l, 0, j)),
                  pl.BlockSpec((None, 1, nb), lambda l, j: (l, 0, j))],
        out_specs=pl.BlockSpec((None, 8, nb), lambda l, j: (l, 0, j)),
        out_shape=jax.ShapeDtypeStruct((depth, 8, n6), F32),
        compiler_params=_cparams(("arbitrary", "arbitrary")),
        name="modulation",
    )(cv8, ada_w, ada_b.reshape(depth, 1, n6))


def _project(x, mod, g_ref, w_ref, outs, splits, grid_rows):
    h = _rms(x, g_ref[...]) * (1.0 + mod[1:2]) + mod[0:1]
    p = _dot_bf(h, w_ref[...])
    lo = 0
    if grid_rows:
        n0 = splits[0]
        scr = outs[-1]
        pitch = GRID_W + PITCH_PAD
        for j in range(n0 // LANES):
            for r in range(grid_rows):
                scr[j, r * pitch:r * pitch + GRID_W, :] = p[r * GRID_W:(r + 1) * GRID_W, j * LANES:(j + 1) * LANES]
        for c in range(GRID_W):
            for j in range(n0 // LANES):
                outs[0][:, c * n0 + j * LANES:c * n0 + (j + 1) * LANES] = scr[j, pl.ds(c, grid_rows, stride=pitch), :]
        lo, outs, splits = n0, outs[1:-1], splits[1:]
    for o_ref, n in zip(outs, splits):
        o_ref[...] = p[:, lo:lo + n]
        lo += n


def _inproj_kernel(x_ref, mod_ref, g_ref, w_ref, *outs, splits, grid_rows):
    _project(x_ref[...], mod_ref[...], g_ref, w_ref, outs, splits, grid_rows)


def _tok_block(grid_mode):
    return SUBLANES * GRID_W if grid_mode else TOK_B


def _inproj(x, modr, mod_of, gain, w_bf, splits, bsz, seq, grid_view=False):
    n, d = x.shape
    tb = _tok_block(grid_view)
    assert n % tb == 0
    nt = seq // tb
    ncols = w_bf.shape[1]
    mod_blk = mod_of(tb)
    row = lambda i: (i, 0)
    in_specs = [pl.BlockSpec((tb, d), row),
                pl.BlockSpec((None, 6, d), lambda i: (mod_blk(i), 0, 0)),
                pl.BlockSpec((1, d), lambda i: (0, 0)),
                pl.BlockSpec((d, ncols), lambda i: (0, 0), pipeline_mode=pl.Buffered(1))]
    args = [x, modr, gain.reshape(1, d), w_bf]
    out_shape = [jax.ShapeDtypeStruct((n, c), F32) for c in splits]
    out_specs = [pl.BlockSpec((tb, c), row) for c in splits]
    scratch = []
    grid_rows = 0
    if grid_view:
        grid_rows = tb // GRID_W
        out_shape[0] = jax.ShapeDtypeStruct((bsz, seq // GRID_W, GRID_W * splits[0]), F32)
        out_specs[0] = pl.BlockSpec((None, grid_rows, GRID_W * splits[0]), lambda i: (i // nt, i % nt, 0))
        scratch = [pltpu.VMEM((splits[0] // LANES, grid_rows * (GRID_W + PITCH_PAD), LANES), F32)]
    return pl.pallas_call(
        functools.partial(_inproj_kernel, splits=splits, grid_rows=grid_rows),
        grid=(n // tb,), in_specs=in_specs, out_specs=out_specs, out_shape=out_shape,
        scratch_shapes=scratch,
        compiler_params=_cparams(("arbitrary",), VMEM_BIG),
        name="inproj",
    )(*args)


def _fill_ext(ext_ref, x, prev_ref, next_ref, first, last, tb):
    ext_ref[0:8, :] = jnp.where(first, 0.0, prev_ref[...])
    ext_ref[8:8 + tb, :] = x
    ext_ref[8 + tb:16 + tb, :] = jnp.where(last, 0.0, next_ref[...])


def _rglru_kernel(xa_ref, prev_ref, next_ref, cw_ref, cb_ref, wg_ref, bg_ref, lam_ref, h0_ref,
                  h_ref, hfin_ref, ext_ref, carry_ref, *, d, nt, tb):
    i = pl.program_id(1)
    t = i if d == 0 else nt - 1 - i
    first = t == 0
    last = t == nt - 1
    x = xa_ref[...]
    da = x.shape[1]
    _fill_ext(ext_ref, x, prev_ref, next_ref, first, last, tb)
    cw = cw_ref[...]
    u = (cw[0:1] * ext_ref[7:7 + tb, :] + cw[1:2] * x + cw[2:3] * ext_ref[9:9 + tb, :]
         + cw[3:4] * ext_ref[10:10 + tb, :] + cb_ref[...])
    gts = _dot_bf(u, wg_ref[...]) + bg_ref[...]
    gate_r = _sigmoid(gts[:, :da])
    gate_i = _sigmoid(gts[:, da:])
    log_a = -C_RGLRU * gate_r * _softplus(-lam_ref[...])
    a = jnp.exp(log_a)
    xin = jnp.sqrt(1.0 - jnp.exp(2.0 * log_a)) * gate_i * u

    @pl.when(i == 0)
    def _():
        carry_ref[...] = h0_ref[...]

    rows = lax.broadcasted_iota(jnp.int32, (tb, da), 0)
    s = 1
    while s < tb:
        if d == 0:
            a_sh = pltpu.roll(a, s, 0)
            x_sh = pltpu.roll(xin, s, 0)
            valid = rows >= s
        else:
            a_sh = pltpu.roll(a, tb - s, 0)
            x_sh = pltpu.roll(xin, tb - s, 0)
            valid = rows < tb - s
        xin = jnp.where(valid, a * x_sh + xin, xin)
        a = jnp.where(valid, a * a_sh, a)
        s *= 2
    h = xin + a * carry_ref[...]
    h_ref[...] = h
    newc = h[tb - 1:tb, :] if d == 0 else h[0:1, :]
    carry_ref[...] = newc
    hfin_ref[...] = newc


def _rglru(pa, conv_w, conv_b, wg_d, bg_d, lam_d, h0_d, d, bsz, seq):
    n = pa.shape[0]
    da = conv_w.shape[1]
    nt = seq // TB
    r8 = TB // 8
    tt = (lambda t: t) if d == 0 else (lambda t: nt - 1 - t)
    blk = lambda b, t: b * nt + tt(t)
    return pl.pallas_call(
        functools.partial(_rglru_kernel, d=d, nt=nt, tb=TB),
        grid=(bsz, nt),
        in_specs=[pl.BlockSpec((TB, da), lambda b, t: (blk(b, t), 0)),
                  pl.BlockSpec((8, da), lambda b, t: (jnp.maximum(blk(b, t) * r8 - 1, 0), 0)),
                  pl.BlockSpec((8, da), lambda b, t: (jnp.minimum((blk(b, t) + 1) * r8, n // 8 - 1), 0)),
                  pl.BlockSpec((4, da), lambda b, t: (0, 0)),
                  pl.BlockSpec((1, da), lambda b, t: (0, 0)),
                  pl.BlockSpec((da, 2 * da), lambda b, t: (0, 0)),
                  pl.BlockSpec((1, 2 * da), lambda b, t: (0, 0)),
                  pl.BlockSpec((1, da), lambda b, t: (0, 0)),
                  pl.BlockSpec((None, 1, da), lambda b, t: (b, 0, 0))],
        out_specs=[pl.BlockSpec((TB, da), lambda b, t: (blk(b, t), 0)),
                   pl.BlockSpec((None, 1, da), lambda b, t: (b, 0, 0))],
        out_shape=[jax.ShapeDtypeStruct((n, da), F32), jax.ShapeDtypeStruct((bsz, 1, da), F32)],
        scratch_shapes=[pltpu.VMEM((TB + 16, da), F32), pltpu.VMEM((1, da), F32)],
        compiler_params=_cparams(("arbitrary", "arbitrary")),
        name=f"rglru_d{d}",
    )(pa, pa, pa, conv_w, conv_b.reshape(1, da), wg_d, bg_d.reshape(1, 2 * da), lam_d.reshape(1, da),
      h0_d.reshape(bsz, 1, da))


def _rwkv_prep_kernel(pb_ref, prev_ref, next_ref, mu_ref, w0_ref, w2_ref, a0_ref, a2_ref, g2_ref,
                      kk_ref, ka_ref, rk_ref, e_ref,
                      r_out, v_out, kk_out, g_out, bonus_out, lw_out, bt_out, kd_out, ext_ref, *, nt, tb, db, lw_, la_):
    t = pl.program_id(1)
    x = pb_ref[...]
    _fill_ext(ext_ref, x, prev_ref, next_ref, t == 0, t == nt - 1, tb)
    sh = 0.5 * (ext_ref[7:7 + tb, :] + ext_ref[9:9 + tb, :])
    z = x + mu_ref[...] * (sh - x)
    r = z[:, 0:db]
    k = z[:, db:2 * db]
    v = z[:, 2 * db:3 * db]
    o = 3 * db
    zw = z[:, o:o + 2 * lw_]
    za = z[:, o + 2 * lw_:o + 2 * lw_ + 2 * la_]
    zg = z[:, o + 2 * lw_ + 2 * la_:]
    w = -_softplus(-(w0_ref[...] + _dot3(_split(jnp.tanh(zw)), _split(w2_ref[...])))) - 0.5
    lw = -jnp.exp(w)
    a = _sigmoid(a0_ref[...] + _dot3(_split(za), _split(a2_ref[...])))
    g = _dot_bf(_sigmoid(zg), g2_ref[...])
    e = e_ref[...]
    kkv = k * kk_ref[...]
    ss = _seg_sum(kkv * kkv, e)
    kkn = kkv / jnp.maximum(jnp.sqrt(ss), 1e-12)
    ka = ka_ref[...]
    kd0 = k * (1.0 + (a[:, :db] - 1.0) * ka)
    kd1 = k * (1.0 + (a[:, db:] - 1.0) * ka)
    bonus = _seg_sum(r * rk_ref[...] * (kd0 + kd1), e) * v
    r_out[...] = r
    v_out[...] = v
    kk_out[...] = kkn
    g_out[...] = g
    bonus_out[...] = bonus
    lw_out[0] = lw[:, :db]
    lw_out[1] = lw[:, db:]
    bt_out[0] = kkn * a[:, :db]
    bt_out[1] = kkn * a[:, db:]
    kd_out[0] = kd0
    kd_out[1] = kd1


def _rwkv_prep(pb, mu, w0c, w2c, a0c, a2c, g2, k_k, k_a, r_k, e_bf, bsz, seq):
    n, ncol = pb.shape
    db = k_k.shape[-1]
    nt = seq // TB
    r8 = TB // 8
    blk = lambda b, t: b * nt + t
    full = lambda shp: pl.BlockSpec(shp, lambda b, t: (0,) * len(shp))
    row = pl.BlockSpec((TB, db), lambda b, t: (blk(b, t), 0))
    row2 = pl.BlockSpec((2, TB, db), lambda b, t: (0, blk(b, t), 0))
    s1 = jax.ShapeDtypeStruct((n, db), F32)
    s2 = jax.ShapeDtypeStruct((2, n, db), F32)
    lw_ = w2c.shape[0] // 2
    la_ = a2c.shape[0] // 2
    return pl.pallas_call(
        functools.partial(_rwkv_prep_kernel, nt=nt, tb=TB, db=db, lw_=lw_, la_=la_),
        grid=(bsz, nt),
        in_specs=[pl.BlockSpec((TB, ncol), lambda b, t: (blk(b, t), 0)),
                  pl.BlockSpec((8, ncol), lambda b, t: (jnp.maximum(blk(b, t) * r8 - 1, 0), 0)),
                  pl.BlockSpec((8, ncol), lambda b, t: (jnp.minimum((blk(b, t) + 1) * r8, n // 8 - 1), 0)),
                  full((1, ncol)), full((1, 2 * db)), full(w2c.shape), full((1, 2 * db)), full(a2c.shape),
                  full(g2.shape), full((1, db)), full((1, db)), full((1, db)), full((db, db))],
        out_specs=[row, row, row, row, row, row2, row2, row2],
        out_shape=[s1, s1, s1, s1, s1, s2, s2, s2],
        scratch_shapes=[pltpu.VMEM((TB + 16, ncol), F32)],
        compiler_params=_cparams(("arbitrary", "arbitrary"), VMEM_BIG),
        name="rwkv_prep",
    )(pb, pb, pb, mu.reshape(1, ncol), w0c, w2c, a0c, a2c, g2, k_k.reshape(1, db), k_a.reshape(1, db),
      r_k.reshape(1, db), e_bf)


def _tri_masks(c, d):
    ti = lax.broadcasted_iota(jnp.int32, (c, c), 0)
    si = lax.broadcasted_iota(jnp.int32, (c, c), 1)
    if d == 0:
        return si < ti, si <= ti, ti == si
    return si > ti, si >= ti, ti == si


def _split(x):
    hi = x.astype(BF16)
    return hi, (x - hi.astype(F32)).astype(BF16)


def _cut(p, sl):
    return p[0][sl], p[1][sl]


def _dot3(a, b, ca=1, cb=0):
    m = a[0].shape[1 - ca]
    n = b[0].shape[1 - cb]
    if not (ca == 1 and cb == 1 and 2 * n == MXU_N):
        return _dot(a[0], b[0], ca, cb) + _dot(a[1], b[0], ca, cb) + _dot(a[0], b[1], ca, cb)
    q = _dot(jnp.concatenate(a, axis=0), jnp.concatenate(b, axis=0), ca, cb)
    return (q[:m, :n] + q[:m, n:]) + (q[m:, :n] + q[m:, n:])


def _staged(gens):
    results = [None] * len(gens)
    live = list(range(len(gens)))
    while live:
        still = []
        for i in live:
            try:
                out = next(gens[i])
            except StopIteration:
                continue
            if out is not None:
                results[i] = out
            still.append(i)
        live = still
    return results


def _rwkv_chunk(d, c, nh, hd, r, kk, v, lw, bt, kd, s_heads):
    strict, incl, diag = _tri_masks(c, d)
    cs = _mask_dot(incl.astype(BF16), lw)
    tot = cs[c - 1:c] if d == 0 else cs[0:1]
    einv = jnp.exp(-cs)
    eend = jnp.exp(tot - cs)
    gtot = jnp.exp(tot)
    a_t = _split(-kk * jnp.exp(cs - lw))
    r_t = _split(r * jnp.exp(cs))
    b_t = _split(bt * einv)
    k_t = _split(kd * einv)
    b_g = _split(bt * eend)
    k_g = _split(kd * eend)
    vs = _split(v)
    eye = diag.astype(F32)
    t2 = lax.broadcasted_iota(jnp.int32, (c, 2 * c), 0)
    s2 = lax.broadcasted_iota(jnp.int32, (c, 2 * c), 1)
    s2 = jnp.where(s2 >= c, s2 - c, s2)
    incl2 = (s2 <= t2) if d == 0 else (s2 >= t2)
    nsq = int(round(math.log2(SUB))) - 1
    ti = lax.broadcasted_iota(jnp.int32, (c, c), 0)
    si = lax.broadcasted_iota(jnp.int32, (c, c), 1)
    blk_masks = [ti // SUB == si // SUB]
    w = SUB
    while w < c:
        blk_masks.append((ti // (2 * w) == si // (2 * w)) & (ti // w != si // w))
        w *= 2
    def head(h):
        sl = (slice(None), slice(h * hd, (h + 1) * hd))
        s0 = s_heads[h]
        s0s = _split(s0)
        ar = tuple(jnp.concatenate([x[sl], y[sl]], axis=0) for x, y in zip(a_t, r_t))
        bk = tuple(jnp.concatenate([x[sl], y[sl]], axis=0) for x, y in zip(b_t, k_t))
        gm = _dot3(ar, bk, 1, 1)
        yield
        a_ab = jnp.where(strict, gm[:c, :c], 0.0)
        a_ak = jnp.where(strict, gm[:c, c:], 0.0)
        ars =_dot3(ar, s0s, 1, 1)
        yield
        vh = _cut(vs, sl)
        x = ars[:c] + _dot3(_split(a_ak), vh)
        yield
        nd = jnp.where(blk_masks[0], a_ab, 0.0)
        tinv = eye + nd
        p = _split(nd)
        p = _split(_dot3(p, p))
        yield
        tinv = tinv + _dot3(_split(tinv), p)
        yield
        pb = p[0]
        for _ in range(nsq - 1):
            pb = _dot(pb, pb).astype(BF16)
            yield
            tinv = tinv + _dot(tinv.astype(BF16), pb)
            yield
        for m in blk_masks[1:]:
            tb16 = tinv.astype(BF16)
            half = _dot(tb16, jnp.where(m, a_ab, 0.0).astype(BF16))
            yield
            tinv = tinv + _dot(half.astype(BF16), tb16)
            yield
        u = _dot3(_split(tinv), _split(x))
        yield
        us = _split(u)
        uv = tuple(jnp.concatenate([x, y], axis=0) for x, y in zip(us, vh))
        y_out = ars[c:] + _dot(jnp.where(incl2, gm[c:], 0.0).astype(BF16), uv[0])
        yield
        bkg = tuple(jnp.concatenate([x[sl], y[sl]], axis=0) for x, y in zip(b_g, k_g))
        s_out = s0 * gtot[sl] + _dot3(uv, bkg, 0, 0)
        yield (y_out, s_out)

    return [head(h) for h in range(nh)]


def _rwkv_scan_kernel(r0_ref, kk0_ref, v0_ref, lw0_ref, bt0_ref, kd0_ref,
                      r1_ref, kk1_ref, v1_ref, lw1_ref, bt1_ref, kd1_ref, s0_ref,
                      y0_ref, y1_ref, sfin_ref, s_ref, *, c, nh, hd, ns):
    @pl.when(pl.program_id(1) == 0)
    def _():
        s_ref[...] = s0_ref[...]

    gens = []
    for s in range(ns):
        ga = _rwkv_chunk(0, c, nh, hd, r0_ref[s], kk0_ref[s], v0_ref[s], lw0_ref[s], bt0_ref[s], kd0_ref[s],
                         [s_ref[s, 0, h] for h in range(nh)])
        gb = _rwkv_chunk(1, c, nh, hd, r1_ref[s], kk1_ref[s], v1_ref[s], lw1_ref[s], bt1_ref[s], kd1_ref[s],
                         [s_ref[s, 1, h] for h in range(nh)])
        gens.append((ga, gb))
    order = [g for h in range(nh) for ga, gb in gens for g in (ga[h], gb[h])]
    res = iter(_staged(order))
    out = [[[None] * nh, [None] * nh] for _ in range(ns)]
    for h in range(nh):
        for s in range(ns):
            out[s][0][h] = next(res)
            out[s][1][h] = next(res)
    for s in range(ns):
        y0_ref[s] = jnp.concatenate([y for y, _ in out[s][0]], axis=1)
        y1_ref[s] = jnp.concatenate([y for y, _ in out[s][1]], axis=1)
        for d in range(2):
            for h in range(nh):
                s_ref[s, d, h] = out[s][d][h][1]
                sfin_ref[s, d, h] = out[s][d][h][1]


def _rwkv_scan(r, kkn, v, lw, bt, kd, st, e, bsz, seq):
    n, db = r.shape
    nh = db // HEAD_B
    nc = seq // CH
    ns = SEQ_PER_STEP if bsz % SEQ_PER_STEP == 0 else 1
    grp = bsz // ns
    tok = lambda a: a.reshape(grp, ns, seq, db)
    tok2 = lambda a: a.reshape(2, grp, ns, seq, db)
    row0 = pl.BlockSpec((None, ns, CH, db), lambda g, c: (g, 0, c, 0))
    row1 = pl.BlockSpec((None, ns, CH, db), lambda g, c: (g, 0, nc - 1 - c, 0))
    dir0 = pl.BlockSpec((None, None, ns, CH, db), lambda g, c: (0, g, 0, c, 0))
    dir1 = pl.BlockSpec((None, None, ns, CH, db), lambda g, c: (1, g, 0, nc - 1 - c, 0))
    st_in = pl.BlockSpec((ns, None, 2, nh, HEAD_B, HEAD_B), lambda g, c: (g, e, 0, 0, 0, 0))
    st_out = pl.BlockSpec((ns, 2, nh, HEAD_B, HEAD_B), lambda g, c: (g, 0, 0, 0, 0))
    y_shape = jax.ShapeDtypeStruct((grp, ns, seq, db), F32)
    y0, y1, sfin = pl.pallas_call(
        functools.partial(_rwkv_scan_kernel, c=CH, nh=nh, hd=HEAD_B, ns=ns),
        grid=(grp, nc),
        in_specs=[row0, row0, row0, dir0, dir0, dir0, row1, row1, row1, dir1, dir1, dir1, st_in],
        out_specs=[row0, row1, st_out],
        out_shape=[y_shape, y_shape, jax.ShapeDtypeStruct((bsz, 2, nh, HEAD_B, HEAD_B), F32)],
        scratch_shapes=[pltpu.VMEM((ns, 2, nh, HEAD_B, HEAD_B), F32)],
        compiler_params=_cparams(("arbitrary", "arbitrary"), VMEM_BIG),
        name="rwkv_scan",
    )(tok(r), tok(kkn), tok(v), tok2(lw), tok2(bt), tok2(kd), tok(r), tok(kkn), tok(v), tok2(lw), tok2(bt), tok2(kd), st)
    return y0.reshape(n, db), y1.reshape(n, db), sfin


def _log_sigmoid(x):
    return -_softplus(-x)


def _split3(x):
    hi = x.astype(BF16)
    r1 = x - hi.astype(F32)
    mid = r1.astype(BF16)
    return hi, mid, (r1 - mid.astype(F32)).astype(BF16)


def _gla_chunk(d, c, nh, dk, dv, p_ref, wup, bgk, s_heads):
    _, incl, _ = _tri_masks(c, d)
    hk = nh * dk
    q = p_ref[:, 0:hk] * dk ** -0.5
    k = p_ref[:, hk:2 * hk]
    gk = _dot3(_split(p_ref[:, 2 * hk + nh * dv:]), _split(wup)) + bgk
    la = _log_sigmoid(gk) / GATE_NORM
    la3 = _split3(la)
    mask_bf = incl.astype(BF16)
    cs = _dot(mask_bf, la3[0]) + _dot(mask_bf, la3[1]) + _dot(mask_bf, la3[2])
    tot = cs[c - 1:c] if d == 0 else cs[0:1]
    q_t = (q * jnp.exp(cs)).astype(BF16)
    k_t = (k * jnp.exp(-cs)).astype(BF16)
    k_e = _split(k * jnp.exp(tot - cs))
    ones = jnp.ones((c, dv), BF16)

    def head(h):
        sk = (slice(None), slice(h * dk, (h + 1) * dk))
        vh = _split(p_ref[:, 2 * hk + h * dv:2 * hk + (h + 1) * dv])
        s0 = s_heads[h]
        qh = q_t[sk]
        att = jnp.where(incl, _dot(qh, k_t[sk], 1, 1), 0.0)
        yield
        inter = _dot(qh, s0.astype(BF16))
        yield
        o_out = _dot(att.astype(BF16), vh[0]) + inter
        yield
        dec = jnp.exp(_dot(la3[0][sk], ones, 0, 0) + _dot(la3[1][sk], ones, 0, 0) + _dot(la3[2][sk], ones, 0, 0))
        yield
        yield (o_out, s0 * dec + _dot3(_cut(k_e, sk), vh, 0, 0))

    return [head(h) for h in range(nh)]


def _gla_kernel(p0_ref, p1_ref, s0_ref, wup_ref, bgk_ref, o0_ref, o1_ref, sfin_ref, s_ref, *, c, nh, dk, dv, ns):
    @pl.when(pl.program_id(1) == 0)
    def _():
        s_ref[...] = s0_ref[...]

    gens = []
    for s in range(ns):
        ga = _gla_chunk(0, c, nh, dk, dv, p0_ref.at[s], wup_ref[0], bgk_ref[0], [s_ref[s, 0, h] for h in range(nh)])
        gb = _gla_chunk(1, c, nh, dk, dv, p1_ref.at[s], wup_ref[1], bgk_ref[1], [s_ref[s, 1, h] for h in range(nh)])
        gens.append((ga, gb))
    order = [g for h in range(nh) for ga, gb in gens for g in (ga[h], gb[h])]
    res = iter(_staged(order))
    out = [[[None] * nh, [None] * nh] for _ in range(ns)]
    for h in range(nh):
        for s in range(ns):
            out[s][0][h] = next(res)
            out[s][1][h] = next(res)
    for s in range(ns):
        o0_ref[s] = jnp.concatenate([o for o, _ in out[s][0]], axis=1)
        o1_ref[s] = jnp.concatenate([o for o, _ in out[s][1]], axis=1)
        for d in range(2):
            for h in range(nh):
                s_ref[s, d, h] = out[s][d][h][1]
                sfin_ref[s, d, h] = out[s][d][h][1]


def _gla(p, st, o_, wup, bgk, bsz, seq, grid_mode):
    nh, dk, dv = st.shape[3:]
    hv = nh * dv
    nc = seq // CH
    ns = SEQ_PER_STEP if bsz % SEQ_PER_STEP == 0 else 1
    grp = bsz // ns
    if grid_mode:
        assert seq // GRID_W == CH and nc == GRID_W
        ncol = p.shape[2] // GRID_W
        p_v = p.reshape(grp, ns, CH, GRID_W * ncol)
        blk = lambda w: (None, ns, CH, w)
        fwd = lambda g, c: (g, 0, 0, c)
        bwd = lambda g, c: (g, 0, 0, nc - 1 - c)
        o_shape = jax.ShapeDtypeStruct((grp, ns, CH, GRID_W * hv), F32)
    else:
        ncol = p.shape[1]
        p_v = p.reshape(grp, ns, nc, CH, ncol)
        blk = lambda w: (None, ns, None, CH, w)
        fwd = lambda g, c: (g, 0, c, 0, 0)
        bwd = lambda g, c: (g, 0, nc - 1 - c, 0, 0)
        o_shape = jax.ShapeDtypeStruct((grp, ns, nc, CH, hv), F32)
    st_in = pl.BlockSpec((ns, None, 2, nh, dk, dv), lambda g, c: (g, o_, 0, 0, 0, 0))
    st_out = pl.BlockSpec((ns, 2, nh, dk, dv), lambda g, c: (g, 0, 0, 0, 0))
    o0, o1, sfin = pl.pallas_call(
        functools.partial(_gla_kernel, c=CH, nh=nh, dk=dk, dv=dv, ns=ns),
        grid=(grp, nc),
        in_specs=[pl.BlockSpec(blk(ncol), fwd), pl.BlockSpec(blk(ncol), bwd), st_in,
                  pl.BlockSpec(wup.shape, lambda g, c: (0, 0, 0)),
                  pl.BlockSpec(bgk.shape, lambda g, c: (0, 0, 0))],
        out_specs=[pl.BlockSpec(blk(hv), fwd), pl.BlockSpec(blk(hv), bwd), st_out],
        out_shape=[o_shape, o_shape, jax.ShapeDtypeStruct((bsz, 2, nh, dk, dv), F32)],
        scratch_shapes=[pltpu.VMEM((ns, 2, nh, dk, dv), F32)],
        compiler_params=_cparams(("arbitrary", "arbitrary"), VMEM_BIG),
        name="gla",
    )(p_v, p_v, st, wup, bgk)
    if grid_mode:
        return o0.reshape(bsz, CH, GRID_W * hv), o1.reshape(bsz, CH, GRID_W * hv), sfin
    return o0.reshape(bsz * seq, hv), o1.reshape(bsz * seq, hv), sfin


def _router_tail(x1, mod, g2n_ref, rw_ref, x_out, h2_out, aff_out):
    x_out[...] = x1
    h2 = _rms(x1, g2n_ref[...]) * (1.0 + mod[4:5]) + mod[3:4]
    h2_out[...] = h2.astype(BF16)
    logits = _dot3(_split(rw_ref[...]), _split(h2), 1, 1)
    m = jnp.max(logits, axis=0, keepdims=True)
    ex = jnp.exp(logits - m)
    aff_out[...] = ex / jnp.sum(ex, axis=0, keepdims=True)


def _out_even_kernel(x_ref, mod_ref, ha0_ref, ha1_ref, ga_ref, y0_ref, y1_ref, g_ref, bonus_ref, lnw_ref, lnb_ref,
                     e_ref, wo_ref, g2n_ref, rw_ref, x_out, h2_out, aff_out, *, da, hd):
    mod = mod_ref[...]
    ya = (ha0_ref[...] + ha1_ref[...]) * _gelu_tanh(ga_ref[...])
    y = y0_ref[...] + y1_ref[...]
    e = e_ref[...]
    mean = _seg_sum(y, e) * (1.0 / hd)
    yc = y - mean
    var = _seg_sum(yc * yc, e) * (1.0 / hd)
    yn = yc * lax.rsqrt(var + GN_EPS_B) * lnw_ref[...] + lnb_ref[...]
    yb = (yn + bonus_ref[...]) * g_ref[...]
    out = _dot_bf(ya, wo_ref[0:da, :]) + _dot_bf(yb, wo_ref[da:, :])
    _router_tail(x_ref[...] + mod[2:3] * out, mod, g2n_ref, rw_ref, x_out, h2_out, aff_out)


def _out_odd_kernel(x_ref, mod_ref, o0_ref, o1_ref, g_ref, gw_ref, wo_ref, g2n_ref, rw_ref,
                    x_out, h2_out, aff_out, *scratch, nh, dv, grid_rows):
    mod = mod_ref[...]
    gw = gw_ref[...]
    if grid_rows:
        hv = nh * dv
        o_scr = scratch[0]
        for c in range(GRID_W):
            for j in range(hv // LANES):
                lanes = slice(c * hv + j * LANES, c * hv + (j + 1) * LANES)
                o_scr[j, pl.ds(c, grid_rows, stride=GRID_W), :] = o0_ref[:, lanes] + o1_ref[:, lanes]
        o_all = jnp.concatenate([o_scr[j] for j in range(hv // LANES)], axis=1)
    else:
        o_all = o0_ref[...] + o1_ref[...]
    parts = []
    for h in range(nh):
        o = o_all[:, h * dv:(h + 1) * dv]
        parts.append(o * lax.rsqrt(jnp.mean(o * o, axis=-1, keepdims=True) + EPS) * gw)
    y = jnp.concatenate(parts, axis=1) * _silu(g_ref[...])
    out = _dot_bf(y, wo_ref[...])
    _router_tail(x_ref[...] + mod[2:3] * out, mod, g2n_ref, rw_ref, x_out, h2_out, aff_out)


def _out_common(kern, x, modr, mod_of, extra_args, extra_specs, norm2_g, router_wt, tb, name, scratch=()):
    n, d = x.shape
    ne = router_wt.shape[0]
    assert n % tb == 0
    mod_blk = mod_of(tb)
    row = lambda i: (i, 0)
    in_specs = ([pl.BlockSpec((tb, d), row), pl.BlockSpec((None, 6, d), lambda i: (mod_blk(i), 0, 0))]
                + extra_specs
                + [pl.BlockSpec((1, d), lambda i: (0, 0)), pl.BlockSpec((ne, d), lambda i: (0, 0))])
    return pl.pallas_call(
        kern, grid=(n // tb,), in_specs=in_specs,
        out_specs=[pl.BlockSpec((tb, d), row), pl.BlockSpec((tb, d), row),
                   pl.BlockSpec((ne, tb), lambda i: (0, i))],
        out_shape=[jax.ShapeDtypeStruct((n, d), F32), jax.ShapeDtypeStruct((n, d), BF16),
                   jax.ShapeDtypeStruct((ne, n), F32)],
        scratch_shapes=list(scratch),
        compiler_params=_cparams(("arbitrary",), VMEM_BIG),
        name=name,
    )(x, modr, *extra_args, norm2_g.reshape(1, d), router_wt)


def _select_kernel(aff_ref, slot_ref, cnt_ref, *, cap, ntb, tokb):
    aff = aff_ref[...]
    ne = aff.shape[0]
    big = jnp.float32(4.0)

    def body(carry):
        lo, hi, _, _ = carry
        inr = (aff > lo) & (aff <= hi)
        mn = jnp.min(jnp.where(inr, aff, big), axis=1, keepdims=True)
        mx = jnp.max(jnp.where(inr, aff, -big), axis=1, keepdims=True)
        done = mn >= mx
        piv = mn + (mx - mn) * 0.5
        piv = jnp.where(piv >= mx, mn, piv)
        ge = jnp.sum((aff > piv).astype(jnp.int32), axis=1, keepdims=True) >= cap
        lo = jnp.where(done | ~ge, lo, piv)
        hi = jnp.where(done | ge, hi, piv)
        return lo, hi, mx, jnp.sum((~done).astype(jnp.int32))

    init = (jnp.full((ne, 1), -1.0, F32), jnp.full((ne, 1), big, F32), jnp.zeros((ne, 1), F32), jnp.int32(1))
    _, _, thr, _ = lax.while_loop(lambda carry: carry[3] > 0, body, init)
    gt = aff > thr
    eq = aff == thr
    need = cap - jnp.sum(gt.astype(jnp.int32), axis=1, keepdims=True)
    si = lax.broadcasted_iota(jnp.int32, (tokb, tokb), 0)
    ti = lax.broadcasted_iota(jnp.int32, (tokb, tokb), 1)
    ustrict = (si < ti).astype(BF16)

    def excl_count(mask_blk, carry):
        m = mask_blk.astype(BF16)
        pre = _dot(m, ustrict).astype(jnp.int32) + carry
        return pre, carry + jnp.sum(mask_blk.astype(jnp.int32), axis=1, keepdims=True)

    ceq = jnp.zeros((ne, 1), jnp.int32)
    csel = jnp.zeros((ne, 1), jnp.int32)
    lane = lax.broadcasted_iota(jnp.int32, (ne, 128), 1)
    cnt_acc = jnp.zeros((ne, 128), jnp.int32)
    for j in range(ntb):
        sl = slice(j * tokb, (j + 1) * tokb)
        eq_rank, ceq = excl_count(eq[:, sl], ceq)
        sel = gt[:, sl] | (eq[:, sl] & (eq_rank < need))
        cnt_acc = jnp.where(lane == j, csel, cnt_acc)
        rank, csel = excl_count(sel, csel)
        slot_ref[:, sl] = jnp.where(sel, rank, -1)
    cnt_acc = jnp.where(lane == ntb, csel, cnt_acc)
    cnt_ref[...] = cnt_acc


def _select(aff_t, cap):
    ne, n = aff_t.shape
    ntb = n // TOK_B
    assert ntb < 128
    return pl.pallas_call(
        functools.partial(_select_kernel, cap=cap, ntb=ntb, tokb=TOK_B),
        grid=(1,),
        in_specs=[pl.BlockSpec((ne, n), lambda i: (0, 0))],
        out_specs=[pl.BlockSpec((ne, n), lambda i: (0, 0)), pl.BlockSpec((ne, 128), lambda i: (0, 0))],
        out_shape=[jax.ShapeDtypeStruct((ne, n), jnp.int32), jax.ShapeDtypeStruct((ne, 128), jnp.int32)],
        compiler_params=_cparams(("arbitrary",)),
        name="select",
    )(aff_t)


def _windows(cnt_ref, tb, ntb, ne, cap, w, rnd):
    out = []
    for e in range(ne):
        first = (cnt_ref[e * (ntb + 1) + tb] // SLOT_ALIGN) * SLOT_ALIGN + rnd * w
        out.append((pl.multiple_of(jnp.minimum(first, cap - w), SLOT_ALIGN), first))
    return out


def _onehot_rows(slot_ref, e, base, first, w, tokb):
    sidx = lax.broadcasted_iota(jnp.int32, (w, tokb), 0) + base
    return (slot_ref[e:e + 1, :] == sidx) & (sidx >= first)


def _gather_kernel(cnt_ref, nr_ref, slot_ref, h2_ref, xe_ref, lhs_ref, *, ntb, ne, cap, w, tokb):
    tb = pl.program_id(0)

    @pl.when(tb == 0)
    def _():
        xe_ref[...] = jnp.zeros_like(xe_ref)

    def body(rnd, carry):
        wins = _windows(cnt_ref, tb, ntb, ne, cap, w, rnd)
        for e, (base, first) in enumerate(wins):
            lhs_ref[e * w:(e + 1) * w, :] = jnp.where(_onehot_rows(slot_ref, e, base, first, w, tokb),
                                                      1.0, 0.0).astype(BF16)
        rows = _dot(lhs_ref[...], h2_ref[...])
        for e, (base, _) in enumerate(wins):
            xe_ref[e, pl.ds(base, w), :] += rows[e * w:(e + 1) * w].astype(BF16)
        return carry

    lax.fori_loop(0, nr_ref[tb], body, 0)


def _gather(cnt, nr, slot_t, h2, cap):
    ntb, ne, tokb = slot_t.shape
    n, d = h2.shape
    return pl.pallas_call(
        functools.partial(_gather_kernel, ntb=ntb, ne=ne, cap=cap, w=SLOT_W, tokb=tokb),
        grid_spec=pltpu.PrefetchScalarGridSpec(
            num_scalar_prefetch=2, grid=(ntb,),
            in_specs=[pl.BlockSpec((None, ne, tokb), lambda t, cnt, nr: (t, 0, 0)),
                      pl.BlockSpec((tokb, d), lambda t, cnt, nr: (t, 0))],
            out_specs=pl.BlockSpec((ne, cap, d), lambda t, cnt, nr: (0, 0, 0), pipeline_mode=pl.Buffered(1)),
            scratch_shapes=[pltpu.VMEM((ne * SLOT_W, tokb), BF16)]),
        out_shape=jax.ShapeDtypeStruct((ne, cap, d), BF16),
        compiler_params=_cparams(("arbitrary",), VMEM_BIG),
        name="moe_gather",
    )(cnt, nr, slot_t, h2)


def _ffn_kernel(xe_ref, wg_ref, wu_ref, wd_ref, ye_ref, acc_ref):
    f = pl.program_id(1)
    xe = xe_ref[...]
    hg = _dot(xe, wg_ref[...].astype(BF16))
    hu = _dot(xe, wu_ref[...].astype(BF16))
    hid = (_silu(hg) * hu).astype(BF16)
    part = _dot(hid, wd_ref[...].astype(BF16))

    @pl.when(f == 0)
    def _():
        acc_ref[...] = part

    @pl.when(f > 0)
    def _():
        acc_ref[...] += part

    @pl.when(f == pl.num_programs(1) - 1)
    def _():
        ye_ref[...] = acc_ref[...].astype(BF16)


def _ffn(xe, w_gate, w_up, w_down, l):
    ne, cap, d = xe.shape
    dff = w_gate.shape[-1]
    nf = dff // FF_B
    return pl.pallas_call(
        _ffn_kernel,
        grid=(ne, nf),
        in_specs=[pl.BlockSpec((None, cap, d), lambda e, f: (e, 0, 0)),
                  pl.BlockSpec((None, None, d, FF_B), lambda e, f: (l, e, 0, f)),
                  pl.BlockSpec((None, None, d, FF_B), lambda e, f: (l, e, 0, f)),
                  pl.BlockSpec((None, None, FF_B, d), lambda e, f: (l, e, f, 0))],
        out_specs=pl.BlockSpec((None, cap, d), lambda e, f: (e, 0, 0)),
        out_shape=jax.ShapeDtypeStruct((ne, cap, d), BF16),
        scratch_shapes=[pltpu.VMEM((cap, d), F32)],
        compiler_params=_cparams(("arbitrary", "arbitrary"), VMEM_BIG),
        name="moe_ffn",
    )(xe, w_gate, w_up, w_down)


def _scatter_kernel(cnt_ref, nr_ref, slot_ref, gate_ref, ye_ref, x_ref, mod_ref, fg_ref, out_ref, g_ref, ywin_ref,
                    *, ntb, ne, cap, w, tokb, final):
    tb = pl.program_id(0)
    out_ref[...] = jnp.zeros_like(out_ref)

    def body(rnd, carry):
        wins = _windows(cnt_ref, tb, ntb, ne, cap, w, rnd)
        for e, (base, first) in enumerate(wins):
            g_ref[e * w:(e + 1) * w, :] = jnp.where(_onehot_rows(slot_ref, e, base, first, w, tokb),
                                                    gate_ref[e:e + 1, :], 0.0).astype(BF16)
            ywin_ref[e * w:(e + 1) * w, :] = ye_ref[e, pl.ds(base, w), :]
        out_ref[...] += _dot(g_ref[...], ywin_ref[...], 0, 0)
        return carry

    lax.fori_loop(0, nr_ref[tb], body, 0)
    x = x_ref[...] + mod_ref[...][5:6] * out_ref[...]
    out_ref[...] = _rms(x, fg_ref[...]) if final else x


def _scatter(cnt, nr, slot_t, gate_t, ye, x, modr, mod_tok, final_gain, final):
    ne, cap, d = ye.shape
    ntb, _, tokb = slot_t.shape
    n = ntb * tokb
    tok = pl.BlockSpec((None, ne, tokb), lambda t, cnt, nr: (t, 0, 0))
    row = pl.BlockSpec((tokb, d), lambda t, cnt, nr: (t, 0))
    return pl.pallas_call(
        functools.partial(_scatter_kernel, ntb=ntb, ne=ne, cap=cap, w=SLOT_W, tokb=tokb, final=final),
        grid_spec=pltpu.PrefetchScalarGridSpec(
            num_scalar_prefetch=2, grid=(ntb,),
            in_specs=[tok, tok,
                      pl.BlockSpec((ne, cap, d), lambda t, cnt, nr: (0, 0, 0), pipeline_mode=pl.Buffered(1)),
                      row,
                      pl.BlockSpec((None, 6, d), lambda t, cnt, nr: (mod_tok(t), 0, 0)),
                      pl.BlockSpec((1, d), lambda t, cnt, nr: (0, 0))],
            out_specs=row,
            scratch_shapes=[pltpu.VMEM((ne * SLOT_W, tokb), BF16), pltpu.VMEM((ne * SLOT_W, d), BF16)]),
        out_shape=jax.ShapeDtypeStruct((n, d), F32),
        compiler_params=_cparams(("arbitrary",), VMEM_BIG),
        name="moe_scatter",
    )(cnt, nr, slot_t, gate_t, ye, x, modr, final_gain.reshape(1, d))


def _moe(h2, aff_t, w_gate, w_up, w_down, l, x, modr, mod_tok, final_gain, final):
    n, d = h2.shape
    ne = aff_t.shape[0]
    cap = max(1, CAP_FACTOR * n // ne)
    assert cap % SLOT_W == 0 and n % TOK_B == 0
    ntb = n // TOK_B
    slot, cnt = _select(aff_t, cap)
    cnt = cnt[:, :ntb + 1]
    span = cnt[:, 1:] - (cnt[:, :-1] // SLOT_ALIGN) * SLOT_ALIGN
    nr = jnp.max(jnp.where(cnt[:, 1:] > cnt[:, :-1], (span + SLOT_W - 1) // SLOT_W, 0), axis=0).astype(jnp.int32)
    slot_t = slot.reshape(ne, ntb, TOK_B).transpose(1, 0, 2)
    gate_t = aff_t.reshape(ne, ntb, TOK_B).transpose(1, 0, 2)
    xe = _gather(cnt.reshape(-1), nr, slot_t, h2, cap)
    ye = _ffn(xe, w_gate, w_up, w_down, l)
    return _scatter(cnt.reshape(-1), nr, slot_t, gate_t, ye, x, modr, mod_tok, final_gain, final)


def _block_diag(w):
    h, n, _ = w.shape
    eye = jnp.eye(h, dtype=w.dtype)
    return (eye[:, None, :, None] * w[:, :, None, :]).reshape(h * n, h * n)


def _lora_cat(w2):
    _, l, c = w2.shape
    z = jnp.zeros((l, c), w2.dtype)
    return jnp.concatenate([jnp.concatenate([w2[0], z], 1), jnp.concatenate([z, w2[1]], 1)], 0)


def _trunk(x, bsz, seq, cond_base, per_batch, st_rglru, st_rwkv, st_gla, grid_mode, modr, P):
    d_model = x.shape[1]
    depth = P['norm1_g'].shape[0]
    new_rglru, new_rwkv, new_gla = [], [], []
    for l in range(depth):
        if per_batch:
            mod_of = lambda tb, l=l: (lambda i: l * 8 + cond_base + (i * tb) // seq)
        else:
            mod_of = lambda tb, l=l: (lambda i: l * 8 + cond_base)
        router_wt = P['router_w'][l].T
        if l % 2 == 0:
            e = l // 2
            da = P['a_conv_w'].shape[-1]
            db = P['b_k_k'].shape[-1]
            w_in = P['ev_w_in'][e].astype(BF16)
            nb_cols = w_in.shape[1] - 2 * da
            pa, pb = _inproj(x, modr, mod_of, P['norm1_g'][l], w_in, (2 * da, nb_cols), bsz, seq)
            ha, sa = [], []
            for d in range(2):
                wg = jnp.concatenate([_block_diag(P['a_w_rg'][e, d]), _block_diag(P['a_w_ig'][e, d])], 1).astype(BF16)
                bg = jnp.concatenate([P['a_b_rg'][e, d], P['a_b_ig'][e, d]])
                h_d, hf = _rglru(pa, P['a_conv_w'][e], P['a_conv_b'][e], wg, bg, P['a_lam'][e, d],
                                 st_rglru[:, e, d], d, bsz, seq)
                ha.append(h_d)
                sa.append(hf.reshape(bsz, da))
            new_rglru.append(jnp.stack(sa, 1))
            e_bf = _block_diag(jnp.ones((db // HEAD_B, HEAD_B, HEAD_B), BF16))
            r, v, kkn, g, bonus, lw, bt, kd = _rwkv_prep(
                pb, P['b_mu'][e], P['b_w0'][e].reshape(1, 2 * db), _lora_cat(P['b_w2'][e]),
                P['b_a0'][e].reshape(1, 2 * db), _lora_cat(P['b_a2'][e]), P['b_g2'][e].astype(BF16),
                P['b_k_k'][e], P['b_k_a'][e], P['b_r_k'][e], e_bf, bsz, seq)
            y0, y1, sfin = _rwkv_scan(r, kkn, v, lw, bt, kd, st_rwkv, e, bsz, seq)
            ys = [y0, y1]
            new_rwkv.append(sfin)
            tb = _tok_block(False)
            rowh = pl.BlockSpec((tb, da), lambda i: (i, 0))
            rowg = pl.BlockSpec((tb, da), lambda i: (i, 1))
            vec = pl.BlockSpec((1, db), lambda i: (0, 0))
            x, h2, aff_t = _out_common(
                functools.partial(_out_even_kernel, da=da, hd=HEAD_B), x, modr, mod_of,
                [ha[0], ha[1], pa, ys[0], ys[1], g, bonus, P['b_ln_w'][e].reshape(1, db), P['b_ln_b'][e].reshape(1, db),
                 e_bf, P['ev_w_out'][e].astype(BF16)],
                [rowh, rowh, rowg, rowh, rowh, rowh, rowh, vec, vec,
                 pl.BlockSpec((db, db), lambda i: (0, 0)),
                 pl.BlockSpec((da + db, d_model), lambda i: (0, 0))],
                P['norm2_g'][l], router_wt, tb, "out_even")
        else:
            o_ = l // 2
            w_in = P['od_w_in'][o_]
            nh, dk, dv = st_gla.shape[3:]
            hv = nh * dv
            qkv = 2 * nh * dk + hv
            gin = -(2 * GATE_RANK) % 128 + 2 * GATE_RANK
            w_in = jnp.concatenate([w_in[:, :qkv], w_in[:, qkv + hv:],
                                    jnp.zeros((d_model, gin - 2 * GATE_RANK), w_in.dtype),
                                    w_in[:, qkv:qkv + hv]], axis=1).astype(BF16)
            p, g = _inproj(x, modr, mod_of, P['norm1_g'][l], w_in, (qkv + gin, hv), bsz, seq, grid_view=grid_mode)
            wup = jnp.stack([jnp.zeros((gin, nh * dk), F32).at[d * GATE_RANK:(d + 1) * GATE_RANK]
                             .set(P['c_w_gk_up'][o_, d]) for d in range(2)])
            o0, o1, sfin = _gla(p, st_gla, o_, wup, P['c_b_gk'][o_].reshape(2, 1, nh * dk), bsz, seq, grid_mode)
            new_gla.append(sfin)
            tb = _tok_block(grid_mode)
            nt = seq // tb
            if grid_mode:
                rowo = pl.BlockSpec((None, SUBLANES, GRID_W * hv), lambda i: (i // nt, i % nt, 0))
                scratch = [pltpu.VMEM((hv // LANES, tb, LANES), F32)]
            else:
                rowo = pl.BlockSpec((tb, hv), lambda i: (i, 0))
                scratch = []
            rowg = pl.BlockSpec((tb, hv), lambda i: (i, 0))
            x, h2, aff_t = _out_common(
                functools.partial(_out_odd_kernel, nh=nh, dv=dv, grid_rows=SUBLANES if grid_mode else 0),
                x, modr, mod_of,
                [o0, o1, g, P['c_gnorm_w'][o_].reshape(1, dv), P['od_w_out'][o_].astype(BF16)],
                [rowo, rowo, rowg, pl.BlockSpec((1, dv), lambda i: (0, 0)),
                 pl.BlockSpec((hv, d_model), lambda i: (0, 0))],
                P['norm2_g'][l], router_wt, tb, "out_odd", scratch=scratch)
        x = _moe(h2, aff_t, P['moe_w_gate'], P['moe_w_up'], P['moe_w_down'], l, x, modr, mod_of(TOK_B),
                 P['final_norm_g'], l == depth - 1)
    return x, jnp.stack(new_rglru, 1), jnp.stack(new_rwkv, 1), jnp.stack(new_gla, 1)


def kernel(x_prompt, x_sample, state_rglru, state_rwkv, state_gla, c, c_ctx, norm1_g, norm2_g, ada_w, ada_b, router_w, moe_w_gate, moe_w_up, moe_w_down, ev_w_in, ev_w_out, a_conv_w, a_conv_b, a_w_rg, a_b_rg, a_w_ig, a_b_ig, a_lam, b_mu, b_w0, b_w2, b_a0, b_a2, b_g2, b_k_k, b_k_a, b_r_k, b_ln_w, b_ln_b, od_w_in, od_w_out, c_w_gk_up, c_b_gk, c_gnorm_w, final_norm_g):
    P = dict(norm1_g=norm1_g, norm2_g=norm2_g, router_w=router_w,
             moe_w_gate=moe_w_gate, moe_w_up=moe_w_up, moe_w_down=moe_w_down,
             ev_w_in=ev_w_in, ev_w_out=ev_w_out, a_conv_w=a_conv_w, a_conv_b=a_conv_b,
             a_w_rg=a_w_rg, a_b_rg=a_b_rg, a_w_ig=a_w_ig, a_b_ig=a_b_ig, a_lam=a_lam,
             b_mu=b_mu, b_w0=b_w0, b_w2=b_w2, b_a0=b_a0, b_a2=b_a2, b_g2=b_g2, b_k_k=b_k_k,
             b_k_a=b_k_a, b_r_k=b_r_k, b_ln_w=b_ln_w, b_ln_b=b_ln_b,
             od_w_in=od_w_in, od_w_out=od_w_out, c_w_gk_up=c_w_gk_up, c_b_gk=c_b_gk,
             c_gnorm_w=c_gnorm_w, final_norm_g=final_norm_g)
    bp, tp, d_model = x_prompt.shape
    bs, ts, _ = x_sample.shape
    assert bs + 1 <= 8 and tp % TB == 0 and ts % TB == 0
    depth = ada_w.shape[0]
    cv8 = jnp.zeros((8, d_model), F32).at[0].set(c_ctx).at[1:1 + bs].set(c)
    modr = _modulation(cv8, ada_w, ada_b).reshape(depth * 8, 6, d_model)
    dt = x_prompt.dtype
    z_rglru = jnp.zeros((bp,) + state_rglru.shape[1:], dt)
    z_rwkv = jnp.zeros((bp,) + state_rwkv.shape[1:], dt)
    z_gla = jnp.zeros((bp,) + state_gla.shape[1:], dt)
    y_prompt, n_rglru, n_rwkv, n_gla = _trunk(x_prompt.reshape(bp * tp, d_model), bp, tp, 0, False,
                                              z_rglru, z_rwkv, z_gla, False, modr, P)
    y_sample, _, _, _ = _trunk(x_sample.reshape(bs * ts, d_model), bs, ts, 1, True,
                               state_rglru, state_rwkv, state_gla, True, modr, P)
    return (y_prompt.reshape(bp, tp, d_model), y_sample.reshape(bs, ts, d_model), n_rglru, n_rwkv, n_gla)
```

```python
import functools
import math

import jax
import jax.numpy as jnp
from jax import lax
from jax.experimental import pallas as pl
from jax.experimental.pallas import tpu as pltpu

F32 = jnp.float32
BF16 = jnp.bfloat16
HI = lax.Precision.HIGHEST

GRID_W = 64
EPS = 1e-6
C_RGLRU = 8.0
GN_EPS_B = 64e-5
GATE_NORM = 16.0
CAP_FACTOR = 2
HEAD_B = 64
GATE_RANK = 16

SUBLANES = 8
LANES = 128
MXU_N = 256
PITCH_PAD = 4
TB = 256
CH = 64
SUB = 16
SEQ_PER_STEP = 2
SLOT_W = 128
SLOT_ALIGN = 16
TOK_B = 512
FF_B = 512
VMEM_BIG = 56 * 1024 * 1024


def _dot(a, b, ca=1, cb=0, prec=None):
    return lax.dot_general(a, b, (((ca,), (cb,)), ((), ())), precision=prec,
                           preferred_element_type=F32)


def _dot_bf(a, b, ca=1, cb=0):
    return _dot(a.astype(BF16), b.astype(BF16), ca, cb)


def _dot_hi(a, b, ca=1, cb=0):
    return _dot(a.astype(F32), b.astype(F32), ca, cb, HI)


def _seg_sum(x, e_bf):
    hi = x.astype(BF16)
    lo = (x - hi.astype(F32)).astype(BF16)
    return _dot(hi, e_bf) + _dot(lo, e_bf)


def _mask_dot(mask_bf, x, cm=1, cx=0):
    hi = x.astype(BF16)
    r1 = x - hi.astype(F32)
    mid = r1.astype(BF16)
    lo = (r1 - mid.astype(F32)).astype(BF16)
    return _dot(mask_bf, hi, cm, cx) + _dot(mask_bf, mid, cm, cx) + _dot(mask_bf, lo, cm, cx)


def _sigmoid(x):
    return 1.0 / (1.0 + jnp.exp(-x))


def _softplus(x):
    return jnp.maximum(x, 0.0) + jnp.log1p(jnp.exp(-jnp.abs(x)))


def _silu(x):
    return x * _sigmoid(x)


def _gelu_tanh(x):
    return 0.5 * x * (1.0 + jnp.tanh(math.sqrt(2.0 / math.pi) * (x + 0.044715 * (x * x * x))))


def _rms(x, g):
    return x * lax.rsqrt(jnp.mean(x * x, axis=-1, keepdims=True) + EPS) * g


def _cparams(sem, vmem=None):
    return pltpu.CompilerParams(dimension_semantics=sem, vmem_limit_bytes=vmem)


def _mod_kernel(c_ref, w_ref, b_ref, o_ref):
    o_ref[...] = _dot_hi(_silu(c_ref[...]), w_ref[...]) + b_ref[...]


def _modulation(cv8, ada_w, ada_b):
    depth, d, n6 = ada_w.shape
    nb = 1536
    return pl.pallas_call(
        _mod_kernel,
        grid=(depth, n6 // nb),
        in_specs=[pl.BlockSpec((8, d), lambda l, j: (0, 0)),
                  pl.BlockSpec((None, d, nb), lambda l, j: (l, 0, j)),
                  pl.BlockSpec((None, 1, nb), lambda l, j: (l, 0, j))],
        out_specs=pl.BlockSpec((None, 8, nb), lambda l, j: (l, 0, j)),
        out_shape=jax.ShapeDtypeStruct((depth, 8, n6), F32),
        compiler_params=_cparams(("arbitrary", "arbitrary")),
        name="modulation",
    )(cv8, ada_w, ada_b.reshape(depth, 1, n6))


def _project(x, mod, g_ref, w_ref, outs, splits, grid_rows):
    h = _rms(x, g_ref[...]) * (1.0 + mod[1:2]) + mod[0:1]
    p = _dot_bf(h, w_ref[...])
    lo = 0
    if grid_rows:
        n0 = splits[0]
        scr = outs[-1]
        pitch = GRID_W + PITCH_PAD
        for j in range(n0 // LANES):
            for r in range(grid_rows):
                scr[j, r * pitch:r * pitch + GRID_W, :] = p[r * GRID_W:(r + 1) * GRID_W, j * LANES:(j + 1) * LANES]
        for c in range(GRID_W):
            for j in range(n0 // LANES):
                outs[0][:, c * n0 + j * LANES:c * n0 + (j + 1) * LANES] = scr[j, pl.ds(c, grid_rows, stride=pitch), :]
        lo, outs, splits = n0, outs[1:-1], splits[1:]
    for o_ref, n in zip(outs, splits):
        o_ref[...] = p[:, lo:lo + n]
        lo += n


def _inproj_kernel(x_ref, mod_ref, g_ref, w_ref, *outs, splits, grid_rows):
    _project(x_ref[...], mod_ref[...], g_ref, w_ref, outs, splits, grid_rows)


def _tok_block(grid_mode):
    return SUBLANES * GRID_W if grid_mode else TOK_B


def _inproj(x, modr, mod_of, gain, w_bf, splits, bsz, seq, grid_view=False):
    n, d = x.shape
    tb = _tok_block(grid_view)
    assert n % tb == 0
    nt = seq // tb
    ncols = w_bf.shape[1]
    mod_blk = mod_of(tb)
    row = lambda i: (i, 0)
    in_specs = [pl.BlockSpec((tb, d), row),
                pl.BlockSpec((None, 6, d), lambda i: (mod_blk(i), 0, 0)),
                pl.BlockSpec((1, d), lambda i: (0, 0)),
                pl.BlockSpec((d, ncols), lambda i: (0, 0), pipeline_mode=pl.Buffered(1))]
    args = [x, modr, gain.reshape(1, d), w_bf]
    out_shape = [jax.ShapeDtypeStruct((n, c), F32) for c in splits]
    out_specs = [pl.BlockSpec((tb, c), row) for c in splits]
    scratch = []
    grid_rows = 0
    if grid_view:
        grid_rows = tb // GRID_W
        out_shape[0] = jax.ShapeDtypeStruct((bsz, seq // GRID_W, GRID_W * splits[0]), F32)
        out_specs[0] = pl.BlockSpec((None, grid_rows, GRID_W * splits[0]), lambda i: (i // nt, i % nt, 0))
        scratch = [pltpu.VMEM((splits[0] // LANES, grid_rows * (GRID_W + PITCH_PAD), LANES), F32)]
    return pl.pallas_call(
        functools.partial(_inproj_kernel, splits=splits, grid_rows=grid_rows),
        grid=(n // tb,), in_specs=in_specs, out_specs=out_specs, out_shape=out_shape,
        scratch_shapes=scratch,
        compiler_params=_cparams(("arbitrary",), VMEM_BIG),
        name="inproj",
    )(*args)


def _fill_ext(ext_ref, x, prev_ref, next_ref, first, last, tb):
    ext_ref[0:8, :] = jnp.where(first, 0.0, prev_ref[...])
    ext_ref[8:8 + tb, :] = x
    ext_ref[8 + tb:16 + tb, :] = jnp.where(last, 0.0, next_ref[...])


def _rglru_kernel(xa_ref, prev_ref, next_ref, cw_ref, cb_ref, wg_ref, bg_ref, lam_ref, h0_ref,
                  h_ref, hfin_ref, ext_ref, carry_ref, *, d, nt, tb):
    i = pl.program_id(1)
    t = i if d == 0 else nt - 1 - i
    first = t == 0
    last = t == nt - 1
    x = xa_ref[...]
    da = x.shape[1]
    _fill_ext(ext_ref, x, prev_ref, next_ref, first, last, tb)
    cw = cw_ref[...]
    u = (cw[0:1] * ext_ref[7:7 + tb, :] + cw[1:2] * x + cw[2:3] * ext_ref[9:9 + tb, :]
         + cw[3:4] * ext_ref[10:10 + tb, :] + cb_ref[...])
    gts = _dot_bf(u, wg_ref[...]) + bg_ref[...]
    gate_r = _sigmoid(gts[:, :da])
    gate_i = _sigmoid(gts[:, da:])
    log_a = -C_RGLRU * gate_r * _softplus(-lam_ref[...])
    a = jnp.exp(log_a)
    xin = jnp.sqrt(1.0 - jnp.exp(2.0 * log_a)) * gate_i * u

    @pl.when(i == 0)
    def _():
        carry_ref[...] = h0_ref[...]

    rows = lax.broadcasted_iota(jnp.int32, (tb, da), 0)
    s = 1
    while s < tb:
        if d == 0:
            a_sh = pltpu.roll(a, s, 0)
            x_sh = pltpu.roll(xin, s, 0)
            valid = rows >= s
        else:
            a_sh = pltpu.roll(a, tb - s, 0)
            x_sh = pltpu.roll(xin, tb - s, 0)
            valid = rows < tb - s
        xin = jnp.where(valid, a * x_sh + xin, xin)
        a = jnp.where(valid, a * a_sh, a)
        s *= 2
    h = xin + a * carry_ref[...]
    h_ref[...] = h
    newc = h[tb - 1:tb, :] if d == 0 else h[0:1, :]
    carry_ref[...] = newc
    hfin_ref[...] = newc


def _rglru(pa, conv_w, conv_b, wg_d, bg_d, lam_d, h0_d, d, bsz, seq):
    n = pa.shape[0]
    da = conv_w.shape[1]
    nt = seq // TB
    r8 = TB // 8
    tt = (lambda t: t) if d == 0 else (lambda t: nt - 1 - t)
    blk = lambda b, t: b * nt + tt(t)
    return pl.pallas_call(
        functools.partial(_rglru_kernel, d=d, nt=nt, tb=TB),
        grid=(bsz, nt),
        in_specs=[pl.BlockSpec((TB, da), lambda b, t: (blk(b, t), 0)),
                  pl.BlockSpec((8, da), lambda b, t: (jnp.maximum(blk(b, t) * r8 - 1, 0), 0)),
                  pl.BlockSpec((8, da), lambda b, t: (jnp.minimum((blk(b, t) + 1) * r8, n // 8 - 1), 0)),
                  pl.BlockSpec((4, da), lambda b, t: (0, 0)),
                  pl.BlockSpec((1, da), lambda b, t: (0, 0)),
                  pl.BlockSpec((da, 2 * da), lambda b, t: (0, 0)),
                  pl.BlockSpec((1, 2 * da), lambda b, t: (0, 0)),
                  pl.BlockSpec((1, da), lambda b, t: (0, 0)),
                  pl.BlockSpec((None, 1, da), lambda b, t: (b, 0, 0))],
        out_specs=[pl.BlockSpec((TB, da), lambda b, t: (blk(b, t), 0)),
                   pl.BlockSpec((None, 1, da), lambda b, t: (b, 0, 0))],
        out_shape=[jax.ShapeDtypeStruct((n, da), F32), jax.ShapeDtypeStruct((bsz, 1, da), F32)],
        scratch_shapes=[pltpu.VMEM((TB + 16, da), F32), pltpu.VMEM((1, da), F32)],
        compiler_params=_cparams(("arbitrary", "arbitrary")),
        name=f"rglru_d{d}",
    )(pa, pa, pa, conv_w, conv_b.reshape(1, da), wg_d, bg_d.reshape(1, 2 * da), lam_d.reshape(1, da),
      h0_d.reshape(bsz, 1, da))


def _rwkv_prep_kernel(pb_ref, prev_ref, next_ref, mu_ref, w0_ref, w2_ref, a0_ref, a2_ref, g2_ref,
                      kk_ref, ka_ref, rk_ref, e_ref,
                      r_out, v_out, kk_out, g_out, bonus_out, lw_out, bt_out, kd_out, ext_ref, *, nt, tb, db, lw_, la_):
    t = pl.program_id(1)
    x = pb_ref[...]
    _fill_ext(ext_ref, x, prev_ref, next_ref, t == 0, t == nt - 1, tb)
    sh = 0.5 * (ext_ref[7:7 + tb, :] + ext_ref[9:9 + tb, :])
    z = x + mu_ref[...] * (sh - x)
    r = z[:, 0:db]
    k = z[:, db:2 * db]
    v = z[:, 2 * db:3 * db]
    o = 3 * db
    zw = z[:, o:o + 2 * lw_]
    za = z[:, o + 2 * lw_:o + 2 * lw_ + 2 * la_]
    zg = z[:, o + 2 * lw_ + 2 * la_:]
    w = -_softplus(-(w0_ref[...] + _dot3(_split(jnp.tanh(zw)), _split(w2_ref[...])))) - 0.5
    lw = -jnp.exp(w)
    a = _sigmoid(a0_ref[...] + _dot3(_split(za), _split(a2_ref[...])))
    g = _dot_bf(_sigmoid(zg), g2_ref[...])
    e = e_ref[...]
    kkv = k * kk_ref[...]
    ss = _seg_sum(kkv * kkv, e)
    kkn = kkv / jnp.maximum(jnp.sqrt(ss), 1e-12)
    ka = ka_ref[...]
    kd0 = k * (1.0 + (a[:, :db] - 1.0) * ka)
    kd1 = k * (1.0 + (a[:, db:] - 1.0) * ka)
    bonus = _seg_sum(r * rk_ref[...] * (kd0 + kd1), e) * v
    r_out[...] = r
    v_out[...] = v
    kk_out[...] = kkn
    g_out[...] = g
    bonus_out[...] = bonus
    lw_out[0] = lw[:, :db]
    lw_out[1] = lw[:, db:]
    bt_out[0] = kkn * a[:, :db]
    bt_out[1] = kkn * a[:, db:]
    kd_out[0] = kd0
    kd_out[1] = kd1


def _rwkv_prep(pb, mu, w0c, w2c, a0c, a2c, g2, k_k, k_a, r_k, e_bf, bsz, seq):
    n, ncol = pb.shape
    db = k_k.shape[-1]
    nt = seq // TB
    r8 = TB // 8
    blk = lambda b, t: b * nt + t
    full = lambda shp: pl.BlockSpec(shp, lambda b, t: (0,) * len(shp))
    row = pl.BlockSpec((TB, db), lambda b, t: (blk(b, t), 0))
    row2 = pl.BlockSpec((2, TB, db), lambda b, t: (0, blk(b, t), 0))
    s1 = jax.ShapeDtypeStruct((n, db), F32)
    s2 = jax.ShapeDtypeStruct((2, n, db), F32)
    lw_ = w2c.shape[0] // 2
    la_ = a2c.shape[0] // 2
    return pl.pallas_call(
        functools.partial(_rwkv_prep_kernel, nt=nt, tb=TB, db=db, lw_=lw_, la_=la_),
        grid=(bsz, nt),
        in_specs=[pl.BlockSpec((TB, ncol), lambda b, t: (blk(b, t), 0)),
                  pl.BlockSpec((8, ncol), lambda b, t: (jnp.maximum(blk(b, t) * r8 - 1, 0), 0)),
                  pl.BlockSpec((8, ncol), lambda b, t: (jnp.minimum((blk(b, t) + 1) * r8, n // 8 - 1), 0)),
                  full((1, ncol)), full((1, 2 * db)), full(w2c.shape), full((1, 2 * db)), full(a2c.shape),
                  full(g2.shape), full((1, db)), full((1, db)), full((1, db)), full((db, db))],
        out_specs=[row, row, row, row, row, row2, row2, row2],
        out_shape=[s1, s1, s1, s1, s1, s2, s2, s2],
        scratch_shapes=[pltpu.VMEM((TB + 16, ncol), F32)],
        compiler_params=_cparams(("arbitrary", "arbitrary"), VMEM_BIG),
        name="rwkv_prep",
    )(pb, pb, pb, mu.reshape(1, ncol), w0c, w2c, a0c, a2c, g2, k_k.reshape(1, db), k_a.reshape(1, db),
      r_k.reshape(1, db), e_bf)


def _tri_masks(c, d):
    ti = lax.broadcasted_iota(jnp.int32, (c, c), 0)
    si = lax.broadcasted_iota(jnp.int32, (c, c), 1)
    if d == 0:
        return si < ti, si <= ti, ti == si
    return si > ti, si >= ti, ti == si


def _split(x):
    hi = x.astype(BF16)
    return hi, (x - hi.astype(F32)).astype(BF16)


def _cut(p, sl):
    return p[0][sl], p[1][sl]


def _dot3(a, b, ca=1, cb=0):
    m = a[0].shape[1 - ca]
    n = b[0].shape[1 - cb]
    if not (ca == 1 and cb == 1 and 2 * n == MXU_N):
        return _dot(a[0], b[0], ca, cb) + _dot(a[1], b[0], ca, cb) + _dot(a[0], b[1], ca, cb)
    q = _dot(jnp.concatenate(a, axis=0), jnp.concatenate(b, axis=0), ca, cb)
    return (q[:m, :n] + q[:m, n:]) + (q[m:, :n] + q[m:, n:])


def _staged(gens):
    results = [None] * len(gens)
    live = list(range(len(gens)))
    while live:
        still = []
        for i in live:
            try:
                out = next(gens[i])
            except StopIteration:
                continue
            if out is not None:
                results[i] = out
            still.append(i)
        live = still
    return results


def _rwkv_chunk(d, c, nh, hd, r, kk, v, lw, bt, kd, s_heads):
    strict, incl, diag = _tri_masks(c, d)
    cs = _mask_dot(incl.astype(BF16), lw)
    tot = cs[c - 1:c] if d == 0 else cs[0:1]
    einv = jnp.exp(-cs)
    eend = jnp.exp(tot - cs)
    gtot = jnp.exp(tot)
    a_t = _split(-kk * jnp.exp(cs - lw))
    r_t = _split(r * jnp.exp(cs))
    b_t = _split(bt * einv)
    k_t = _split(kd * einv)
    b_g = _split(bt * eend)
    k_g = _split(kd * eend)
    vs = _split(v)
    eye = diag.astype(F32)
    t2 = lax.broadcasted_iota(jnp.int32, (c, 2 * c), 0)
    s2 = lax.broadcasted_iota(jnp.int32, (c, 2 * c), 1)
    s2 = jnp.where(s2 >= c, s2 - c, s2)
    incl2 = (s2 <= t2) if d == 0 else (s2 >= t2)
    nsq = int(round(math.log2(SUB))) - 1
    ti = lax.broadcasted_iota(jnp.int32, (c, c), 0)
    si = lax.broadcasted_iota(jnp.int32, (c, c), 1)
    blk_masks = [ti // SUB == si // SUB]
    w = SUB
    while w < c:
        blk_masks.append((ti // (2 * w) == si // (2 * w)) & (ti // w != si // w))
        w *= 2
    def head(h):
        sl = (slice(None), slice(h * hd, (h + 1) * hd))
        s0 = s_heads[h]
        s0s = _split(s0)
        ar = tuple(jnp.concatenate([x[sl], y[sl]], axis=0) for x, y in zip(a_t, r_t))
        bk = tuple(jnp.concatenate([x[sl], y[sl]], axis=0) for x, y in zip(b_t, k_t))
        gm = _dot3(ar, bk, 1, 1)
        yield
        a_ab = jnp.where(strict, gm[:c, :c], 0.0)
        a_ak = jnp.where(strict, gm[:c, c:], 0.0)
        ars =_dot3(ar, s0s, 1, 1)
        yield
        vh = _cut(vs, sl)
        x = ars[:c] + _dot3(_split(a_ak), vh)
        yield
        nd = jnp.where(blk_masks[0], a_ab, 0.0)
        tinv = eye + nd
        p = _split(nd)
        p = _split(_dot3(p, p))
        yield
        tinv = tinv + _dot3(_split(tinv), p)
        yield
        pb = p[0]
        for _ in range(nsq - 1):
            pb = _dot(pb, pb).astype(BF16)
            yield
            tinv = tinv + _dot(tinv.astype(BF16), pb)
            yield
        for m in blk_masks[1:]:
            tb16 = tinv.astype(BF16)
            half = _dot(tb16, jnp.where(m, a_ab, 0.0).astype(BF16))
            yield
            tinv = tinv + _dot(half.astype(BF16), tb16)
            yield
        u = _dot3(_split(tinv), _split(x))
        yield
        us = _split(u)
        uv = tuple(jnp.concatenate([x, y], axis=0) for x, y in zip(us, vh))
        y_out = ars[c:] + _dot(jnp.where(incl2, gm[c:], 0.0).astype(BF16), uv[0])
        yield
        bkg = tuple(jnp.concatenate([x[sl], y[sl]], axis=0) for x, y in zip(b_g, k_g))
        s_out = s0 * gtot[sl] + _dot3(uv, bkg, 0, 0)
        yield (y_out, s_out)

    return [head(h) for h in range(nh)]


def _rwkv_scan_kernel(r0_ref, kk0_ref, v0_ref, lw0_ref, bt0_ref, kd0_ref,
                      r1_ref, kk1_ref, v1_ref, lw1_ref, bt1_ref, kd1_ref, s0_ref,
                      y0_ref, y1_ref, sfin_ref, s_ref, *, c, nh, hd, ns):
    @pl.when(pl.program_id(1) == 0)
    def _():
        s_ref[...] = s0_ref[...]

    gens = []
    for s in range(ns):
        ga = _rwkv_chunk(0, c, nh, hd, r0_ref[s], kk0_ref[s], v0_ref[s], lw0_ref[s], bt0_ref[s], kd0_ref[s],
                         [s_ref[s, 0, h] for h in range(nh)])
        gb = _rwkv_chunk(1, c, nh, hd, r1_ref[s], kk1_ref[s], v1_ref[s], lw1_ref[s], bt1_ref[s], kd1_ref[s],
                         [s_ref[s, 1, h] for h in range(nh)])
        gens.append((ga, gb))
    order = [g for h in range(nh) for ga, gb in gens for g in (ga[h], gb[h])]
    res = iter(_staged(order))
    out = [[[None] * nh, [None] * nh] for _ in range(ns)]
    for h in range(nh):
        for s in range(ns):
            out[s][0][h] = next(res)
            out[s][1][h] = next(res)
    for s in range(ns):
        y0_ref[s] = jnp.concatenate([y for y, _ in out[s][0]], axis=1)
        y1_ref[s] = jnp.concatenate([y for y, _ in out[s][1]], axis=1)
        for d in range(2):
            for h in range(nh):
                s_ref[s, d, h] = out[s][d][h][1]
                sfin_ref[s, d, h] = out[s][d][h][1]


def _rwkv_scan(r, kkn, v, lw, bt, kd, st, e, bsz, seq):
    n, db = r.shape
    nh = db // HEAD_B
    nc = seq // CH
    ns = SEQ_PER_STEP if bsz % SEQ_PER_STEP == 0 else 1
    grp = bsz // ns
    tok = lambda a: a.reshape(grp, ns, seq, db)
    tok2 = lambda a: a.reshape(2, grp, ns, seq, db)
    row0 = pl.BlockSpec((None, ns, CH, db), lambda g, c: (g, 0, c, 0))
    row1 = pl.BlockSpec((None, ns, CH, db), lambda g, c: (g, 0, nc - 1 - c, 0))
    dir0 = pl.BlockSpec((None, None, ns, CH, db), lambda g, c: (0, g, 0, c, 0))
    dir1 = pl.BlockSpec((None, None, ns, CH, db), lambda g, c: (1, g, 0, nc - 1 - c, 0))
    st_in = pl.BlockSpec((ns, None, 2, nh, HEAD_B, HEAD_B), lambda g, c: (g, e, 0, 0, 0, 0))
    st_out = pl.BlockSpec((ns, 2, nh, HEAD_B, HEAD_B), lambda g, c: (g, 0, 0, 0, 0))
    y_shape = jax.ShapeDtypeStruct((grp, ns, seq, db), F32)
    y0, y1, sfin = pl.pallas_call(
        functools.partial(_rwkv_scan_kernel, c=CH, nh=nh, hd=HEAD_B, ns=ns),
        grid=(grp, nc),
        in_specs=[row0, row0, row0, dir0, dir0, dir0, row1, row1, row1, dir1, dir1, dir1, st_in],
        out_specs=[row0, row1, st_out],
        out_shape=[y_shape, y_shape, jax.ShapeDtypeStruct((bsz, 2, nh, HEAD_B, HEAD_B), F32)],
        scratch_shapes=[pltpu.VMEM((ns, 2, nh, HEAD_B, HEAD_B), F32)],
        compiler_params=_cparams(("arbitrary", "arbitrary"), VMEM_BIG),
        name="rwkv_scan",
    )(tok(r), tok(kkn), tok(v), tok2(lw), tok2(bt), tok2(kd), tok(r), tok(kkn), tok(v), tok2(lw), tok2(bt), tok2(kd), st)
    return y0.reshape(n, db), y1.reshape(n, db), sfin


def _log_sigmoid(x):
    return -_softplus(-x)


def _split3(x):
    hi = x.astype(BF16)
    r1 = x - hi.astype(F32)
    mid = r1.astype(BF16)
    return hi, mid, (r1 - mid.astype(F32)).astype(BF16)


def _gla_chunk(d, c, nh, dk, dv, p_ref, wup, bgk, s_heads):
    _, incl, _ = _tri_masks(c, d)
    hk = nh * dk
    q = p_ref[:, 0:hk] * dk ** -0.5
    k = p_ref[:, hk:2 * hk]
    gk = _dot3(_split(p_ref[:, 2 * hk + nh * dv:]), _split(wup)) + bgk
    la = _log_sigmoid(gk) / GATE_NORM
    la3 = _split3(la)
    mask_bf = incl.astype(BF16)
    cs = _dot(mask_bf, la3[0]) + _dot(mask_bf, la3[1]) + _dot(mask_bf, la3[2])
    tot = cs[c - 1:c] if d == 0 else cs[0:1]
    q_t = (q * jnp.exp(cs)).astype(BF16)
    k_t = (k * jnp.exp(-cs)).astype(BF16)
    k_e = _split(k * jnp.exp(tot - cs))
    ones = jnp.ones((c, dv), BF16)
    dec_all = jnp.exp(_dot(la3[0], ones, 0, 0) + _dot(la3[1], ones, 0, 0) + _dot(la3[2], ones, 0, 0))

    def head(h):
        sk = (slice(None), slice(h * dk, (h + 1) * dk))
        vh = _split(p_ref[:, 2 * hk + h * dv:2 * hk + (h + 1) * dv])
        s0 = s_heads[h]
        qh = q_t[sk]
        att = jnp.where(incl, _dot(qh, k_t[sk], 1, 1), 0.0)
        yield
        inter = _dot(qh, s0.astype(BF16))
        yield
        o_out = _dot(att.astype(BF16), vh[0]) + inter
        yield
        yield (o_out, s0 * dec_all[h * dk:(h + 1) * dk] + _dot3(_cut(k_e, sk), vh, 0, 0))

    return [head(h) for h in range(nh)]


def _gla_kernel(p0_ref, p1_ref, s0_ref, wup_ref, bgk_ref, o0_ref, o1_ref, sfin_ref, s_ref, *, c, nh, dk, dv, ns):
    @pl.when(pl.program_id(1) == 0)
    def _():
        s_ref[...] = s0_ref[...]

    gens = []
    for s in range(ns):
        ga = _gla_chunk(0, c, nh, dk, dv, p0_ref.at[s], wup_ref[0], bgk_ref[0], [s_ref[s, 0, h] for h in range(nh)])
        gb = _gla_chunk(1, c, nh, dk, dv, p1_ref.at[s], wup_ref[1], bgk_ref[1], [s_ref[s, 1, h] for h in range(nh)])
        gens.append((ga, gb))
    order = [g for h in range(nh) for ga, gb in gens for g in (ga[h], gb[h])]
    res = iter(_staged(order))
    out = [[[None] * nh, [None] * nh] for _ in range(ns)]
    for h in range(nh):
        for s in range(ns):
            out[s][0][h] = next(res)
            out[s][1][h] = next(res)
    for s in range(ns):
        o0_ref[s] = jnp.concatenate([o for o, _ in out[s][0]], axis=1)
        o1_ref[s] = jnp.concatenate([o for o, _ in out[s][1]], axis=1)
        for d in range(2):
            for h in range(nh):
                s_ref[s, d, h] = out[s][d][h][1]
                sfin_ref[s, d, h] = out[s][d][h][1]


def _gla(p, st, o_, wup, bgk, bsz, seq, grid_mode):
    nh, dk, dv = st.shape[3:]
    hv = nh * dv
    nc = seq // CH
    ns = SEQ_PER_STEP if bsz % SEQ_PER_STEP == 0 else 1
    grp = bsz // ns
    if grid_mode:
        assert seq // GRID_W == CH and nc == GRID_W
        ncol = p.shape[2] // GRID_W
        p_v = p.reshape(grp, ns, CH, GRID_W * ncol)
        blk = lambda w: (None, ns, CH, w)
        fwd = lambda g, c: (g, 0, 0, c)
        bwd = lambda g, c: (g, 0, 0, nc - 1 - c)
        o_shape = jax.ShapeDtypeStruct((grp, ns, CH, GRID_W * hv), F32)
    else:
        ncol = p.shape[1]
        p_v = p.reshape(grp, ns, nc, CH, ncol)
        blk = lambda w: (None, ns, None, CH, w)
        fwd = lambda g, c: (g, 0, c, 0, 0)
        bwd = lambda g, c: (g, 0, nc - 1 - c, 0, 0)
        o_shape = jax.ShapeDtypeStruct((grp, ns, nc, CH, hv), F32)
    st_in = pl.BlockSpec((ns, None, 2, nh, dk, dv), lambda g, c: (g, o_, 0, 0, 0, 0))
    st_out = pl.BlockSpec((ns, 2, nh, dk, dv), lambda g, c: (g, 0, 0, 0, 0))
    o0, o1, sfin = pl.pallas_call(
        functools.partial(_gla_kernel, c=CH, nh=nh, dk=dk, dv=dv, ns=ns),
        grid=(grp, nc),
        in_specs=[pl.BlockSpec(blk(ncol), fwd), pl.BlockSpec(blk(ncol), bwd), st_in,
                  pl.BlockSpec(wup.shape, lambda g, c: (0, 0, 0)),
                  pl.BlockSpec(bgk.shape, lambda g, c: (0, 0, 0))],
        out_specs=[pl.BlockSpec(blk(hv), fwd), pl.BlockSpec(blk(hv), bwd), st_out],
        out_shape=[o_shape, o_shape, jax.ShapeDtypeStruct((bsz, 2, nh, dk, dv), F32)],
        scratch_shapes=[pltpu.VMEM((ns, 2, nh, dk, dv), F32)],
        compiler_params=_cparams(("arbitrary", "arbitrary"), VMEM_BIG),
        name="gla",
    )(p_v, p_v, st, wup, bgk)
    if grid_mode:
        return o0.reshape(bsz, CH, GRID_W * hv), o1.reshape(bsz, CH, GRID_W * hv), sfin
    return o0.reshape(bsz * seq, hv), o1.reshape(bsz * seq, hv), sfin


def _router_tail(x1, mod, g2n_ref, rw_ref, x_out, h2_out, aff_out):
    x_out[...] = x1
    h2 = _rms(x1, g2n_ref[...]) * (1.0 + mod[4:5]) + mod[3:4]
    h2_out[...] = h2.astype(BF16)
    logits = _dot3(_split(rw_ref[...]), _split(h2), 1, 1)
    m = jnp.max(logits, axis=0, keepdims=True)
    ex = jnp.exp(logits - m)
    aff_out[...] = ex / jnp.sum(ex, axis=0, keepdims=True)


def _out_even_kernel(x_ref, mod_ref, ha0_ref, ha1_ref, ga_ref, y0_ref, y1_ref, g_ref, bonus_ref, lnw_ref, lnb_ref,
                     e_ref, wo_ref, g2n_ref, rw_ref, x_out, h2_out, aff_out, *, da, hd):
    mod = mod_ref[...]
    ya = (ha0_ref[...] + ha1_ref[...]) * _gelu_tanh(ga_ref[...])
    y = y0_ref[...] + y1_ref[...]
    e = e_ref[...]
    mean = _seg_sum(y, e) * (1.0 / hd)
    yc = y - mean
    var = _seg_sum(yc * yc, e) * (1.0 / hd)
    yn = yc * lax.rsqrt(var + GN_EPS_B) * lnw_ref[...] + lnb_ref[...]
    yb = (yn + bonus_ref[...]) * g_ref[...]
    out = _dot_bf(ya, wo_ref[0:da, :]) + _dot_bf(yb, wo_ref[da:, :])
    _router_tail(x_ref[...] + mod[2:3] * out, mod, g2n_ref, rw_ref, x_out, h2_out, aff_out)


def _out_odd_kernel(x_ref, mod_ref, o0_ref, o1_ref, g_ref, gw_ref, wo_ref, g2n_ref, rw_ref,
                    x_out, h2_out, aff_out, *scratch, nh, dv, grid_rows):
    mod = mod_ref[...]
    gw = gw_ref[...]
    if grid_rows:
        hv = nh * dv
        o_scr = scratch[0]
        for c in range(GRID_W):
            for j in range(hv // LANES):
                lanes = slice(c * hv + j * LANES, c * hv + (j + 1) * LANES)
                o_scr[j, pl.ds(c, grid_rows, stride=GRID_W), :] = o0_ref[:, lanes] + o1_ref[:, lanes]
        o_all = jnp.concatenate([o_scr[j] for j in range(hv // LANES)], axis=1)
    else:
        o_all = o0_ref[...] + o1_ref[...]
    parts = []
    for h in range(nh):
        o = o_all[:, h * dv:(h + 1) * dv]
        parts.append(o * lax.rsqrt(jnp.mean(o * o, axis=-1, keepdims=True) + EPS) * gw)
    y = jnp.concatenate(parts, axis=1) * _silu(g_ref[...])
    out = _dot_bf(y, wo_ref[...])
    _router_tail(x_ref[...] + mod[2:3] * out, mod, g2n_ref, rw_ref, x_out, h2_out, aff_out)


def _out_common(kern, x, modr, mod_of, extra_args, extra_specs, norm2_g, router_wt, tb, name, scratch=()):
    n, d = x.shape
    ne = router_wt.shape[0]
    assert n % tb == 0
    mod_blk = mod_of(tb)
    row = lambda i: (i, 0)
    in_specs = ([pl.BlockSpec((tb, d), row), pl.BlockSpec((None, 6, d), lambda i: (mod_blk(i), 0, 0))]
                + extra_specs
                + [pl.BlockSpec((1, d), lambda i: (0, 0)), pl.BlockSpec((ne, d), lambda i: (0, 0))])
    return pl.pallas_call(
        kern, grid=(n // tb,), in_specs=in_specs,
        out_specs=[pl.BlockSpec((tb, d), row), pl.BlockSpec((tb, d), row),
                   pl.BlockSpec((ne, tb), lambda i: (0, i))],
        out_shape=[jax.ShapeDtypeStruct((n, d), F32), jax.ShapeDtypeStruct((n, d), BF16),
                   jax.ShapeDtypeStruct((ne, n), F32)],
        scratch_shapes=list(scratch),
        compiler_params=_cparams(("arbitrary",), VMEM_BIG),
        name=name,
    )(x, modr, *extra_args, norm2_g.reshape(1, d), router_wt)


def _select_kernel(aff_ref, slot_ref, cnt_ref, *, cap, ntb, tokb):
    aff = aff_ref[...]
    ne = aff.shape[0]
    big = jnp.float32(4.0)

    def body(carry):
        lo, hi, _, _ = carry
        inr = (aff > lo) & (aff <= hi)
        mn = jnp.min(jnp.where(inr, aff, big), axis=1, keepdims=True)
        mx = jnp.max(jnp.where(inr, aff, -big), axis=1, keepdims=True)
        done = mn >= mx
        piv = mn + (mx - mn) * 0.5
        piv = jnp.where(piv >= mx, mn, piv)
        ge = jnp.sum((aff > piv).astype(jnp.int32), axis=1, keepdims=True) >= cap
        lo = jnp.where(done | ~ge, lo, piv)
        hi = jnp.where(done | ge, hi, piv)
        return lo, hi, mx, jnp.sum((~done).astype(jnp.int32))

    init = (jnp.full((ne, 1), -1.0, F32), jnp.full((ne, 1), big, F32), jnp.zeros((ne, 1), F32), jnp.int32(1))
    _, _, thr, _ = lax.while_loop(lambda carry: carry[3] > 0, body, init)
    gt = aff > thr
    eq = aff == thr
    need = cap - jnp.sum(gt.astype(jnp.int32), axis=1, keepdims=True)
    si = lax.broadcasted_iota(jnp.int32, (tokb, tokb), 0)
    ti = lax.broadcasted_iota(jnp.int32, (tokb, tokb), 1)
    ustrict = (si < ti).astype(BF16)

    def excl_count(mask_blk, carry):
        m = mask_blk.astype(BF16)
        pre = _dot(m, ustrict).astype(jnp.int32) + carry
        return pre, carry + jnp.sum(mask_blk.astype(jnp.int32), axis=1, keepdims=True)

    ceq = jnp.zeros((ne, 1), jnp.int32)
    csel = jnp.zeros((ne, 1), jnp.int32)
    lane = lax.broadcasted_iota(jnp.int32, (ne, 128), 1)
    cnt_acc = jnp.zeros((ne, 128), jnp.int32)
    for j in range(ntb):
        sl = slice(j * tokb, (j + 1) * tokb)
        eq_rank, ceq = excl_count(eq[:, sl], ceq)
        sel = gt[:, sl] | (eq[:, sl] & (eq_rank < need))
        cnt_acc = jnp.where(lane == j, csel, cnt_acc)
        rank, csel = excl_count(sel, csel)
        slot_ref[:, sl] = jnp.where(sel, rank, -1)
    cnt_acc = jnp.where(lane == ntb, csel, cnt_acc)
    cnt_ref[...] = cnt_acc


def _select(aff_t, cap):
    ne, n = aff_t.shape
    ntb = n // TOK_B
    assert ntb < 128
    return pl.pallas_call(
        functools.partial(_select_kernel, cap=cap, ntb=ntb, tokb=TOK_B),
        grid=(1,),
        in_specs=[pl.BlockSpec((ne, n), lambda i: (0, 0))],
        out_specs=[pl.BlockSpec((ne, n), lambda i: (0, 0)), pl.BlockSpec((ne, 128), lambda i: (0, 0))],
        out_shape=[jax.ShapeDtypeStruct((ne, n), jnp.int32), jax.ShapeDtypeStruct((ne, 128), jnp.int32)],
        compiler_params=_cparams(("arbitrary",)),
        name="select",
    )(aff_t)


def _windows(cnt_ref, tb, ntb, ne, cap, w, rnd):
    out = []
    for e in range(ne):
        first = (cnt_ref[e * (ntb + 1) + tb] // SLOT_ALIGN) * SLOT_ALIGN + rnd * w
        out.append((pl.multiple_of(jnp.minimum(first, cap - w), SLOT_ALIGN), first))
    return out


def _onehot_rows(slot_ref, e, base, first, w, tokb):
    sidx = lax.broadcasted_iota(jnp.int32, (w, tokb), 0) + base
    return (slot_ref[e:e + 1, :] == sidx) & (sidx >= first)


def _gather_kernel(cnt_ref, nr_ref, slot_ref, h2_ref, xe_ref, lhs_ref, *, ntb, ne, cap, w, tokb):
    tb = pl.program_id(0)

    @pl.when(tb == 0)
    def _():
        xe_ref[...] = jnp.zeros_like(xe_ref)

    def body(rnd, carry):
        wins = _windows(cnt_ref, tb, ntb, ne, cap, w, rnd)
        for e, (base, first) in enumerate(wins):
            lhs_ref[e * w:(e + 1) * w, :] = jnp.where(_onehot_rows(slot_ref, e, base, first, w, tokb),
                                                      1.0, 0.0).astype(BF16)
        rows = _dot(lhs_ref[...], h2_ref[...])
        for e, (base, _) in enumerate(wins):
            xe_ref[e, pl.ds(base, w), :] += rows[e * w:(e + 1) * w].astype(BF16)
        return carry

    lax.fori_loop(0, nr_ref[tb], body, 0)


def _gather(cnt, nr, slot_t, h2, cap):
    ntb, ne, tokb = slot_t.shape
    n, d = h2.shape
    return pl.pallas_call(
        functools.partial(_gather_kernel, ntb=ntb, ne=ne, cap=cap, w=SLOT_W, tokb=tokb),
        grid_spec=pltpu.PrefetchScalarGridSpec(
            num_scalar_prefetch=2, grid=(ntb,),
            in_specs=[pl.BlockSpec((None, ne, tokb), lambda t, cnt, nr: (t, 0, 0)),
                      pl.BlockSpec((tokb, d), lambda t, cnt, nr: (t, 0))],
            out_specs=pl.BlockSpec((ne, cap, d), lambda t, cnt, nr: (0, 0, 0), pipeline_mode=pl.Buffered(1)),
            scratch_shapes=[pltpu.VMEM((ne * SLOT_W, tokb), BF16)]),
        out_shape=jax.ShapeDtypeStruct((ne, cap, d), BF16),
        compiler_params=_cparams(("arbitrary",), VMEM_BIG),
        name="moe_gather",
    )(cnt, nr, slot_t, h2)


def _ffn_kernel(xe_ref, wg_ref, wu_ref, wd_ref, ye_ref, acc_ref):
    f = pl.program_id(1)
    xe = xe_ref[...]
    hg = _dot(xe, wg_ref[...].astype(BF16))
    hu = _dot(xe, wu_ref[...].astype(BF16))
    hid = (_silu(hg) * hu).astype(BF16)
    part = _dot(hid, wd_ref[...].astype(BF16))

    @pl.when(f == 0)
    def _():
        acc_ref[...] = part

    @pl.when(f > 0)
    def _():
        acc_ref[...] += part

    @pl.when(f == pl.num_programs(1) - 1)
    def _():
        ye_ref[...] = acc_ref[...].astype(BF16)


def _ffn(xe, w_gate, w_up, w_down, l):
    ne, cap, d = xe.shape
    dff = w_gate.shape[-1]
    nf = dff // FF_B
    return pl.pallas_call(
        _ffn_kernel,
        grid=(ne, nf),
        in_specs=[pl.BlockSpec((None, cap, d), lambda e, f: (e, 0, 0)),
                  pl.BlockSpec((None, None, d, FF_B), lambda e, f: (l, e, 0, f)),
                  pl.BlockSpec((None, None, d, FF_B), lambda e, f: (l, e, 0, f)),
                  pl.BlockSpec((None, None, FF_B, d), lambda e, f: (l, e, f, 0))],
        out_specs=pl.BlockSpec((None, cap, d), lambda e, f: (e, 0, 0)),
        out_shape=jax.ShapeDtypeStruct((ne, cap, d), BF16),
        scratch_shapes=[pltpu.VMEM((cap, d), F32)],
        compiler_params=_cparams(("arbitrary", "arbitrary"), VMEM_BIG),
        name="moe_ffn",
    )(xe, w_gate, w_up, w_down)


def _scatter_kernel(cnt_ref, nr_ref, slot_ref, gate_ref, ye_ref, x_ref, mod_ref, fg_ref, out_ref, g_ref, ywin_ref,
                    *, ntb, ne, cap, w, tokb, final):
    tb = pl.program_id(0)
    out_ref[...] = jnp.zeros_like(out_ref)

    def body(rnd, carry):
        wins = _windows(cnt_ref, tb, ntb, ne, cap, w, rnd)
        for e, (base, first) in enumerate(wins):
            g_ref[e * w:(e + 1) * w, :] = jnp.where(_onehot_rows(slot_ref, e, base, first, w, tokb),
                                                    gate_ref[e:e + 1, :], 0.0).astype(BF16)
            ywin_ref[e * w:(e + 1) * w, :] = ye_ref[e, pl.ds(base, w), :]
        out_ref[...] += _dot(g_ref[...], ywin_ref[...], 0, 0)
        return carry

    lax.fori_loop(0, nr_ref[tb], body, 0)
    x = x_ref[...] + mod_ref[...][5:6] * out_ref[...]
    out_ref[...] = _rms(x, fg_ref[...]) if final else x


def _scatter(cnt, nr, slot_t, gate_t, ye, x, modr, mod_tok, final_gain, final):
    ne, cap, d = ye.shape
    ntb, _, tokb = slot_t.shape
    n = ntb * tokb
    tok = pl.BlockSpec((None, ne, tokb), lambda t, cnt, nr: (t, 0, 0))
    row = pl.BlockSpec((tokb, d), lambda t, cnt, nr: (t, 0))
    return pl.pallas_call(
        functools.partial(_scatter_kernel, ntb=ntb, ne=ne, cap=cap, w=SLOT_W, tokb=tokb, final=final),
        grid_spec=pltpu.PrefetchScalarGridSpec(
            num_scalar_prefetch=2, grid=(ntb,),
            in_specs=[tok, tok,
                      pl.BlockSpec((ne, cap, d), lambda t, cnt, nr: (0, 0, 0), pipeline_mode=pl.Buffered(1)),
                      row,
                      pl.BlockSpec((None, 6, d), lambda t, cnt, nr: (mod_tok(t), 0, 0)),
                      pl.BlockSpec((1, d), lambda t, cnt, nr: (0, 0))],
            out_specs=row,
            scratch_shapes=[pltpu.VMEM((ne * SLOT_W, tokb), BF16), pltpu.VMEM((ne * SLOT_W, d), BF16)]),
        out_shape=jax.ShapeDtypeStruct((n, d), F32),
        compiler_params=_cparams(("arbitrary",), VMEM_BIG),
        name="moe_scatter",
    )(cnt, nr, slot_t, gate_t, ye, x, modr, final_gain.reshape(1, d))


def _moe(h2, aff_t, w_gate, w_up, w_down, l, x, modr, mod_tok, final_gain, final):
    n, d = h2.shape
    ne = aff_t.shape[0]
    cap = max(1, CAP_FACTOR * n // ne)
    assert cap % SLOT_W == 0 and n % TOK_B == 0
    ntb = n // TOK_B
    slot, cnt = _select(aff_t, cap)
    cnt = cnt[:, :ntb + 1]
    span = cnt[:, 1:] - (cnt[:, :-1] // SLOT_ALIGN) * SLOT_ALIGN
    nr = jnp.max(jnp.where(cnt[:, 1:] > cnt[:, :-1], (span + SLOT_W - 1) // SLOT_W, 0), axis=0).astype(jnp.int32)
    slot_t = slot.reshape(ne, ntb, TOK_B).transpose(1, 0, 2)
    gate_t = aff_t.reshape(ne, ntb, TOK_B).transpose(1, 0, 2)
    xe = _gather(cnt.reshape(-1), nr, slot_t, h2, cap)
    ye = _ffn(xe, w_gate, w_up, w_down, l)
    return _scatter(cnt.reshape(-1), nr, slot_t, gate_t, ye, x, modr, mod_tok, final_gain, final)


def _block_diag(w):
    h, n, _ = w.shape
    eye = jnp.eye(h, dtype=w.dtype)
    return (eye[:, None, :, None] * w[:, :, None, :]).reshape(h * n, h * n)


def _lora_cat(w2):
    _, l, c = w2.shape
    z = jnp.zeros((l, c), w2.dtype)
    return jnp.concatenate([jnp.concatenate([w2[0], z], 1), jnp.concatenate([z, w2[1]], 1)], 0)


def _trunk(x, bsz, seq, cond_base, per_batch, st_rglru, st_rwkv, st_gla, grid_mode, modr, P):
    d_model = x.shape[1]
    depth = P['norm1_g'].shape[0]
    new_rglru, new_rwkv, new_gla = [], [], []
    for l in range(depth):
        if per_batch:
            mod_of = lambda tb, l=l: (lambda i: l * 8 + cond_base + (i * tb) // seq)
        else:
            mod_of = lambda tb, l=l: (lambda i: l * 8 + cond_base)
        router_wt = P['router_w'][l].T
        if l % 2 == 0:
            e = l // 2
            da = P['a_conv_w'].shape[-1]
            db = P['b_k_k'].shape[-1]
            w_in = P['ev_w_in'][e].astype(BF16)
            nb_cols = w_in.shape[1] - 2 * da
            pa, pb = _inproj(x, modr, mod_of, P['norm1_g'][l], w_in, (2 * da, nb_cols), bsz, seq)
            ha, sa = [], []
            for d in range(2):
                wg = jnp.concatenate([_block_diag(P['a_w_rg'][e, d]), _block_diag(P['a_w_ig'][e, d])], 1).astype(BF16)
                bg = jnp.concatenate([P['a_b_rg'][e, d], P['a_b_ig'][e, d]])
                h_d, hf = _rglru(pa, P['a_conv_w'][e], P['a_conv_b'][e], wg, bg, P['a_lam'][e, d],
                                 st_rglru[:, e, d], d, bsz, seq)
                ha.append(h_d)
                sa.append(hf.reshape(bsz, da))
            new_rglru.append(jnp.stack(sa, 1))
            e_bf = _block_diag(jnp.ones((db // HEAD_B, HEAD_B, HEAD_B), BF16))
            r, v, kkn, g, bonus, lw, bt, kd = _rwkv_prep(
                pb, P['b_mu'][e], P['b_w0'][e].reshape(1, 2 * db), _lora_cat(P['b_w2'][e]),
                P['b_a0'][e].reshape(1, 2 * db), _lora_cat(P['b_a2'][e]), P['b_g2'][e].astype(BF16),
                P['b_k_k'][e], P['b_k_a'][e], P['b_r_k'][e], e_bf, bsz, seq)
            y0, y1, sfin = _rwkv_scan(r, kkn, v, lw, bt, kd, st_rwkv, e, bsz, seq)
            ys = [y0, y1]
            new_rwkv.append(sfin)
            tb = _tok_block(False)
            rowh = pl.BlockSpec((tb, da), lambda i: (i, 0))
            rowg = pl.BlockSpec((tb, da), lambda i: (i, 1))
            vec = pl.BlockSpec((1, db), lambda i: (0, 0))
            x, h2, aff_t = _out_common(
                functools.partial(_out_even_kernel, da=da, hd=HEAD_B), x, modr, mod_of,
                [ha[0], ha[1], pa, ys[0], ys[1], g, bonus, P['b_ln_w'][e].reshape(1, db), P['b_ln_b'][e].reshape(1, db),
                 e_bf, P['ev_w_out'][e].astype(BF16)],
                [rowh, rowh, rowg, rowh, rowh, rowh, rowh, vec, vec,
                 pl.BlockSpec((db, db), lambda i: (0, 0)),
                 pl.BlockSpec((da + db, d_model), lambda i: (0, 0))],
                P['norm2_g'][l], router_wt, tb, "out_even")
        else:
            o_ = l // 2
            w_in = P['od_w_in'][o_]
            nh, dk, dv = st_gla.shape[3:]
            hv = nh * dv
            qkv = 2 * nh * dk + hv
            gin = -(2 * GATE_RANK) % 128 + 2 * GATE_RANK
            w_in = jnp.concatenate([w_in[:, :qkv], w_in[:, qkv + hv:],
                                    jnp.zeros((d_model, gin - 2 * GATE_RANK), w_in.dtype),
                                    w_in[:, qkv:qkv + hv]], axis=1).astype(BF16)
            p, g = _inproj(x, modr, mod_of, P['norm1_g'][l], w_in, (qkv + gin, hv), bsz, seq, grid_view=grid_mode)
            wup = jnp.stack([jnp.zeros((gin, nh * dk), F32).at[d * GATE_RANK:(d + 1) * GATE_RANK]
                             .set(P['c_w_gk_up'][o_, d]) for d in range(2)])
            o0, o1, sfin = _gla(p, st_gla, o_, wup, P['c_b_gk'][o_].reshape(2, 1, nh * dk), bsz, seq, grid_mode)
            new_gla.append(sfin)
            tb = _tok_block(grid_mode)
            nt = seq // tb
            if grid_mode:
                rowo = pl.BlockSpec((None, SUBLANES, GRID_W * hv), lambda i: (i // nt, i % nt, 0))
                scratch = [pltpu.VMEM((hv // LANES, tb, LANES), F32)]
            else:
                rowo = pl.BlockSpec((tb, hv), lambda i: (i, 0))
                scratch = []
            rowg = pl.BlockSpec((tb, hv), lambda i: (i, 0))
            x, h2, aff_t = _out_common(
                functools.partial(_out_odd_kernel, nh=nh, dv=dv, grid_rows=SUBLANES if grid_mode else 0),
                x, modr, mod_of,
                [o0, o1, g, P['c_gnorm_w'][o_].reshape(1, dv), P['od_w_out'][o_].astype(BF16)],
                [rowo, rowo, rowg, pl.BlockSpec((1, dv), lambda i: (0, 0)),
                 pl.BlockSpec((hv, d_model), lambda i: (0, 0))],
                P['norm2_g'][l], router_wt, tb, "out_odd", scratch=scratch)
        x = _moe(h2, aff_t, P['moe_w_gate'], P['moe_w_up'], P['moe_w_down'], l, x, modr, mod_of(TOK_B),
                 P['final_norm_g'], l == depth - 1)
    return x, jnp.stack(new_rglru, 1), jnp.stack(new_rwkv, 1), jnp.stack(new_gla, 1)


def kernel(x_prompt, x_sample, state_rglru, state_rwkv, state_gla, c, c_ctx, norm1_g, norm2_g, ada_w, ada_b, router_w, moe_w_gate, moe_w_up, moe_w_down, ev_w_in, ev_w_out, a_conv_w, a_conv_b, a_w_rg, a_b_rg, a_w_ig, a_b_ig, a_lam, b_mu, b_w0, b_w2, b_a0, b_a2, b_g2, b_k_k, b_k_a, b_r_k, b_ln_w, b_ln_b, od_w_in, od_w_out, c_w_gk_up, c_b_gk, c_gnorm_w, final_norm_g):
    P = dict(norm1_g=norm1_g, norm2_g=norm2_g, router_w=router_w,
             moe_w_gate=moe_w_gate, moe_w_up=moe_w_up, moe_w_down=moe_w_down,
             ev_w_in=ev_w_in, ev_w_out=ev_w_out, a_conv_w=a_conv_w, a_conv_b=a_conv_b,
             a_w_rg=a_w_rg, a_b_rg=a_b_rg, a_w_ig=a_w_ig, a_b_ig=a_b_ig, a_lam=a_lam,
             b_mu=b_mu, b_w0=b_w0, b_w2=b_w2, b_a0=b_a0, b_a2=b_a2, b_g2=b_g2, b_k_k=b_k_k,
             b_k_a=b_k_a, b_r_k=b_r_k, b_ln_w=b_ln_w, b_ln_b=b_ln_b,
             od_w_in=od_w_in, od_w_out=od_w_out, c_w_gk_up=c_w_gk_up, c_b_gk=c_b_gk,
             c_gnorm_w=c_gnorm_w, final_norm_g=final_norm_g)
    bp, tp, d_model = x_prompt.shape
    bs, ts, _ = x_sample.shape
    assert bs + 1 <= 8 and tp % TB == 0 and ts % TB == 0
    depth = ada_w.shape[0]
    cv8 = jnp.zeros((8, d_model), F32).at[0].set(c_ctx).at[1:1 + bs].set(c)
    modr = _modulation(cv8, ada_w, ada_b).reshape(depth * 8, 6, d_model)
    dt = x_prompt.dtype
    z_rglru = jnp.zeros((bp,) + state_rglru.shape[1:], dt)
    z_rwkv = jnp.zeros((bp,) + state_rwkv.shape[1:], dt)
    z_gla = jnp.zeros((bp,) + state_gla.shape[1:], dt)
    y_prompt, n_rglru, n_rwkv, n_gla = _trunk(x_prompt.reshape(bp * tp, d_model), bp, tp, 0, False,
                                              z_rglru, z_rwkv, z_gla, False, modr, P)
    y_sample, _, _, _ = _trunk(x_sample.reshape(bs * ts, d_model), bs, ts, 1, True,
                               state_rglru, state_rwkv, state_gla, True, modr, P)
    return (y_prompt.reshape(bp, tp, d_model), y_sample.reshape(bs, ts, d_model), n_rglru, n_rwkv, n_gla)
```
